```python
import jax, jax.numpy as jnp
from jax import lax
import numpy as np

D_MODEL = 4096
BATCH = 1
SEQ = 8192
DEPTH = 1

GRID_W = 64
CTX_LEN = 256
EPS = 1e-6

D_MIX = D_MODEL
D_GLA = D_MIX // 2
D_CONV = D_MIX - D_GLA
GLA_HEADS = 16
GLA_DK = D_GLA // 2
GLA_HK = GLA_DK // GLA_HEADS
GLA_HV = D_GLA // GLA_HEADS
GATE_RANK = 16
GATE_TAU = 16.0
GLA_CHUNK = 64
CONV_WIDTH = 31
CONV_PAD = (CONV_WIDTH - 1) // 2

OFF_Q = 0
OFF_K = OFF_Q + GLA_DK
OFF_V = OFF_K + GLA_DK
OFF_R = OFF_V + D_GLA
OFF_GF = OFF_R + D_GLA
OFF_GB = OFF_GF + GATE_RANK
OFF_CA = OFF_GB + GATE_RANK
OFF_CB = OFF_CA + D_CONV
IN_COLS = OFF_CB + D_CONV

N_EXPERTS = 64
D_EXPERT = 512
D_SHARED = 512
TOP_K = 8
N_GROUPS = 8
TOPK_GROUPS = 4
ROUTED_SCALE = 2.5
MOE_BLOCK = 128

kernel_name = "hybrid_gla_conformer_moe_dit"


def rmsnorm(x, g):
    xf = x.astype(jnp.float32)
    y = xf * lax.rsqrt(jnp.mean(xf * xf, axis=-1, keepdims=True) + EPS) * g.astype(jnp.float32)
    return y.astype(x.dtype)


def modulate(h, shift, scale):
    return h * (1 + scale) + shift


def gla_scan(q, k, v, logf, s0, with_output):
    B, H, T, dk = q.shape
    L = GLA_CHUNK
    nc = T // L
    f32 = jnp.float32

    def to_chunks(a):
        return jnp.moveaxis(a.astype(f32).reshape(B, H, nc, L, a.shape[-1]), 2, 0)

    mask = jnp.tril(jnp.ones((L, L), dtype=bool))[:, :, None]

    def step(S, inp):
        qc, kc, vc, gc = inp
        b = jnp.cumsum(gc, axis=2)
        bL = b[:, :, -1, :]
        S_new = jnp.exp(bL)[..., None] * S + jnp.einsum(
            'bhsd,bhse->bhde', kc * jnp.exp(bL[:, :, None, :] - b), vc)
        if not with_output:
            return S_new, None
        o_inter = jnp.einsum('bhtd,bhde->bhte', qc * jnp.exp(b), S)
        diff = b[:, :, :, None, :] - b[:, :, None, :, :]
        decay = jnp.exp(jnp.where(mask, diff, -jnp.inf))
        att = jnp.einsum('bhtd,bhsd,bhtsd->bhts', qc, kc, decay)
        o = o_inter + jnp.einsum('bhts,bhse->bhte', att, vc)
        return S_new, o

    S, o = lax.scan(step, s0.astype(f32), (to_chunks(q), to_chunks(k), to_chunks(v), to_chunks(logf)))
    if not with_output:
        return None, S
    o = jnp.moveaxis(o, 0, 2).reshape(B, H, T, v.shape[-1])
    return o, S


def gla_branch(proj, w_gate_up, b_gate_up, g_out, s_fwd, s_bwd, with_output):
    B, T, _ = proj.shape

    def heads(a, dh):
        return a.reshape(B, T, GLA_HEADS, dh).transpose(0, 2, 1, 3)

    q = heads(proj[..., OFF_Q:OFF_K], GLA_HK) * GLA_HK ** -0.5
    k = heads(proj[..., OFF_K:OFF_V], GLA_HK)
    v = heads(proj[..., OFF_V:OFF_R], GLA_HV)

    def log_decay(low, d):
        z = (low @ w_gate_up[d] + b_gate_up[d]).astype(jnp.float32)
        return heads(jax.nn.log_sigmoid(z) / GATE_TAU, GLA_HK)

    lf = log_decay(proj[..., OFF_GF:OFF_GB], 0)
    lb = log_decay(proj[..., OFF_GB:OFF_CA], 1)
    flip = lambda a: jnp.flip(a, axis=2)
    o_f, s_f = gla_scan(q, k, v, lf, s_fwd, with_output)
    o_b, s_b = gla_scan(flip(q), flip(k), flip(v), flip(lb), s_bwd, with_output)
    if not with_output:
        return None, (s_f, s_b)
    o = o_f + flip(o_b)
    o = o * lax.rsqrt(jnp.mean(o * o, axis=-1, keepdims=True) + EPS)
    o = o.transpose(0, 2, 1, 3).reshape(B, T, D_GLA) * g_out.astype(jnp.float32)
    r = proj[..., OFF_R:OFF_GF].astype(jnp.float32)
    return (o * jax.nn.silu(r)).astype(proj.dtype), (s_f, s_b)


def conv_branch(proj, w_dw, b_dw, g_ln, b_ln, rows):
    u = proj[..., OFF_CA:OFF_CB] * jax.nn.sigmoid(proj[..., OFF_CB:IN_COLS])
    B, T, C = u.shape
    seqs = u.reshape(B * rows, GRID_W, C) if rows is not None else u
    y = lax.conv_general_dilated(seqs, w_dw[:, None, :].astype(seqs.dtype), window_strides=(1,),
                                 padding=[(CONV_PAD, CONV_PAD)],
                                 dimension_numbers=('NWC', 'WIO', 'NWC'), feature_group_count=C)
    y = y.reshape(B, T, C).astype(jnp.float32) + b_dw.astype(jnp.float32)
    mu = jnp.mean(y, axis=-1, keepdims=True)
    var = jnp.mean(jnp.square(y - mu), axis=-1, keepdims=True)
    y = (y - mu) * lax.rsqrt(var + EPS) * g_ln.astype(jnp.float32) + b_ln.astype(jnp.float32)
    return jax.nn.silu(y).astype(proj.dtype)


def moe(h, w_router, b_router, w_e_gate, w_e_up, w_e_down, w_s_gate, w_s_up, w_s_down):
    B, T, D = h.shape
    N = B * T
    xt = h.reshape(N, D)
    scores = jax.nn.sigmoid((xt @ w_router).astype(jnp.float32))
    sel = scores + b_router.astype(jnp.float32)
    grp_score = lax.top_k(sel.reshape(N, N_GROUPS, N_EXPERTS // N_GROUPS), 2)[0].sum(-1)
    _, gidx = lax.top_k(grp_score, TOPK_GROUPS)
    gmask = jax.nn.one_hot(gidx, N_GROUPS, dtype=jnp.float32).sum(-2) > 0
    emask = jnp.repeat(gmask, N_EXPERTS // N_GROUPS, axis=1)
    _, eidx = lax.top_k(jnp.where(emask, sel, -jnp.inf), TOP_K)
    w = jnp.take_along_axis(scores, eidx, axis=1)
    w = w / jnp.sum(w, axis=-1, keepdims=True) * ROUTED_SCALE

    NK = N * TOP_K
    NB = (NK + MOE_BLOCK - 1) // MOE_BLOCK + N_EXPERTS
    flat_e = eidx.reshape(-1).astype(jnp.int32)
    flat_tok = jnp.repeat(jnp.arange(N, dtype=jnp.int32), TOP_K)
    flat_w = w.reshape(-1)
    order = jnp.argsort(flat_e)
    se = flat_e[order]
    counts = jnp.bincount(flat_e, length=N_EXPERTS).astype(jnp.int32)
    start = jnp.cumsum(counts) - counts
    pcounts = (counts + MOE_BLOCK - 1) // MOE_BLOCK * MOE_BLOCK
    pend = jnp.cumsum(pcounts)
    pstart = pend - pcounts
    dest = pstart[se] + (jnp.arange(NK, dtype=jnp.int32) - start[se])
    slot_tok = jnp.full((NB * MOE_BLOCK,), N, dtype=jnp.int32).at[dest].set(flat_tok[order])
    slot_w = jnp.zeros((NB * MOE_BLOCK,), jnp.float32).at[dest].set(flat_w[order])
    blk_e = jnp.minimum(jnp.searchsorted(pend, jnp.arange(NB, dtype=jnp.int32) * MOE_BLOCK, side='right'),
                        N_EXPERTS - 1)
    xpad = jnp.concatenate([xt, jnp.zeros((1, D), xt.dtype)], axis=0)

    def body(y, inp):
        e, tok, wt = inp
        xb = xpad[tok]
        hb = jax.nn.silu(xb @ w_e_gate[e]) * (xb @ w_e_up[e])
        ob = (hb @ w_e_down[e]).astype(jnp.float32) * wt[:, None]
        return y.at[tok].add(ob), None

    y, _ = lax.scan(body, jnp.zeros((N + 1, D), jnp.float32),
                    (blk_e, slot_tok.reshape(NB, MOE_BLOCK), slot_w.reshape(NB, MOE_BLOCK)))
    shared = (jax.nn.silu(xt @ w_s_gate) * (xt @ w_s_up)) @ w_s_down
    return (y[:N] + shared.astype(jnp.float32)).astype(h.dtype).reshape(B, T, D)


def setup_inputs(seed: int = 0) -> dict:
    key = jax.random.key(seed)
    ks = jax.random.split(key, 28)
    f32 = jnp.float32
    nrm = lambda k, shape, s: jax.random.normal(k, shape, f32) * s
    L, D = DEPTH, D_MODEL
    return {
        "x": nrm(ks[0], (BATCH, SEQ, D), 1.0),
        "c": nrm(ks[1], (BATCH, D), 1.0),
        "ctx": nrm(ks[2], (BATCH, CTX_LEN, D), 1.0),
        "c_ctx": nrm(ks[3], (D,), 1.0),
        "w_ada": nrm(ks[4], (L, D, 6 * D), 0.5 * D ** -0.5),
        "b_ada": nrm(ks[5], (L, 6 * D), 0.02),
        "g_norm_mix": 1.0 + nrm(ks[6], (L, D), 0.02),
        "g_norm_ffn": 1.0 + nrm(ks[7], (L, D), 0.02),
        "w_in": nrm(ks[8], (L, D, IN_COLS), D ** -0.5),
        "w_gate_up": nrm(ks[9], (L, 2, GATE_RANK, GLA_DK), GATE_RANK ** -0.5),
        "b_gate_up": nrm(ks[10], (L, 2, GLA_DK), 0.1),
        "g_gla_out": 1.0 + nrm(ks[11], (L, D_GLA), 0.02),
        "w_dw": nrm(ks[12], (L, CONV_WIDTH, D_CONV), CONV_WIDTH ** -0.5),
        "b_dw": nrm(ks[13], (L, D_CONV), 0.02),
        "g_conv_ln": 1.0 + nrm(ks[14], (L, D_CONV), 0.02),
        "b_conv_ln": nrm(ks[15], (L, D_CONV), 0.02),
        "w_out": nrm(ks[16], (L, D_MIX, D), D_MIX ** -0.5),
        "w_router": nrm(ks[17], (L, D, N_EXPERTS), D ** -0.5),
        "b_router": nrm(ks[18], (L, N_EXPERTS), 0.01),
        "w_e_gate": nrm(ks[19], (L, N_EXPERTS, D, D_EXPERT), D ** -0.5),
        "w_e_up": nrm(ks[20], (L, N_EXPERTS, D, D_EXPERT), D ** -0.5),
        "w_e_down": nrm(ks[21], (L, N_EXPERTS, D_EXPERT, D), D_EXPERT ** -0.5),
        "w_s_gate": nrm(ks[22], (L, D, D_SHARED), D ** -0.5),
        "w_s_up": nrm(ks[23], (L, D, D_SHARED), D ** -0.5),
        "w_s_down": nrm(ks[24], (L, D_SHARED, D), D_SHARED ** -0.5),
        "g_final": 1.0 + nrm(ks[25], (D,), 0.02),
    }


def reference(x, c, ctx, c_ctx, w_ada, b_ada, g_norm_mix, g_norm_ffn, w_in, w_gate_up, b_gate_up,
              g_gla_out, w_dw, b_dw, g_conv_ln, b_conv_ln, w_out, w_router, b_router,
              w_e_gate, w_e_up, w_e_down, w_s_gate, w_s_up, w_s_down, g_final):
    B = x.shape[0]
    rows = x.shape[1] // GRID_W
    x_lat, x_ctx = x, ctx
    zero_state = jnp.zeros((B, GLA_HEADS, GLA_HK, GLA_HV), jnp.float32)
    for l in range(DEPTH):
        last = l == DEPTH - 1
        mod_l = (jax.nn.silu(c) @ w_ada[l] + b_ada[l])[:, None, :]
        mod_c = jax.nn.silu(c_ctx) @ w_ada[l] + b_ada[l]
        sh_m, sc_m, ga_m, sh_f, sc_f, ga_f = jnp.split(mod_l, 6, axis=-1)
        csh_m, csc_m, cga_m, csh_f, csc_f, cga_f = jnp.split(mod_c, 6, axis=-1)

        h_c = modulate(rmsnorm(x_ctx, g_norm_mix[l]), csh_m, csc_m)
        proj_c = h_c @ w_in[l]
        o_gla_c, (s_f, s_b) = gla_branch(proj_c, w_gate_up[l], b_gate_up[l], g_gla_out[l],
                                         zero_state, zero_state, not last)

        h_l = modulate(rmsnorm(x_lat, g_norm_mix[l]), sh_m, sc_m)
        proj_l = h_l @ w_in[l]
        o_gla_l, _ = gla_branch(proj_l, w_gate_up[l], b_gate_up[l], g_gla_out[l], s_f, s_b, True)
        o_conv_l = conv_branch(proj_l, w_dw[l], b_dw[l], g_conv_ln[l], b_conv_ln[l], rows)
        mix_l = jnp.concatenate([o_gla_l, o_conv_l], axis=-1) @ w_out[l]
        x_lat = x_lat + ga_m * mix_l

        h_l = modulate(rmsnorm(x_lat, g_norm_ffn[l]), sh_f, sc_f)
        x_lat = x_lat + ga_f * moe(h_l, w_router[l], b_router[l], w_e_gate[l], w_e_up[l], w_e_down[l],
                                   w_s_gate[l], w_s_up[l], w_s_down[l])

        if not last:
            o_conv_c = conv_branch(proj_c, w_dw[l], b_dw[l], g_conv_ln[l], b_conv_ln[l], None)
            x_ctx = x_ctx + cga_m * (jnp.concatenate([o_gla_c, o_conv_c], axis=-1) @ w_out[l])
            h_c = modulate(rmsnorm(x_ctx, g_norm_ffn[l]), csh_f, csc_f)
            x_ctx = x_ctx + cga_f * moe(h_c, w_router[l], b_router[l], w_e_gate[l], w_e_up[l], w_e_down[l],
                                       w_s_gate[l], w_s_up[l], w_s_down[l])
    return rmsnorm(x_lat, g_final)
```

```python
import functools

import jax
import jax.numpy as jnp
from jax import lax
from jax.experimental import pallas as pl
from jax.experimental.pallas import tpu as pltpu

F32 = jnp.float32
BF16 = jnp.bfloat16
I32 = jnp.int32
U32 = jnp.uint32

EPS = 1e-6
LANES = 128
GLA_HK = 64
GLA_HV = 128
GLA_CHUNK = 64
GATE_RANK = 16
GATE_TAU = 16.0
CONV_WIDTH = 31
CONV_PAD = (CONV_WIDTH - 1) // 2
GRID_W = 64
N_EXPERTS = 64
N_GROUPS = 8
TOPK_GROUPS = 4
TOP_K = 8
ROUTED_SCALE = 2.5
MOE_BLOCK = 128
VMEM_LIMIT = 56 * 1024 * 1024

NT_DIMS = (((1,), (1,)), ((), ()))
TN_DIMS = (((0,), (0,)), ((), ()))


def _params(*sem):
    return pltpu.CompilerParams(dimension_semantics=sem, vmem_limit_bytes=VMEM_LIMIT)


def _tile(n, *preferred):
    for p in preferred:
        if n % p == 0:
            return p
    raise ValueError(f"no tile for {n} among {preferred}")


def _sigmoid(x):
    return 1.0 / (1.0 + jnp.exp(-x))


def _silu(x):
    return x * _sigmoid(x)


def _ada_kernel(cb_ref, w_ref, b_ref, o_ref, s_scr):
    @pl.when(pl.program_id(0) == 0)
    def _():
        s_scr[...] = _silu(cb_ref[...])

    d, tn = w_ref.shape
    for jt in range(tn // LANES):
        sl = slice(jt * LANES, (jt + 1) * LANES)
        w = w_ref[:, sl]
        for m in range(2):
            p = (w * s_scr[m]).reshape(d // 8, 8, LANES).sum(axis=0)
            o_ref[m:m + 1, sl] = p.sum(axis=0, keepdims=True) + b_ref[:, sl]


def _ada(c2, w_ada, b_ada):
    d, n = w_ada.shape
    tn = _tile(n, 1024, 512, LANES)
    cb = jnp.broadcast_to(c2[:, :, None], (2, d, LANES))
    return pl.pallas_call(
        _ada_kernel,
        grid=(n // tn,),
        in_specs=[pl.BlockSpec((2, d, LANES), lambda j: (0, 0, 0)),
                  pl.BlockSpec((d, tn), lambda j: (0, j)),
                  pl.BlockSpec((1, tn), lambda j: (0, j))],
        out_specs=pl.BlockSpec((2, tn), lambda j: (0, j)),
        out_shape=jax.ShapeDtypeStruct((2, n), F32),
        scratch_shapes=[pltpu.VMEM((2, d, LANES), F32)],
        compiler_params=_params("arbitrary"),
        name="ada",
    )(cb, w_ada, b_ada.reshape(1, n))


def _norm_mod(x, g, sh, sc):
    ms = jnp.mean(x * x, axis=-1, keepdims=True)
    y = x * lax.rsqrt(ms + EPS) * g
    return y * (1.0 + sc) + sh


def _proj_kernel(x_ref, g_ref, sh_ref, sc_ref, w_ref, o_ref, h_scr):
    @pl.when(pl.program_id(1) == 0)
    def _():
        h_scr[...] = _norm_mod(x_ref[...], g_ref[...], sh_ref[...], sc_ref[...]).astype(BF16)

    o_ref[...] = jnp.dot(h_scr[...], w_ref[...], preferred_element_type=F32)


def _proj(x, g, sh, sc, w):
    t, d = x.shape
    n = w.shape[1]
    tm = _tile(t, 512, 256, 128)
    tn = _tile(n, 1152, 384, LANES)
    vec = pl.BlockSpec((1, d), lambda i, j: (0, 0))
    return pl.pallas_call(
        _proj_kernel,
        grid=(t // tm, n // tn),
        in_specs=[pl.BlockSpec((tm, d), lambda i, j: (i, 0)), vec, vec, vec,
                  pl.BlockSpec((d, tn), lambda i, j: (0, j))],
        out_specs=pl.BlockSpec((tm, tn), lambda i, j: (i, j)),
        out_shape=jax.ShapeDtypeStruct((t, n), F32),
        scratch_shapes=[pltpu.VMEM((tm, d), BF16)],
        compiler_params=_params("arbitrary", "arbitrary"),
        name="proj",
    )(x, g, sh, sc, w)


def _gla_chunk(q, k, v2, low, wg, bg, s_t, tri, mask2, lane0, fwd, with_output):
    c = GLA_CHUNK
    z = jnp.dot(low.astype(BF16), wg, preferred_element_type=F32) + bg
    g = (jnp.minimum(z, 0.0) - jnp.log(1.0 + jnp.exp(-jnp.abs(z)))) * (1.0 / GATE_TAU)
    b = jnp.dot(tri, g, preferred_element_type=F32, precision=lax.Precision.HIGHEST)
    tot = b[c - 1:c] if fwd else b[0:1]
    mid = b[c // 2:c // 2 + 1]

    def stack(a):
        return jnp.concatenate([jnp.where(lane0, a, 0.0), jnp.where(lane0, 0.0, a)], axis=0).astype(BF16)

    vs = jnp.concatenate([v2[:, :GLA_HV], v2[:, GLA_HV:]], axis=0).astype(BF16)
    ks = stack(k * jnp.exp(tot - b))
    s_new = s_t * jnp.exp(tot) + lax.dot_general(vs, ks, TN_DIMS, preferred_element_type=F32)
    if not with_output:
        return s_new, None
    qs = q * (GLA_HK ** -0.5)
    qa = stack(qs * jnp.exp(b - mid))
    ka = stack(k * jnp.exp(mid - b))
    qi = stack(qs * jnp.exp(b))
    att = lax.dot_general(qa, ka, NT_DIMS, preferred_element_type=F32)
    att = jnp.where(mask2, att, 0.0).astype(BF16)
    o = jnp.dot(att, vs, preferred_element_type=F32)
    o = o + lax.dot_general(qi, s_t.astype(BF16), NT_DIMS, preferred_element_type=F32)
    return s_new, jnp.concatenate([o[:c], o[c:]], axis=1)


def _gla_kernel(*refs, cb, with_output):
    (qf, kf, vf, lf, qb, kb, vb, lb, wgf, wgb, bgf, bgb, s0_ref) = refs[:13]
    if with_output:
        of_ref, ob_ref, s_scr = refs[13:]
    else:
        sout_ref, s_scr = refs[13:]
    i = pl.program_id(1)
    c = GLA_CHUNK

    @pl.when(i == 0)
    def _():
        s_scr[...] = s0_ref[:, 0]

    row = lax.broadcasted_iota(I32, (c, c), 0)
    col = lax.broadcasted_iota(I32, (c, c), 1)
    tri_f = (col <= row).astype(F32)
    tri_b = (col >= row).astype(F32)
    r2 = lax.broadcasted_iota(I32, (2 * c, 2 * c), 0)
    c2 = lax.broadcasted_iota(I32, (2 * c, 2 * c), 1)
    same = (r2 >= c) == (c2 >= c)
    mask_f = same & (c2 <= r2)
    mask_b = same & (c2 >= r2)
    lane0 = lax.broadcasted_iota(I32, (c, 2 * GLA_HK), 1) < GLA_HK

    s_f = s_scr[0]
    s_b = s_scr[1]
    for n in range(cb):
        rf = slice(n * c, (n + 1) * c)
        rb = slice((cb - 1 - n) * c, (cb - n) * c)
        s_f, o_f = _gla_chunk(qf[rf], kf[rf], vf[rf], lf[rf], wgf[...], bgf[...], s_f,
                              tri_f, mask_f, lane0, True, with_output)
        s_b, o_b = _gla_chunk(qb[rb], kb[rb], vb[rb], lb[rb], wgb[...], bgb[...], s_b,
                              tri_b, mask_b, lane0, False, with_output)
        if with_output:
            of_ref[rf] = o_f
            ob_ref[rb] = o_b
    s_scr[0] = s_f
    s_scr[1] = s_b
    if not with_output:
        @pl.when(i == pl.num_programs(1) - 1)
        def _():
            sout_ref[:, 0] = s_scr[...]


def _gla(proj, wg_f, wg_b, bg_f, bg_b, s0, with_output, col_q, col_k, col_v, col_low):
    t = proj.shape[0]
    cb = _tile(t // GLA_CHUNK, 8, 4, 2, 1)
    rows = cb * GLA_CHUNK
    nb = t // rows
    npairs = wg_f.shape[1] // (2 * GLA_HK)
    pw = 2 * GLA_HK
    vw = 2 * GLA_HV
    fi = lambda p, i: i
    bi = lambda p, i: nb - 1 - i

    def spec(width, col0, blk):
        return pl.BlockSpec((rows, width), lambda p, i: (blk(p, i), col0 // width + p))

    def low_spec(blk):
        return pl.BlockSpec((rows, LANES), lambda p, i: (blk(p, i), col_low // LANES))

    in_specs = [spec(pw, col_q, fi), spec(pw, col_k, fi), spec(vw, col_v, fi), low_spec(fi),
                spec(pw, col_q, bi), spec(pw, col_k, bi), spec(vw, col_v, bi), low_spec(bi),
                pl.BlockSpec((LANES, pw), lambda p, i: (0, p)),
                pl.BlockSpec((LANES, pw), lambda p, i: (0, p)),
                pl.BlockSpec((1, pw), lambda p, i: (0, p)),
                pl.BlockSpec((1, pw), lambda p, i: (0, p)),
                pl.BlockSpec((2, 1, GLA_HV, pw), lambda p, i: (0, p, 0, 0))]
    if with_output:
        out_specs = [pl.BlockSpec((rows, vw), lambda p, i: (i, p)),
                     pl.BlockSpec((rows, vw), lambda p, i: (nb - 1 - i, p))]
        out_shape = [jax.ShapeDtypeStruct((t, npairs * 2 * GLA_HV), F32)] * 2
    else:
        out_specs = pl.BlockSpec((2, 1, GLA_HV, pw), lambda p, i: (0, p, 0, 0))
        out_shape = jax.ShapeDtypeStruct((2, npairs, GLA_HV, pw), F32)
    return pl.pallas_call(
        functools.partial(_gla_kernel, cb=cb, with_output=with_output),
        grid=(npairs, nb),
        in_specs=in_specs,
        out_specs=out_specs,
        out_shape=out_shape,
        scratch_shapes=[pltpu.VMEM((2, GLA_HV, pw), F32)],
        compiler_params=_params("arbitrary", "arbitrary"),
        name="gla_out" if with_output else "gla_state",
    )(proj, proj, proj, proj, proj, proj, proj, proj, wg_f, wg_b, bg_f, bg_b, s0)


def _mix_kernel(of_ref, ob_ref, r_ref, ca_ref, cb_ref, gout_ref, wdw_ref, bdw_ref, gln_ref, bln_ref,
                o_ref, upad, y_scr):
    tm = of_ref.shape[0]
    dg = of_ref.shape[1]
    dc = ca_ref.shape[1]
    nrow = tm // GRID_W
    lead = 16
    for h in range(dg // GLA_HV):
        sl = slice(h * GLA_HV, (h + 1) * GLA_HV)
        o = of_ref[:, sl] + ob_ref[:, sl]
        o = o * lax.rsqrt(jnp.mean(o * o, axis=-1, keepdims=True) + EPS) * gout_ref[:, sl]
        o_ref[:, sl] = (o * _silu(r_ref[:, sl])).astype(o_ref.dtype)
    for r in range(nrow):
        rs = slice(r * GRID_W, (r + 1) * GRID_W)
        upad[r, 0:lead, :] = jnp.zeros((lead, dc), F32)
        upad[r, lead:lead + GRID_W, :] = ca_ref[rs, :] * _sigmoid(cb_ref[rs, :])
        upad[r, lead + GRID_W:, :] = jnp.zeros((lead, dc), F32)
    cw = 256

    def row_body(r, carry):
        for cc in range(dc // cw):
            cs = slice(cc * cw, (cc + 1) * cw)
            acc = jnp.zeros((GRID_W, cw), F32) + bdw_ref[:, cs]
            for j in range(CONV_WIDTH):
                off = lead - CONV_PAD + j
                acc = acc + wdw_ref[j:j + 1, cs] * upad[r, off:off + GRID_W, cs]
            y_scr[r, :, cs] = acc
        return carry

    lax.fori_loop(0, nrow, row_body, 0)
    for r in range(nrow):
        y = y_scr[r]
        mu = jnp.mean(y, axis=-1, keepdims=True)
        yc = y - mu
        var = jnp.mean(yc * yc, axis=-1, keepdims=True)
        yn = yc * lax.rsqrt(var + EPS) * gln_ref[...] + bln_ref[...]
        o_ref[r * GRID_W:(r + 1) * GRID_W, dg:] = _silu(yn).astype(o_ref.dtype)


def _mix(o_f, o_b, proj, g_out, w_dw, b_dw, g_ln, b_ln, col_r, col_ca, col_cb):
    t, dg = o_f.shape
    tm = _tile(t, 256, 128, GRID_W)
    dc = w_dw.shape[1]
    nrow = tm // GRID_W
    row = lambda width: pl.BlockSpec((1, width), lambda i: (0, 0))
    return pl.pallas_call(
        _mix_kernel,
        grid=(t // tm,),
        in_specs=[pl.BlockSpec((tm, dg), lambda i: (i, 0)),
                  pl.BlockSpec((tm, dg), lambda i: (i, 0)),
                  pl.BlockSpec((tm, dg), lambda i: (i, col_r // dg)),
                  pl.BlockSpec((tm, dc), lambda i: (i, col_ca // dc)),
                  pl.BlockSpec((tm, dc), lambda i: (i, col_cb // dc)),
                  row(dg),
                  pl.BlockSpec((CONV_WIDTH, dc), lambda i: (0, 0)),
                  row(dc), row(dc), row(dc)],
        out_specs=pl.BlockSpec((tm, dg + dc), lambda i: (i, 0)),
        out_shape=jax.ShapeDtypeStruct((t, dg + dc), BF16),
        scratch_shapes=[pltpu.VMEM((nrow, GRID_W + 32, dc), F32),
                        pltpu.VMEM((nrow, GRID_W, dc), F32)],
        compiler_params=_params("arbitrary"),
        name="mix",
    )(o_f, o_b, proj, proj, proj, g_out, w_dw, b_dw, g_ln, b_ln)


def _outproj_kernel(m_ref, w_ref, x_ref, ga_ref, o_ref):
    o_ref[...] = x_ref[...] + ga_ref[...] * jnp.dot(m_ref[...], w_ref[...], preferred_element_type=F32)


def _outproj(mix, w, x, ga):
    t, k = mix.shape
    n = w.shape[1]
    tm = _tile(t, 1024, 512, 256)
    tn = _tile(n, 512, 256, LANES)
    return pl.pallas_call(
        _outproj_kernel,
        grid=(t // tm, n // tn),
        in_specs=[pl.BlockSpec((tm, k), lambda i, j: (i, 0)),
                  pl.BlockSpec((k, tn), lambda i, j: (0, j)),
                  pl.BlockSpec((tm, tn), lambda i, j: (i, j)),
                  pl.BlockSpec((1, tn), lambda i, j: (0, j))],
        out_specs=pl.BlockSpec((tm, tn), lambda i, j: (i, j)),
        out_shape=jax.ShapeDtypeStruct((t, n), F32),
        compiler_params=_params("arbitrary", "arbitrary"),
        name="out_proj",
    )(mix, w, x, ga)


def _pack_halves(h):
    c = h.shape[1] // 2
    lo = lax.bitcast_convert_type(h[:, :c].astype(BF16).astype(F32), U32)
    hi = lax.bitcast_convert_type(h[:, c:].astype(BF16).astype(F32), U32)
    return (lo >> 16) | (hi & jnp.uint32(0xFFFF0000))


def _unpack_halves(p):
    lo = lax.bitcast_convert_type(p << 16, F32).astype(BF16)
    hi = lax.bitcast_convert_type(p & jnp.uint32(0xFFFF0000), F32).astype(BF16)
    return jnp.concatenate([lo, hi], axis=1)


def _route_kernel(x_ref, g_ref, sh_ref, sc_ref, wr_ref, br_ref, hp_ref, e_ref, w_ref, rk_ref, cnt_ref,
                  carry):
    tm = x_ref.shape[0]
    ne = N_EXPERTS
    gs = ne // N_GROUPS
    neg = -jnp.inf

    @pl.when(pl.program_id(0) == 0)
    def _():
        carry[...] = jnp.zeros_like(carry)

    h = _norm_mod(x_ref[...], g_ref[...], sh_ref[...], sc_ref[...])
    hp_ref[...] = _pack_halves(h)
    logits = lax.dot_general(wr_ref[...], h, NT_DIMS, preferred_element_type=F32,
                             precision=lax.Precision.HIGHEST)
    scores = _sigmoid(logits)
    sel = scores + br_ref[...]
    sel3 = sel.reshape(N_GROUPS, gs, tm)
    sub = lax.broadcasted_iota(I32, (N_GROUPS, gs, tm), 1)
    m1 = jnp.max(sel3, axis=1, keepdims=True)
    i1 = jnp.min(jnp.where(sel3 == m1, sub, gs), axis=1, keepdims=True)
    m2 = jnp.max(jnp.where(sub == i1, neg, sel3), axis=1, keepdims=True)
    gscore = (m1 + m2).reshape(N_GROUPS, tm)
    gid = lax.broadcasted_iota(I32, (N_GROUPS, tm), 0)
    gmask = jnp.zeros((N_GROUPS, tm), F32)
    for _ in range(TOPK_GROUPS):
        mx = jnp.max(gscore, axis=0, keepdims=True)
        idx = jnp.min(jnp.where(gscore == mx, gid, N_GROUPS), axis=0, keepdims=True)
        pick = gid == idx
        gmask = jnp.where(pick, 1.0, gmask)
        gscore = jnp.where(pick, neg, gscore)
    emask = jnp.broadcast_to(gmask.reshape(N_GROUPS, 1, tm), (N_GROUPS, gs, tm)).reshape(ne, tm)
    cand = jnp.where(emask > 0.0, sel, neg)
    eid = lax.broadcasted_iota(I32, (ne, tm), 0)
    chosen = jnp.zeros((ne, tm), F32)
    idxs, scs = [], []
    for _ in range(TOP_K):
        mx = jnp.max(cand, axis=0, keepdims=True)
        idx = jnp.min(jnp.where(cand == mx, eid, ne), axis=0, keepdims=True)
        pick = eid == idx
        idxs.append(idx)
        scs.append(jnp.sum(jnp.where(pick, scores, 0.0), axis=0, keepdims=True))
        chosen = jnp.where(pick, 1.0, chosen)
        cand = jnp.where(pick, neg, cand)
    ssum = scs[0]
    for s in scs[1:]:
        ssum = ssum + s
    before = (lax.broadcasted_iota(I32, (tm, tm), 0) < lax.broadcasted_iota(I32, (tm, tm), 1))
    prior = jnp.dot(chosen.astype(BF16), before.astype(BF16), preferred_element_type=F32) + carry[...]
    for k in range(TOP_K):
        e_ref[k:k + 1, :] = idxs[k]
        w_ref[k:k + 1, :] = scs[k] / ssum * ROUTED_SCALE
        rk = jnp.sum(jnp.where(eid == idxs[k], prior, 0.0), axis=0, keepdims=True)
        rk_ref[k:k + 1, :] = rk.astype(I32)
    carry[...] = carry[...] + jnp.sum(chosen, axis=1, keepdims=True)
    cnt_ref[...] = jnp.broadcast_to(carry[...], cnt_ref.shape).astype(I32)


def _route(x, g, sh, sc, w_router_t, b_router):
    t, d = x.shape
    tm = _tile(t, 256, LANES)
    vec = pl.BlockSpec((1, d), lambda i: (0, 0))
    tok = lambda dt: jax.ShapeDtypeStruct((TOP_K, t), dt)
    tok_spec = pl.BlockSpec((TOP_K, tm), lambda i: (0, i))
    return pl.pallas_call(
        _route_kernel,
        grid=(t // tm,),
        in_specs=[pl.BlockSpec((tm, d), lambda i: (i, 0)), vec, vec, vec,
                  pl.BlockSpec((N_EXPERTS, d), lambda i: (0, 0)),
                  pl.BlockSpec((N_EXPERTS, 1), lambda i: (0, 0))],
        out_specs=[pl.BlockSpec((tm, d // 2), lambda i: (i, 0)), tok_spec, tok_spec, tok_spec,
                   pl.BlockSpec((N_EXPERTS, LANES), lambda i: (0, 0))],
        out_shape=[jax.ShapeDtypeStruct((t, d // 2), U32), tok(I32), tok(F32), tok(I32),
                   jax.ShapeDtypeStruct((N_EXPERTS, LANES), I32)],
        scratch_shapes=[pltpu.VMEM((N_EXPERTS, 1), F32)],
        compiler_params=_params("arbitrary"),
        name="route",
    )(x, g, sh, sc, w_router_t, b_router)


def _experts_kernel(blk_e, nused, slot_tok, hp_hbm, sw_ref, wg_ref, wu_ref, wd_ref, o_ref, xbuf, sem):
    b = pl.program_id(0)
    nb = pl.num_programs(0)

    def row_copy(blk, slot, r):
        tok = slot_tok[blk * MOE_BLOCK + r]
        return pltpu.make_async_copy(hp_hbm.at[pl.ds(tok, 1)], xbuf.at[slot, pl.ds(r, 1)], sem.at[slot])

    def start_gather(blk, slot):
        def body(r, c):
            row_copy(blk, slot, r).start()
            return c
        lax.fori_loop(0, MOE_BLOCK, body, 0)

    @pl.when(b == 0)
    def _():
        start_gather(0, 0)

    @pl.when((b + 1 < nb) & (b + 1 < nused[0]))
    def _():
        start_gather(b + 1, (b + 1) % 2)

    live = (b < nused[0]) | (b == 0)

    @pl.when(live)
    def _():
        slot = b % 2

        def body(r, c):
            row_copy(b, slot, r).wait()
            return c
        lax.fori_loop(0, MOE_BLOCK, body, 0)
        x = _unpack_halves(xbuf[slot])
        hg = jnp.dot(x, wg_ref[0], preferred_element_type=F32)
        hu = jnp.dot(x, wu_ref[0], preferred_element_type=F32)
        hb = (_silu(hg) * hu * sw_ref[...]).astype(BF16)
        o_ref[...] = jnp.dot(hb, wd_ref[0], preferred_element_type=F32)

    @pl.when(jnp.logical_not(live))
    def _():
        o_ref[...] = jnp.zeros_like(o_ref)


def _experts(blk_e, nused, slot_tok, hp, slot_w, wg, wu, wd):
    nb = blk_e.shape[0]
    _, d, de = wg.shape
    grid_spec = pltpu.PrefetchScalarGridSpec(
        num_scalar_prefetch=3,
        grid=(nb,),
        in_specs=[pl.BlockSpec(memory_space=pl.ANY),
                  pl.BlockSpec((MOE_BLOCK, 1), lambda b, e, n, s: (b, 0)),
                  pl.BlockSpec((1, d, de), lambda b, e, n, s: (e[b], 0, 0)),
                  pl.BlockSpec((1, d, de), lambda b, e, n, s: (e[b], 0, 0)),
                  pl.BlockSpec((1, de, d), lambda b, e, n, s: (e[b], 0, 0))],
        out_specs=pl.BlockSpec((MOE_BLOCK, d), lambda b, e, n, s: (b, 0)),
        scratch_shapes=[pltpu.VMEM((2, MOE_BLOCK, d // 2), U32), pltpu.SemaphoreType.DMA((2,))],
    )
    return pl.pallas_call(
        _experts_kernel,
        grid_spec=grid_spec,
        out_shape=jax.ShapeDtypeStruct((nb * MOE_BLOCK, d), F32),
        compiler_params=_params("arbitrary"),
        name="experts",
    )(blk_e, nused, slot_tok, hp, slot_w, wg, wu, wd)


def _shared_kernel(hp_ref, wg_ref, wu_ref, wd_ref, x_ref, ga_ref, o_ref):
    x = _unpack_halves(hp_ref[...])
    hg = jnp.dot(x, wg_ref[...], preferred_element_type=F32)
    hu = jnp.dot(x, wu_ref[...], preferred_element_type=F32)
    hb = (_silu(hg) * hu).astype(BF16)
    o_ref[...] = x_ref[...] + ga_ref[...] * jnp.dot(hb, wd_ref[...], preferred_element_type=F32)


def _shared(hp, wg, wu, wd, x, ga):
    t, d = x.shape
    tm = _tile(t, 256, LANES)
    ds_ = wg.shape[1]
    return pl.pallas_call(
        _shared_kernel,
        grid=(t // tm,),
        in_specs=[pl.BlockSpec((tm, d // 2), lambda i: (i, 0)),
                  pl.BlockSpec((d, ds_), lambda i: (0, 0)),
                  pl.BlockSpec((d, ds_), lambda i: (0, 0)),
                  pl.BlockSpec((ds_, d), lambda i: (0, 0)),
                  pl.BlockSpec((tm, d), lambda i: (i, 0)),
                  pl.BlockSpec((1, d), lambda i: (0, 0))],
        out_specs=pl.BlockSpec((tm, d), lambda i: (i, 0)),
        out_shape=jax.ShapeDtypeStruct((t, d), F32),
        compiler_params=_params("arbitrary"),
        name="shared",
    )(hp, wg, wu, wd, x, ga)


def _combine_kernel(dest, ys_hbm, base_ref, ga_ref, gf_ref, o_ref, gbuf, sem):
    i = pl.program_id(0)
    n = pl.num_programs(0)
    tm = base_ref.shape[0]

    def row_copy(tile, slot, j):
        src = dest[tile * (tm * TOP_K) + j]
        return pltpu.make_async_copy(ys_hbm.at[pl.ds(src, 1)],
                                     gbuf.at[slot, j % TOP_K, pl.ds(j // TOP_K, 1)], sem.at[slot])

    def start_gather(tile, slot):
        def body(j, c):
            row_copy(tile, slot, j).start()
            return c
        lax.fori_loop(0, tm * TOP_K, body, 0)

    @pl.when(i == 0)
    def _():
        start_gather(0, 0)

    @pl.when(i + 1 < n)
    def _():
        start_gather(i + 1, (i + 1) % 2)

    slot = i % 2

    def body(j, c):
        row_copy(i, slot, j).wait()
        return c
    lax.fori_loop(0, tm * TOP_K, body, 0)
    y = gbuf[slot, 0]
    for k in range(1, TOP_K):
        y = y + gbuf[slot, k]
    x = base_ref[...] + ga_ref[...] * y
    o_ref[...] = x * lax.rsqrt(jnp.mean(x * x, axis=-1, keepdims=True) + EPS) * gf_ref[...]


def _combine(dest, ys, base, ga, g_final):
    t, d = base.shape
    tm = _tile(t, 64)
    grid_spec = pltpu.PrefetchScalarGridSpec(
        num_scalar_prefetch=1,
        grid=(t // tm,),
        in_specs=[pl.BlockSpec(memory_space=pl.ANY),
                  pl.BlockSpec((tm, d), lambda i, s: (i, 0)),
                  pl.BlockSpec((1, d), lambda i, s: (0, 0)),
                  pl.BlockSpec((1, d), lambda i, s: (0, 0))],
        out_specs=pl.BlockSpec((tm, d), lambda i, s: (i, 0)),
        scratch_shapes=[pltpu.VMEM((2, TOP_K, tm, d), F32), pltpu.SemaphoreType.DMA((2,))],
    )
    return pl.pallas_call(
        _combine_kernel,
        grid_spec=grid_spec,
        out_shape=jax.ShapeDtypeStruct((t, d), F32),
        compiler_params=_params("arbitrary"),
        name="combine",
    )(dest, ys, base, ga, g_final)


def kernel(x, c, ctx, c_ctx, w_ada, b_ada, g_norm_mix, g_norm_ffn, w_in, w_gate_up, b_gate_up,
           g_gla_out, w_dw, b_dw, g_conv_ln, b_conv_ln, w_out, w_router, b_router,
           w_e_gate, w_e_up, w_e_down, w_s_gate, w_s_up, w_s_down, g_final):
    assert x.shape[0] == 1 and w_ada.shape[0] == 1, "single batch element, single layer"
    t, d = x.shape[1], x.shape[2]
    dk = w_gate_up.shape[3]
    heads = dk // GLA_HK
    dg = heads * GLA_HV
    dc = w_dw.shape[2]
    xl = x[0]
    xc = ctx[0]

    col_q, col_k, col_v = 0, dk, 2 * dk
    col_r = col_v + dg
    col_ca = col_r + dg
    col_cb = col_ca + dc
    col_low = col_cb + dc
    n_gate = 2 * GATE_RANK
    wi = w_in[0]
    w_all = jnp.concatenate([wi[:, :col_ca], wi[:, col_ca + n_gate:], wi[:, col_ca:col_ca + n_gate],
                             jnp.zeros((d, LANES - n_gate), F32)], axis=1).astype(BF16)

    mod = _ada(jnp.stack([c[0], c_ctx]), w_ada[0], b_ada[0])
    sh_m, sc_m, ga_m, sh_f, sc_f, ga_f = [mod[0:1, k * d:(k + 1) * d] for k in range(6)]
    csh_m, csc_m = mod[1:2, 0:d], mod[1:2, d:2 * d]

    wg_f = jnp.zeros((LANES, dk), F32).at[:GATE_RANK].set(w_gate_up[0, 0]).astype(BF16)
    wg_b = jnp.zeros((LANES, dk), F32).at[GATE_RANK:n_gate].set(w_gate_up[0, 1]).astype(BF16)
    bg_f = b_gate_up[0, 0:1]
    bg_b = b_gate_up[0, 1:2]
    cols = dict(col_q=col_q, col_k=col_k, col_v=col_v, col_low=col_low)

    gn_mix = g_norm_mix[0:1]
    proj_c = _proj(xc, gn_mix, csh_m, csc_m, w_all)
    s0 = jnp.zeros((2, heads // 2, GLA_HV, 2 * GLA_HK), F32)
    s_ctx = _gla(proj_c, wg_f, wg_b, bg_f, bg_b, s0, with_output=False, **cols)

    proj_l = _proj(xl, gn_mix, sh_m, sc_m, w_all)
    o_f, o_b = _gla(proj_l, wg_f, wg_b, bg_f, bg_b, s_ctx, with_output=True, **cols)
    mix = _mix(o_f, o_b, proj_l, g_gla_out[0:1], w_dw[0], b_dw[0:1], g_conv_ln[0:1], b_conv_ln[0:1],
               col_r, col_ca, col_cb)
    x1 = _outproj(mix, w_out[0].astype(BF16), xl, ga_m)

    hp, eidx, wts, rank, cnt = _route(x1, g_norm_ffn[0:1], sh_f, sc_f, w_router[0].T,
                                      b_router[0].reshape(N_EXPERTS, 1))
    nk = t * TOP_K
    nb = (nk + MOE_BLOCK - 1) // MOE_BLOCK + N_EXPERTS
    counts = cnt[:, 0]
    pcounts = (counts + MOE_BLOCK - 1) // MOE_BLOCK * MOE_BLOCK
    pend = jnp.cumsum(pcounts)
    pstart = pend - pcounts
    dest = (pstart[eidx] + rank).T.reshape(-1).astype(I32)
    tok_ids = jnp.repeat(jnp.arange(t, dtype=I32), TOP_K)
    slot_tok = jnp.zeros((nb * MOE_BLOCK,), I32).at[dest].set(tok_ids, unique_indices=True)
    slot_w = jnp.zeros((nb * MOE_BLOCK,), F32).at[dest].set(wts.T.reshape(-1), unique_indices=True)
    blk_e = jnp.minimum(jnp.searchsorted(pend, jnp.arange(nb, dtype=I32) * MOE_BLOCK, side='right'),
                        N_EXPERTS - 1).astype(I32)
    nused = (pend[-1:] // MOE_BLOCK).astype(I32)

    ys = _experts(blk_e, nused, slot_tok, hp, slot_w.reshape(-1, 1),
                  w_e_gate[0].astype(BF16), w_e_up[0].astype(BF16), w_e_down[0].astype(BF16))
    base = _shared(hp, w_s_gate[0].astype(BF16), w_s_up[0].astype(BF16), w_s_down[0].astype(BF16), x1, ga_f)
    out = _combine(dest, ys, base, ga_f, g_final.reshape(1, d))
    return out[None]
```

```python
import functools

import jax
import jax.numpy as jnp
from jax import lax
from jax.experimental import pallas as pl
from jax.experimental.pallas import tpu as pltpu

F32 = jnp.float32
BF16 = jnp.bfloat16
I32 = jnp.int32
U32 = jnp.uint32

EPS = 1e-6
LANES = 128
GLA_HK = 64
GLA_HV = 128
GLA_CHUNK = 64
GATE_RANK = 16
GATE_TAU = 16.0
CONV_WIDTH = 31
CONV_PAD = (CONV_WIDTH - 1) // 2
GRID_W = 64
N_EXPERTS = 64
N_GROUPS = 8
TOPK_GROUPS = 4
TOP_K = 8
ROUTED_SCALE = 2.5
MOE_BLOCK = 128
VMEM_LIMIT = 56 * 1024 * 1024

NT_DIMS = (((1,), (1,)), ((), ()))
TN_DIMS = (((0,), (0,)), ((), ()))


def _params(*sem):
    return pltpu.CompilerParams(dimension_semantics=sem, vmem_limit_bytes=VMEM_LIMIT)


def _tile(n, *preferred):
    for p in preferred:
        if n % p == 0:
            return p
    raise ValueError(f"no tile for {n} among {preferred}")


def _sigmoid(x):
    return 1.0 / (1.0 + jnp.exp(-x))


def _silu(x):
    return x * _sigmoid(x)


def _ada_kernel(cb_ref, w_ref, b_ref, o_ref, s_scr):
    @pl.when(pl.program_id(0) == 0)
    def _():
        s_scr[...] = _silu(cb_ref[...])

    d, tn = w_ref.shape
    for jt in range(tn // LANES):
        sl = slice(jt * LANES, (jt + 1) * LANES)
        w = w_ref[:, sl]
        for m in range(2):
            p = (w * s_scr[m]).reshape(d // 8, 8, LANES).sum(axis=0)
            o_ref[m:m + 1, sl] = p.sum(axis=0, keepdims=True) + b_ref[:, sl]


def _ada(c2, w_ada, b_ada):
    d, n = w_ada.shape
    tn = _tile(n, 1024, 512, LANES)
    cb = jnp.broadcast_to(c2[:, :, None], (2, d, LANES))
    return pl.pallas_call(
        _ada_kernel,
        grid=(n // tn,),
        in_specs=[pl.BlockSpec((2, d, LANES), lambda j: (0, 0, 0)),
                  pl.BlockSpec((d, tn), lambda j: (0, j)),
                  pl.BlockSpec((1, tn), lambda j: (0, j))],
        out_specs=pl.BlockSpec((2, tn), lambda j: (0, j)),
        out_shape=jax.ShapeDtypeStruct((2, n), F32),
        scratch_shapes=[pltpu.VMEM((2, d, LANES), F32)],
        compiler_params=_params("arbitrary"),
        name="ada",
    )(cb, w_ada, b_ada.reshape(1, n))


def _norm_mod(x, g, sh, sc):
    ms = jnp.mean(x * x, axis=-1, keepdims=True)
    y = x * lax.rsqrt(ms + EPS) * g
    return y * (1.0 + sc) + sh


def _proj_kernel(x_ref, g_ref, sh_ref, sc_ref, w_ref, o_ref, h_scr):
    @pl.when(pl.program_id(1) == 0)
    def _():
        h_scr[...] = _norm_mod(x_ref[...], g_ref[...], sh_ref[...], sc_ref[...]).astype(BF16)

    o_ref[...] = jnp.dot(h_scr[...], w_ref[...], preferred_element_type=F32)


def _proj(x, g, sh, sc, w):
    t, d = x.shape
    n = w.shape[1]
    tm = _tile(t, 512, 256, 128)
    tn = _tile(n, 1152, 384, LANES)
    vec = pl.BlockSpec((1, d), lambda i, j: (0, 0))
    return pl.pallas_call(
        _proj_kernel,
        grid=(t // tm, n // tn),
        in_specs=[pl.BlockSpec((tm, d), lambda i, j: (i, 0)), vec, vec, vec,
                  pl.BlockSpec((d, tn), lambda i, j: (0, j))],
        out_specs=pl.BlockSpec((tm, tn), lambda i, j: (i, j)),
        out_shape=jax.ShapeDtypeStruct((t, n), F32),
        scratch_shapes=[pltpu.VMEM((tm, d), BF16)],
        compiler_params=_params("arbitrary", "arbitrary"),
        name="proj",
    )(x, g, sh, sc, w)


def _gla_chunk(q, k, v2, low, wg, bg, s_t, tri, mask2, lane0, fwd, with_output):
    c = GLA_CHUNK
    z = jnp.dot(low.astype(BF16), wg, preferred_element_type=F32) + bg
    g = (jnp.minimum(z, 0.0) - jnp.log(1.0 + jnp.exp(-jnp.abs(z)))) * (1.0 / GATE_TAU)
    b = jnp.dot(tri, g, preferred_element_type=F32, precision=lax.Precision.HIGHEST)
    tot = b[c - 1:c] if fwd else b[0:1]
    mid = b[c // 2:c // 2 + 1]

    def stack(a):
        return jnp.concatenate([jnp.where(lane0, a, 0.0), jnp.where(lane0, 0.0, a)], axis=0).astype(BF16)

    vs = jnp.concatenate([v2[:, :GLA_HV], v2[:, GLA_HV:]], axis=0).astype(BF16)
    ks = stack(k * jnp.exp(tot - b))
    s_new = s_t * jnp.exp(tot) + lax.dot_general(vs, ks, TN_DIMS, preferred_element_type=F32)
    if not with_output:
        return s_new, None
    qs = q * (GLA_HK ** -0.5)
    qa = stack(qs * jnp.exp(b - mid))
    ka = stack(k * jnp.exp(mid - b))
    qi = stack(qs * jnp.exp(b))
    att = lax.dot_general(qa, ka, NT_DIMS, preferred_element_type=F32)
    att = jnp.where(mask2, att, 0.0).astype(BF16)
    o = jnp.dot(att, vs, preferred_element_type=F32)
    o = o + lax.dot_general(qi, s_t.astype(BF16), NT_DIMS, preferred_element_type=F32)
    return s_new, jnp.concatenate([o[:c], o[c:]], axis=1)


def _gla_kernel(*refs, cb, with_output):
    (qf, kf, vf, lf, qb, kb, vb, lb, wgf, wgb, bgf, bgb, s0_ref) = refs[:13]
    if with_output:
        of_ref, ob_ref, s_scr = refs[13:]
    else:
        sout_ref, s_scr = refs[13:]
    i = pl.program_id(1)
    c = GLA_CHUNK

    @pl.when(i == 0)
    def _():
        s_scr[...] = s0_ref[:, 0]

    row = lax.broadcasted_iota(I32, (c, c), 0)
    col = lax.broadcasted_iota(I32, (c, c), 1)
    tri_f = (col <= row).astype(F32)
    tri_b = (col >= row).astype(F32)
    r2 = lax.broadcasted_iota(I32, (2 * c, 2 * c), 0)
    c2 = lax.broadcasted_iota(I32, (2 * c, 2 * c), 1)
    same = (r2 >= c) == (c2 >= c)
    mask_f = same & (c2 <= r2)
    mask_b = same & (c2 >= r2)
    lane0 = lax.broadcasted_iota(I32, (c, 2 * GLA_HK), 1) < GLA_HK

    s_f = s_scr[0]
    s_b = s_scr[1]
    for n in range(cb):
        rf = slice(n * c, (n + 1) * c)
        rb = slice((cb - 1 - n) * c, (cb - n) * c)
        s_f, o_f = _gla_chunk(qf[rf], kf[rf], vf[rf], lf[rf], wgf[...], bgf[...], s_f,
                              tri_f, mask_f, lane0, True, with_output)
        s_b, o_b = _gla_chunk(qb[rb], kb[rb], vb[rb], lb[rb], wgb[...], bgb[...], s_b,
                              tri_b, mask_b, lane0, False, with_output)
        if with_output:
            of_ref[rf] = o_f
            ob_ref[rb] = o_b
    s_scr[0] = s_f
    s_scr[1] = s_b
    if not with_output:
        @pl.when(i == pl.num_programs(1) - 1)
        def _():
            sout_ref[:, 0] = s_scr[...]


def _gla(proj, wg_f, wg_b, bg_f, bg_b, s0, with_output, col_q, col_k, col_v, col_low):
    t = proj.shape[0]
    cb = _tile(t // GLA_CHUNK, 8, 4, 2, 1)
    rows = cb * GLA_CHUNK
    nb = t // rows
    npairs = wg_f.shape[1] // (2 * GLA_HK)
    pw = 2 * GLA_HK
    vw = 2 * GLA_HV
    fi = lambda p, i: i
    bi = lambda p, i: nb - 1 - i

    def spec(width, col0, blk):
        return pl.BlockSpec((rows, width), lambda p, i: (blk(p, i), col0 // width + p))

    def low_spec(blk):
        return pl.BlockSpec((rows, LANES), lambda p, i: (blk(p, i), col_low // LANES))

    in_specs = [spec(pw, col_q, fi), spec(pw, col_k, fi), spec(vw, col_v, fi), low_spec(fi),
                spec(pw, col_q, bi), spec(pw, col_k, bi), spec(vw, col_v, bi), low_spec(bi),
                pl.BlockSpec((LANES, pw), lambda p, i: (0, p)),
                pl.BlockSpec((LANES, pw), lambda p, i: (0, p)),
                pl.BlockSpec((1, pw), lambda p, i: (0, p)),
                pl.BlockSpec((1, pw), lambda p, i: (0, p)),
                pl.BlockSpec((2, 1, GLA_HV, pw), lambda p, i: (0, p, 0, 0))]
    if with_output:
        out_specs = [pl.BlockSpec((rows, vw), lambda p, i: (i, p)),
                     pl.BlockSpec((rows, vw), lambda p, i: (nb - 1 - i, p))]
        out_shape = [jax.ShapeDtypeStruct((t, npairs * 2 * GLA_HV), F32)] * 2
    else:
        out_specs = pl.BlockSpec((2, 1, GLA_HV, pw), lambda p, i: (0, p, 0, 0))
        out_shape = jax.ShapeDtypeStruct((2, npairs, GLA_HV, pw), F32)
    return pl.pallas_call(
        functools.partial(_gla_kernel, cb=cb, with_output=with_output),
        grid=(npairs, nb),
        in_specs=in_specs,
        out_specs=out_specs,
        out_shape=out_shape,
        scratch_shapes=[pltpu.VMEM((2, GLA_HV, pw), F32)],
        compiler_params=_params("arbitrary", "arbitrary"),
        name="gla_out" if with_output else "gla_state",
    )(proj, proj, proj, proj, proj, proj, proj, proj, wg_f, wg_b, bg_f, bg_b, s0)


def _mix_kernel(of_ref, ob_ref, r_ref, ca_ref, cb_ref, gout_ref, wdw_ref, bdw_ref, gln_ref, bln_ref,
                o_ref, upad, y_scr):
    tm = of_ref.shape[0]
    dg = of_ref.shape[1]
    dc = ca_ref.shape[1]
    nrow = tm // GRID_W
    lead = 16
    for h in range(dg // GLA_HV):
        sl = slice(h * GLA_HV, (h + 1) * GLA_HV)
        o = of_ref[:, sl] + ob_ref[:, sl]
        o = o * lax.rsqrt(jnp.mean(o * o, axis=-1, keepdims=True) + EPS) * gout_ref[:, sl]
        o_ref[:, sl] = (o * _silu(r_ref[:, sl])).astype(o_ref.dtype)
    for r in range(nrow):
        rs = slice(r * GRID_W, (r + 1) * GRID_W)
        upad[r, 0:lead, :] = jnp.zeros((lead, dc), F32)
        upad[r, lead:lead + GRID_W, :] = ca_ref[rs, :] * _sigmoid(cb_ref[rs, :])
        upad[r, lead + GRID_W:, :] = jnp.zeros((lead, dc), F32)
    cw = 256

    def row_body(r, carry):
        for cc in range(dc // cw):
            cs = slice(cc * cw, (cc + 1) * cw)
            acc = jnp.zeros((GRID_W, cw), F32) + bdw_ref[:, cs]
            for j in range(CONV_WIDTH):
                off = lead - CONV_PAD + j
                acc = acc + wdw_ref[j:j + 1, cs] * upad[r, off:off + GRID_W, cs]
            y_scr[r, :, cs] = acc
        return carry

    lax.fori_loop(0, nrow, row_body, 0)
    for r in range(nrow):
        y = y_scr[r]
        mu = jnp.mean(y, axis=-1, keepdims=True)
        yc = y - mu
        var = jnp.mean(yc * yc, axis=-1, keepdims=True)
        yn = yc * lax.rsqrt(var + EPS) * gln_ref[...] + bln_ref[...]
        o_ref[r * GRID_W:(r + 1) * GRID_W, dg:] = _silu(yn).astype(o_ref.dtype)


def _mix(o_f, o_b, proj, g_out, w_dw, b_dw, g_ln, b_ln, col_r, col_ca, col_cb):
    t, dg = o_f.shape
    tm = _tile(t, 256, 128, GRID_W)
    dc = w_dw.shape[1]
    nrow = tm // GRID_W
    row = lambda width: pl.BlockSpec((1, width), lambda i: (0, 0))
    return pl.pallas_call(
        _mix_kernel,
        grid=(t // tm,),
        in_specs=[pl.BlockSpec((tm, dg), lambda i: (i, 0)),
                  pl.BlockSpec((tm, dg), lambda i: (i, 0)),
                  pl.BlockSpec((tm, dg), lambda i: (i, col_r // dg)),
                  pl.BlockSpec((tm, dc), lambda i: (i, col_ca // dc)),
                  pl.BlockSpec((tm, dc), lambda i: (i, col_cb // dc)),
                  row(dg),
                  pl.BlockSpec((CONV_WIDTH, dc), lambda i: (0, 0)),
                  row(dc), row(dc), row(dc)],
        out_specs=pl.BlockSpec((tm, dg + dc), lambda i: (i, 0)),
        out_shape=jax.ShapeDtypeStruct((t, dg + dc), BF16),
        scratch_shapes=[pltpu.VMEM((nrow, GRID_W + 32, dc), F32),
                        pltpu.VMEM((nrow, GRID_W, dc), F32)],
        compiler_params=_params("arbitrary"),
        name="mix",
    )(o_f, o_b, proj, proj, proj, g_out, w_dw, b_dw, g_ln, b_ln)


def _outproj_kernel(m_ref, w_ref, x_ref, ga_ref, o_ref):
    o_ref[...] = x_ref[...] + ga_ref[...] * jnp.dot(m_ref[...], w_ref[...], preferred_element_type=F32)


def _outproj(mix, w, x, ga):
    t, k = mix.shape
    n = w.shape[1]
    tm = _tile(t, 1024, 512, 256)
    tn = _tile(n, 512, 256, LANES)
    return pl.pallas_call(
        _outproj_kernel,
        grid=(t // tm, n // tn),
        in_specs=[pl.BlockSpec((tm, k), lambda i, j: (i, 0)),
                  pl.BlockSpec((k, tn), lambda i, j: (0, j)),
                  pl.BlockSpec((tm, tn), lambda i, j: (i, j)),
                  pl.BlockSpec((1, tn), lambda i, j: (0, j))],
        out_specs=pl.BlockSpec((tm, tn), lambda i, j: (i, j)),
        out_shape=jax.ShapeDtypeStruct((t, n), F32),
        compiler_params=_params("arbitrary", "arbitrary"),
        name="out_proj",
    )(mix, w, x, ga)


def _pack_halves(h):
    c = h.shape[1] // 2
    lo = lax.bitcast_convert_type(h[:, :c].astype(BF16).astype(F32), U32)
    hi = lax.bitcast_convert_type(h[:, c:].astype(BF16).astype(F32), U32)
    return (lo >> 16) | (hi & jnp.uint32(0xFFFF0000))


def _unpack_halves(p):
    lo = lax.bitcast_convert_type(p << 16, F32).astype(BF16)
    hi = lax.bitcast_convert_type(p & jnp.uint32(0xFFFF0000), F32).astype(BF16)
    return jnp.concatenate([lo, hi], axis=1)


def _route_kernel(x_ref, g_ref, sh_ref, sc_ref, wr_ref, br_ref, hp_ref, e_ref, w_ref, rk_ref, cnt_ref,
                  carry):
    tm = x_ref.shape[0]
    ne = N_EXPERTS
    gs = ne // N_GROUPS
    neg = -jnp.inf

    @pl.when(pl.program_id(0) == 0)
    def _():
        carry[...] = jnp.zeros_like(carry)

    h = _norm_mod(x_ref[...], g_ref[...], sh_ref[...], sc_ref[...])
    hp_ref[...] = _pack_halves(h)
    logits = lax.dot_general(wr_ref[...], h, NT_DIMS, preferred_element_type=F32,
                             precision=lax.Precision.HIGHEST)
    scores = _sigmoid(logits)
    sel = scores + br_ref[...]
    sel3 = sel.reshape(N_GROUPS, gs, tm)
    sub = lax.broadcasted_iota(I32, (N_GROUPS, gs, tm), 1)
    m1 = jnp.max(sel3, axis=1, keepdims=True)
    i1 = jnp.min(jnp.where(sel3 == m1, sub, gs), axis=1, keepdims=True)
    m2 = jnp.max(jnp.where(sub == i1, neg, sel3), axis=1, keepdims=True)
    gscore = (m1 + m2).reshape(N_GROUPS, tm)
    gid = lax.broadcasted_iota(I32, (N_GROUPS, tm), 0)
    gmask = jnp.zeros((N_GROUPS, tm), F32)
    for _ in range(TOPK_GROUPS):
        mx = jnp.max(gscore, axis=0, keepdims=True)
        idx = jnp.min(jnp.where(gscore == mx, gid, N_GROUPS), axis=0, keepdims=True)
        pick = gid == idx
        gmask = jnp.where(pick, 1.0, gmask)
        gscore = jnp.where(pick, neg, gscore)
    emask = jnp.broadcast_to(gmask.reshape(N_GROUPS, 1, tm), (N_GROUPS, gs, tm)).reshape(ne, tm)
    cand = jnp.where(emask > 0.0, sel, neg)
    eid = lax.broadcasted_iota(I32, (ne, tm), 0)
    chosen = jnp.zeros((ne, tm), F32)
    idxs, scs = [], []
    for _ in range(TOP_K):
        mx = jnp.max(cand, axis=0, keepdims=True)
        idx = jnp.min(jnp.where(cand == mx, eid, ne), axis=0, keepdims=True)
        pick = eid == idx
        idxs.append(idx)
        scs.append(jnp.sum(jnp.where(pick, scores, 0.0), axis=0, keepdims=True))
        chosen = jnp.where(pick, 1.0, chosen)
        cand = jnp.where(pick, neg, cand)
    ssum = scs[0]
    for s in scs[1:]:
        ssum = ssum + s
    before = (lax.broadcasted_iota(I32, (tm, tm), 0) < lax.broadcasted_iota(I32, (tm, tm), 1))
    prior = jnp.dot(chosen.astype(BF16), before.astype(BF16), preferred_element_type=F32) + carry[...]
    for k in range(TOP_K):
        e_ref[k:k + 1, :] = idxs[k]
        w_ref[k:k + 1, :] = scs[k] / ssum * ROUTED_SCALE
        rk = jnp.sum(jnp.where(eid == idxs[k], prior, 0.0), axis=0, keepdims=True)
        rk_ref[k:k + 1, :] = rk.astype(I32)
    carry[...] = carry[...] + jnp.sum(chosen, axis=1, keepdims=True)
    cnt_ref[...] = jnp.broadcast_to(carry[...], cnt_ref.shape).astype(I32)


def _route(x, g, sh, sc, w_router_t, b_router):
    t, d = x.shape
    tm = _tile(t, 256, LANES)
    vec = pl.BlockSpec((1, d), lambda i: (0, 0))
    tok = lambda dt: jax.ShapeDtypeStruct((TOP_K, t), dt)
    tok_spec = pl.BlockSpec((TOP_K, tm), lambda i: (0, i))
    return pl.pallas_call(
        _route_kernel,
        grid=(t // tm,),
        in_specs=[pl.BlockSpec((tm, d), lambda i: (i, 0)), vec, vec, vec,
                  pl.BlockSpec((N_EXPERTS, d), lambda i: (0, 0)),
                  pl.BlockSpec((N_EXPERTS, 1), lambda i: (0, 0))],
        out_specs=[pl.BlockSpec((tm, d // 2), lambda i: (i, 0)), tok_spec, tok_spec, tok_spec,
                   pl.BlockSpec((N_EXPERTS, LANES), lambda i: (0, 0))],
        out_shape=[jax.ShapeDtypeStruct((t, d // 2), U32), tok(I32), tok(F32), tok(I32),
                   jax.ShapeDtypeStruct((N_EXPERTS, LANES), I32)],
        scratch_shapes=[pltpu.VMEM((N_EXPERTS, 1), F32)],
        compiler_params=_params("arbitrary"),
        name="route",
    )(x, g, sh, sc, w_router_t, b_router)


def _experts_kernel(blk_e, nused, slot_tok, hp_hbm, wg_ref, wu_ref, wd_ref, o_ref, xbuf, sem):
    b = pl.program_id(0)
    nb = pl.num_programs(0)

    def start_gather(blk, slot):
        for r in range(MOE_BLOCK):
            tok = slot_tok[blk * MOE_BLOCK + r]
            pltpu.make_async_copy(hp_hbm.at[pl.ds(tok, 1)], xbuf.at[slot, pl.ds(r, 1)],
                                  sem.at[slot]).start(priority=r % 2)

    def wait_gather(slot):
        pltpu.make_async_copy(hp_hbm.at[pl.ds(0, MOE_BLOCK)], xbuf.at[slot], sem.at[slot]).wait()

    @pl.when(b == 0)
    def _():
        start_gather(0, 0)

    live = (b < nused[0]) | (b == 0)
    next_live = (b + 1 < nb) & (b + 1 < nused[0])

    def compute(gather_next):
        slot = b % 2
        wait_gather(slot)
        if gather_next:
            start_gather(b + 1, 1 - slot)
        x = _unpack_halves(xbuf[slot])
        hg = jnp.dot(x, wg_ref[0], preferred_element_type=F32)
        hu = jnp.dot(x, wu_ref[0], preferred_element_type=F32)
        hb = (_silu(hg) * hu).astype(BF16)
        o_ref[...] = jnp.dot(hb, wd_ref[0], preferred_element_type=F32)

    pl.when(live & next_live)(functools.partial(compute, True))
    pl.when(live & jnp.logical_not(next_live))(functools.partial(compute, False))

    @pl.when(jnp.logical_not(live))
    def _():
        o_ref[...] = jnp.zeros_like(o_ref)


def _experts(blk_e, nused, slot_tok, hp, wg, wu, wd):
    nb = blk_e.shape[0]
    _, d, de = wg.shape
    grid_spec = pltpu.PrefetchScalarGridSpec(
        num_scalar_prefetch=3,
        grid=(nb,),
        in_specs=[pl.BlockSpec(memory_space=pl.ANY),
                  pl.BlockSpec((1, d, de), lambda b, e, n, s: (e[b], 0, 0)),
                  pl.BlockSpec((1, d, de), lambda b, e, n, s: (e[b], 0, 0)),
                  pl.BlockSpec((1, de, d), lambda b, e, n, s: (e[b], 0, 0))],
        out_specs=pl.BlockSpec((MOE_BLOCK, d), lambda b, e, n, s: (b, 0)),
        scratch_shapes=[pltpu.VMEM((2, MOE_BLOCK, d // 2), U32), pltpu.SemaphoreType.DMA((2,))],
    )
    return pl.pallas_call(
        _experts_kernel,
        grid_spec=grid_spec,
        out_shape=jax.ShapeDtypeStruct((nb * MOE_BLOCK, d), F32),
        compiler_params=_params("arbitrary"),
        name="experts",
    )(blk_e, nused, slot_tok, hp, wg, wu, wd)


def _shared_kernel(hp_ref, wg_ref, wu_ref, wd_ref, x_ref, ga_ref, o_ref):
    x = _unpack_halves(hp_ref[...])
    hg = jnp.dot(x, wg_ref[...], preferred_element_type=F32)
    hu = jnp.dot(x, wu_ref[...], preferred_element_type=F32)
    hb = (_silu(hg) * hu).astype(BF16)
    o_ref[...] = x_ref[...] + ga_ref[...] * jnp.dot(hb, wd_ref[...], preferred_element_type=F32)


def _shared(hp, wg, wu, wd, x, ga):
    t, d = x.shape
    tm = _tile(t, 256, LANES)
    ds_ = wg.shape[1]
    return pl.pallas_call(
        _shared_kernel,
        grid=(t // tm,),
        in_specs=[pl.BlockSpec((tm, d // 2), lambda i: (i, 0)),
                  pl.BlockSpec((d, ds_), lambda i: (0, 0)),
                  pl.BlockSpec((d, ds_), lambda i: (0, 0)),
                  pl.BlockSpec((ds_, d), lambda i: (0, 0)),
                  pl.BlockSpec((tm, d), lambda i: (i, 0)),
                  pl.BlockSpec((1, d), lambda i: (0, 0))],
        out_specs=pl.BlockSpec((tm, d), lambda i: (i, 0)),
        out_shape=jax.ShapeDtypeStruct((t, d), F32),
        compiler_params=_params("arbitrary"),
        name="shared",
    )(hp, wg, wu, wd, x, ga)


def _combine_kernel(dest, ys_hbm, w_ref, base_ref, ga_ref, gf_ref, o_ref, gbuf, sem):
    i = pl.program_id(0)
    n = pl.num_programs(0)
    tm = base_ref.shape[0]

    def start_gather(tile, slot):
        for j in range(tm * TOP_K):
            src = dest[tile * (tm * TOP_K) + j]
            pltpu.make_async_copy(ys_hbm.at[pl.ds(src, 1)], gbuf.at[slot, j % TOP_K, pl.ds(j // TOP_K, 1)],
                                  sem.at[slot]).start(priority=j % 2)

    def compute(gather_next):
        slot = i % 2
        for k in range(TOP_K):
            pltpu.make_async_copy(ys_hbm.at[pl.ds(0, tm)], gbuf.at[slot, k], sem.at[slot]).wait()
        if gather_next:
            start_gather(i + 1, 1 - slot)
        y = w_ref[:, 0:1] * gbuf[slot, 0]
        for k in range(1, TOP_K):
            y = y + w_ref[:, k:k + 1] * gbuf[slot, k]
        x = base_ref[...] + ga_ref[...] * y
        o_ref[...] = x * lax.rsqrt(jnp.mean(x * x, axis=-1, keepdims=True) + EPS) * gf_ref[...]

    @pl.when(i == 0)
    def _():
        start_gather(0, 0)

    pl.when(i + 1 < n)(functools.partial(compute, True))
    pl.when(i + 1 == n)(functools.partial(compute, False))


def _combine(dest, ys, wts, base, ga, g_final):
    t, d = base.shape
    tm = _tile(t, 64)
    grid_spec = pltpu.PrefetchScalarGridSpec(
        num_scalar_prefetch=1,
        grid=(t // tm,),
        in_specs=[pl.BlockSpec(memory_space=pl.ANY),
                  pl.BlockSpec((tm, TOP_K), lambda i, s: (i, 0)),
                  pl.BlockSpec((tm, d), lambda i, s: (i, 0)),
                  pl.BlockSpec((1, d), lambda i, s: (0, 0)),
                  pl.BlockSpec((1, d), lambda i, s: (0, 0))],
        out_specs=pl.BlockSpec((tm, d), lambda i, s: (i, 0)),
        scratch_shapes=[pltpu.VMEM((2, TOP_K, tm, d), F32), pltpu.SemaphoreType.DMA((2,))],
    )
    return pl.pallas_call(
        _combine_kernel,
        grid_spec=grid_spec,
        out_shape=jax.ShapeDtypeStruct((t, d), F32),
        compiler_params=_params("arbitrary"),
        name="combine",
    )(dest, ys, wts, base, ga, g_final)


def kernel(x, c, ctx, c_ctx, w_ada, b_ada, g_norm_mix, g_norm_ffn, w_in, w_gate_up, b_gate_up,
           g_gla_out, w_dw, b_dw, g_conv_ln, b_conv_ln, w_out, w_router, b_router,
           w_e_gate, w_e_up, w_e_down, w_s_gate, w_s_up, w_s_down, g_final):
    assert x.shape[0] == 1 and w_ada.shape[0] == 1, "single batch element, single layer"
    t, d = x.shape[1], x.shape[2]
    dk = w_gate_up.shape[3]
    heads = dk // GLA_HK
    dg = heads * GLA_HV
    dc = w_dw.shape[2]
    xl = x[0]
    xc = ctx[0]

    col_q, col_k, col_v = 0, dk, 2 * dk
    col_r = col_v + dg
    col_ca = col_r + dg
    col_cb = col_ca + dc
    col_low = col_cb + dc
    n_gate = 2 * GATE_RANK
    wi = w_in[0]
    w_all = jnp.concatenate([wi[:, :col_ca], wi[:, col_ca + n_gate:], wi[:, col_ca:col_ca + n_gate],
                             jnp.zeros((d, LANES - n_gate), F32)], axis=1).astype(BF16)

    mod = _ada(jnp.stack([c[0], c_ctx]), w_ada[0], b_ada[0])
    sh_m, sc_m, ga_m, sh_f, sc_f, ga_f = [mod[0:1, k * d:(k + 1) * d] for k in range(6)]
    csh_m, csc_m = mod[1:2, 0:d], mod[1:2, d:2 * d]

    wg_f = jnp.zeros((LANES, dk), F32).at[:GATE_RANK].set(w_gate_up[0, 0]).astype(BF16)
    wg_b = jnp.zeros((LANES, dk), F32).at[GATE_RANK:n_gate].set(w_gate_up[0, 1]).astype(BF16)
    bg_f = b_gate_up[0, 0:1]
    bg_b = b_gate_up[0, 1:2]
    cols = dict(col_q=col_q, col_k=col_k, col_v=col_v, col_low=col_low)

    gn_mix = g_norm_mix[0:1]
    proj_c = _proj(xc, gn_mix, csh_m, csc_m, w_all)
    s0 = jnp.zeros((2, heads // 2, GLA_HV, 2 * GLA_HK), F32)
    s_ctx = _gla(proj_c, wg_f, wg_b, bg_f, bg_b, s0, with_output=False, **cols)

    proj_l = _proj(xl, gn_mix, sh_m, sc_m, w_all)
    o_f, o_b = _gla(proj_l, wg_f, wg_b, bg_f, bg_b, s_ctx, with_output=True, **cols)
    mix = _mix(o_f, o_b, proj_l, g_gla_out[0:1], w_dw[0], b_dw[0:1], g_conv_ln[0:1], b_conv_ln[0:1],
               col_r, col_ca, col_cb)
    x1 = _outproj(mix, w_out[0].astype(BF16), xl, ga_m)

    hp, eidx, wts, rank, cnt = _route(x1, g_norm_ffn[0:1], sh_f, sc_f, w_router[0].T,
                                      b_router[0].reshape(N_EXPERTS, 1))
    nk = t * TOP_K
    nb = (nk + MOE_BLOCK - 1) // MOE_BLOCK + N_EXPERTS
    counts = cnt[:, 0]
    pcounts = (counts + MOE_BLOCK - 1) // MOE_BLOCK * MOE_BLOCK
    pend = jnp.cumsum(pcounts)
    pstart = pend - pcounts
    experts = jnp.arange(N_EXPERTS, dtype=I32)
    pstart_tok = jnp.sum(jnp.where(eidx[:, :, None] == experts, pstart, 0), axis=-1)
    dest = (pstart_tok + rank).T.reshape(-1).astype(I32)
    tok_ids = jnp.repeat(jnp.arange(t, dtype=I32), TOP_K)
    slot_tok = jnp.zeros((nb * MOE_BLOCK,), I32).at[dest].set(tok_ids, unique_indices=True)
    blk_start = jnp.arange(nb, dtype=I32) * MOE_BLOCK
    blk_e = jnp.minimum(jnp.sum(pend[None, :] <= blk_start[:, None], axis=1), N_EXPERTS - 1).astype(I32)
    nused = (pend[-1:] // MOE_BLOCK).astype(I32)

    ys = _experts(blk_e, nused, slot_tok, hp,
                  w_e_gate[0].astype(BF16), w_e_up[0].astype(BF16), w_e_down[0].astype(BF16))
    base = _shared(hp, w_s_gate[0].astype(BF16), w_s_up[0].astype(BF16), w_s_down[0].astype(BF16), x1, ga_f)
    out = _combine(dest, ys, wts.T, base, ga_f, g_final.reshape(1, d))
    return out[None]
```

```python
import functools

import jax
import jax.numpy as jnp
from jax import lax
from jax.experimental import pallas as pl
from jax.experimental.pallas import tpu as pltpu

F32 = jnp.float32
BF16 = jnp.bfloat16
I32 = jnp.int32
U32 = jnp.uint32

EPS = 1e-6
LANES = 128
GLA_HK = 64
GLA_HV = 128
GLA_CHUNK = 64
GATE_RANK = 16
GATE_TAU = 16.0
CONV_WIDTH = 31
CONV_PAD = (CONV_WIDTH - 1) // 2
GRID_W = 64
N_EXPERTS = 64
N_GROUPS = 8
TOPK_GROUPS = 4
TOP_K = 8
ROUTED_SCALE = 2.5
MOE_BLOCK = 128
VMEM_LIMIT = 56 * 1024 * 1024

NT_DIMS = (((1,), (1,)), ((), ()))
TN_DIMS = (((0,), (0,)), ((), ()))


def _params(*sem):
    return pltpu.CompilerParams(dimension_semantics=sem, vmem_limit_bytes=VMEM_LIMIT)


def _tile(n, *preferred):
    for p in preferred:
        if n % p == 0:
            return p
    raise ValueError(f"no tile for {n} among {preferred}")


def _sigmoid(x):
    return 1.0 / (1.0 + jnp.exp(-x))


def _silu(x):
    return x * _sigmoid(x)


def _ada_kernel(cb_ref, w_ref, b_ref, o_ref, s_scr):
    @pl.when(pl.program_id(0) == 0)
    def _():
        s_scr[...] = _silu(cb_ref[...])

    d, tn = w_ref.shape
    for jt in range(tn // LANES):
        sl = slice(jt * LANES, (jt + 1) * LANES)
        w = w_ref[:, sl]
        for m in range(2):
            p = (w * s_scr[m]).reshape(d // 8, 8, LANES).sum(axis=0)
            o_ref[m:m + 1, sl] = p.sum(axis=0, keepdims=True) + b_ref[:, sl]


def _ada(c2, w_ada, b_ada):
    d, n = w_ada.shape
    tn = _tile(n, 1024, 512, LANES)
    cb = jnp.broadcast_to(c2[:, :, None], (2, d, LANES))
    return pl.pallas_call(
        _ada_kernel,
        grid=(n // tn,),
        in_specs=[pl.BlockSpec((2, d, LANES), lambda j: (0, 0, 0)),
                  pl.BlockSpec((d, tn), lambda j: (0, j)),
                  pl.BlockSpec((1, tn), lambda j: (0, j))],
        out_specs=pl.BlockSpec((2, tn), lambda j: (0, j)),
        out_shape=jax.ShapeDtypeStruct((2, n), F32),
        scratch_shapes=[pltpu.VMEM((2, d, LANES), F32)],
        compiler_params=_params("arbitrary"),
        name="ada",
    )(cb, w_ada, b_ada.reshape(1, n))


def _norm_mod(x, g, sh, sc):
    ms = jnp.mean(x * x, axis=-1, keepdims=True)
    y = x * lax.rsqrt(ms + EPS) * g
    return y * (1.0 + sc) + sh


def _proj_kernel(x_ref, g_ref, sh_ref, sc_ref, w_ref, o_ref, h_scr):
    @pl.when(pl.program_id(1) == 0)
    def _():
        h_scr[...] = _norm_mod(x_ref[...], g_ref[...], sh_ref[...], sc_ref[...]).astype(BF16)

    o_ref[...] = jnp.dot(h_scr[...], w_ref[...], preferred_element_type=F32)


def _proj(x, g, sh, sc, w):
    t, d = x.shape
    n = w.shape[1]
    tm = _tile(t, 512, 256, 128)
    tn = _tile(n, 1152, 384, LANES)
    vec = pl.BlockSpec((1, d), lambda i, j: (0, 0))
    return pl.pallas_call(
        _proj_kernel,
        grid=(t // tm, n // tn),
        in_specs=[pl.BlockSpec((tm, d), lambda i, j: (i, 0)), vec, vec, vec,
                  pl.BlockSpec((d, tn), lambda i, j: (0, j))],
        out_specs=pl.BlockSpec((tm, tn), lambda i, j: (i, j)),
        out_shape=jax.ShapeDtypeStruct((t, n), F32),
        scratch_shapes=[pltpu.VMEM((tm, d), BF16)],
        compiler_params=_params("arbitrary", "arbitrary"),
        name="proj",
    )(x, g, sh, sc, w)


def _gla_chunk(q, k, v2, low, wg, bg, s_t, tri, mask2, lane0, fwd, with_output):
    c = GLA_CHUNK
    z = jnp.dot(low.astype(BF16), wg, preferred_element_type=F32) + bg
    g = (jnp.minimum(z, 0.0) - jnp.log(1.0 + jnp.exp(-jnp.abs(z)))) * (1.0 / GATE_TAU)
    b = jnp.dot(tri, g, preferred_element_type=F32, precision=lax.Precision.HIGHEST)
    tot = b[c - 1:c] if fwd else b[0:1]
    mid = b[c // 2:c // 2 + 1]

    def stack(a):
        return jnp.concatenate([jnp.where(lane0, a, 0.0), jnp.where(lane0, 0.0, a)], axis=0).astype(BF16)

    vs = jnp.concatenate([v2[:, :GLA_HV], v2[:, GLA_HV:]], axis=0).astype(BF16)
    ks = stack(k * jnp.exp(tot - b))
    s_new = s_t * jnp.exp(tot) + lax.dot_general(vs, ks, TN_DIMS, preferred_element_type=F32)
    if not with_output:
        return s_new, None
    qs = q * (GLA_HK ** -0.5)
    qa = stack(qs * jnp.exp(b - mid))
    ka = stack(k * jnp.exp(mid - b))
    qi = stack(qs * jnp.exp(b))
    att = lax.dot_general(qa, ka, NT_DIMS, preferred_element_type=F32)
    att = jnp.where(mask2, att, 0.0).astype(BF16)
    o = jnp.dot(att, vs, preferred_element_type=F32)
    o = o + lax.dot_general(qi, s_t.astype(BF16), NT_DIMS, preferred_element_type=F32)
    return s_new, jnp.concatenate([o[:c], o[c:]], axis=1)


def _gla_kernel(*refs, cb, with_output):
    (qf, kf, vf, lf, qb, kb, vb, lb, wgf, wgb, bgf, bgb, s0_ref) = refs[:13]
    if with_output:
        of_ref, ob_ref, s_scr = refs[13:]
    else:
        sout_ref, s_scr = refs[13:]
    i = pl.program_id(1)
    c = GLA_CHUNK

    @pl.when(i == 0)
    def _():
        s_scr[...] = s0_ref[:, 0]

    row = lax.broadcasted_iota(I32, (c, c), 0)
    col = lax.broadcasted_iota(I32, (c, c), 1)
    tri_f = (col <= row).astype(F32)
    tri_b = (col >= row).astype(F32)
    r2 = lax.broadcasted_iota(I32, (2 * c, 2 * c), 0)
    c2 = lax.broadcasted_iota(I32, (2 * c, 2 * c), 1)
    same = (r2 >= c) == (c2 >= c)
    mask_f = same & (c2 <= r2)
    mask_b = same & (c2 >= r2)
    lane0 = lax.broadcasted_iota(I32, (c, 2 * GLA_HK), 1) < GLA_HK

    s_f = s_scr[0]
    s_b = s_scr[1]
    for n in range(cb):
        rf = slice(n * c, (n + 1) * c)
        rb = slice((cb - 1 - n) * c, (cb - n) * c)
        s_f, o_f = _gla_chunk(qf[rf], kf[rf], vf[rf], lf[rf], wgf[...], bgf[...], s_f,
                              tri_f, mask_f, lane0, True, with_output)
        s_b, o_b = _gla_chunk(qb[rb], kb[rb], vb[rb], lb[rb], wgb[...], bgb[...], s_b,
                              tri_b, mask_b, lane0, False, with_output)
        if with_output:
            of_ref[rf] = o_f
            ob_ref[rb] = o_b
    s_scr[0] = s_f
    s_scr[1] = s_b
    if not with_output:
        @pl.when(i == pl.num_programs(1) - 1)
        def _():
            sout_ref[:, 0] = s_scr[...]


def _gla(proj, wg_f, wg_b, bg_f, bg_b, s0, with_output, col_q, col_k, col_v, col_low):
    t = proj.shape[0]
    cb = _tile(t // GLA_CHUNK, 8, 4, 2, 1)
    rows = cb * GLA_CHUNK
    nb = t // rows
    npairs = wg_f.shape[1] // (2 * GLA_HK)
    pw = 2 * GLA_HK
    vw = 2 * GLA_HV
    fi = lambda p, i: i
    bi = lambda p, i: nb - 1 - i

    def spec(width, col0, blk):
        return pl.BlockSpec((rows, width), lambda p, i: (blk(p, i), col0 // width + p))

    def low_spec(blk):
        return pl.BlockSpec((rows, LANES), lambda p, i: (blk(p, i), col_low // LANES))

    in_specs = [spec(pw, col_q, fi), spec(pw, col_k, fi), spec(vw, col_v, fi), low_spec(fi),
                spec(pw, col_q, bi), spec(pw, col_k, bi), spec(vw, col_v, bi), low_spec(bi),
                pl.BlockSpec((LANES, pw), lambda p, i: (0, p)),
                pl.BlockSpec((LANES, pw), lambda p, i: (0, p)),
                pl.BlockSpec((1, pw), lambda p, i: (0, p)),
                pl.BlockSpec((1, pw), lambda p, i: (0, p)),
                pl.BlockSpec((2, 1, GLA_HV, pw), lambda p, i: (0, p, 0, 0))]
    if with_output:
        out_specs = [pl.BlockSpec((rows, vw), lambda p, i: (i, p)),
                     pl.BlockSpec((rows, vw), lambda p, i: (nb - 1 - i, p))]
        out_shape = [jax.ShapeDtypeStruct((t, npairs * 2 * GLA_HV), F32)] * 2
    else:
        out_specs = pl.BlockSpec((2, 1, GLA_HV, pw), lambda p, i: (0, p, 0, 0))
        out_shape = jax.ShapeDtypeStruct((2, npairs, GLA_HV, pw), F32)
    return pl.pallas_call(
        functools.partial(_gla_kernel, cb=cb, with_output=with_output),
        grid=(npairs, nb),
        in_specs=in_specs,
        out_specs=out_specs,
        out_shape=out_shape,
        scratch_shapes=[pltpu.VMEM((2, GLA_HV, pw), F32)],
        compiler_params=_params("arbitrary", "arbitrary"),
        name="gla_out" if with_output else "gla_state",
    )(proj, proj, proj, proj, proj, proj, proj, proj, wg_f, wg_b, bg_f, bg_b, s0)


def _mix_kernel(of_ref, ob_ref, r_ref, ca_ref, cb_ref, gout_ref, wdw_ref, bdw_ref, gln_ref, bln_ref,
                o_ref, upad, y_scr):
    tm = of_ref.shape[0]
    dg = of_ref.shape[1]
    dc = ca_ref.shape[1]
    nrow = tm // GRID_W
    lead = 16
    for h in range(dg // GLA_HV):
        sl = slice(h * GLA_HV, (h + 1) * GLA_HV)
        o = of_ref[:, sl] + ob_ref[:, sl]
        o = o * lax.rsqrt(jnp.mean(o * o, axis=-1, keepdims=True) + EPS) * gout_ref[:, sl]
        o_ref[:, sl] = (o * _silu(r_ref[:, sl])).astype(o_ref.dtype)
    for r in range(nrow):
        rs = slice(r * GRID_W, (r + 1) * GRID_W)
        upad[r, 0:lead, :] = jnp.zeros((lead, dc), F32)
        upad[r, lead:lead + GRID_W, :] = ca_ref[rs, :] * _sigmoid(cb_ref[rs, :])
        upad[r, lead + GRID_W:, :] = jnp.zeros((lead, dc), F32)
    cw = 256

    def row_body(r, carry):
        for cc in range(dc // cw):
            cs = slice(cc * cw, (cc + 1) * cw)
            acc = jnp.zeros((GRID_W, cw), F32) + bdw_ref[:, cs]
            for j in range(CONV_WIDTH):
                off = lead - CONV_PAD + j
                acc = acc + wdw_ref[j:j + 1, cs] * upad[r, off:off + GRID_W, cs]
            y_scr[r, :, cs] = acc
        return carry

    lax.fori_loop(0, nrow, row_body, 0)
    for r in range(nrow):
        y = y_scr[r]
        mu = jnp.mean(y, axis=-1, keepdims=True)
        yc = y - mu
        var = jnp.mean(yc * yc, axis=-1, keepdims=True)
        yn = yc * lax.rsqrt(var + EPS) * gln_ref[...] + bln_ref[...]
        o_ref[r * GRID_W:(r + 1) * GRID_W, dg:] = _silu(yn).astype(o_ref.dtype)


def _mix(o_f, o_b, proj, g_out, w_dw, b_dw, g_ln, b_ln, col_r, col_ca, col_cb):
    t, dg = o_f.shape
    tm = _tile(t, 256, 128, GRID_W)
    dc = w_dw.shape[1]
    nrow = tm // GRID_W
    row = lambda width: pl.BlockSpec((1, width), lambda i: (0, 0))
    return pl.pallas_call(
        _mix_kernel,
        grid=(t // tm,),
        in_specs=[pl.BlockSpec((tm, dg), lambda i: (i, 0)),
                  pl.BlockSpec((tm, dg), lambda i: (i, 0)),
                  pl.BlockSpec((tm, dg), lambda i: (i, col_r // dg)),
                  pl.BlockSpec((tm, dc), lambda i: (i, col_ca // dc)),
                  pl.BlockSpec((tm, dc), lambda i: (i, col_cb // dc)),
                  row(dg),
                  pl.BlockSpec((CONV_WIDTH, dc), lambda i: (0, 0)),
                  row(dc), row(dc), row(dc)],
        out_specs=pl.BlockSpec((tm, dg + dc), lambda i: (i, 0)),
        out_shape=jax.ShapeDtypeStruct((t, dg + dc), BF16),
        scratch_shapes=[pltpu.VMEM((nrow, GRID_W + 32, dc), F32),
                        pltpu.VMEM((nrow, GRID_W, dc), F32)],
        compiler_params=_params("arbitrary"),
        name="mix",
    )(o_f, o_b, proj, proj, proj, g_out, w_dw, b_dw, g_ln, b_ln)


def _outproj_kernel(m_ref, w_ref, x_ref, ga_ref, o_ref):
    o_ref[...] = x_ref[...] + ga_ref[...] * jnp.dot(m_ref[...], w_ref[...], preferred_element_type=F32)


def _outproj(mix, w, x, ga):
    t, k = mix.shape
    n = w.shape[1]
    tm = _tile(t, 1024, 512, 256)
    tn = _tile(n, 512, 256, LANES)
    return pl.pallas_call(
        _outproj_kernel,
        grid=(t // tm, n // tn),
        in_specs=[pl.BlockSpec((tm, k), lambda i, j: (i, 0)),
                  pl.BlockSpec((k, tn), lambda i, j: (0, j)),
                  pl.BlockSpec((tm, tn), lambda i, j: (i, j)),
                  pl.BlockSpec((1, tn), lambda i, j: (0, j))],
        out_specs=pl.BlockSpec((tm, tn), lambda i, j: (i, j)),
        out_shape=jax.ShapeDtypeStruct((t, n), F32),
        compiler_params=_params("arbitrary", "arbitrary"),
        name="out_proj",
    )(mix, w, x, ga)


def _pack_halves(h):
    c = h.shape[1] // 2
    lo = lax.bitcast_convert_type(h[:, :c].astype(BF16).astype(F32), U32)
    hi = lax.bitcast_convert_type(h[:, c:].astype(BF16).astype(F32), U32)
    return (lo >> 16) | (hi & jnp.uint32(0xFFFF0000))


def _unpack_halves(p):
    lo = lax.bitcast_convert_type(p << 16, F32).astype(BF16)
    hi = lax.bitcast_convert_type(p & jnp.uint32(0xFFFF0000), F32).astype(BF16)
    return jnp.concatenate([lo, hi], axis=1)


def _route_kernel(x_ref, g_ref, sh_ref, sc_ref, wr_ref, br_ref, hp_ref, e_ref, w_ref, rk_ref, cnt_ref,
                  carry):
    tm = x_ref.shape[0]
    ne = N_EXPERTS
    gs = ne // N_GROUPS
    neg = -jnp.inf

    @pl.when(pl.program_id(0) == 0)
    def _():
        carry[...] = jnp.zeros_like(carry)

    h = _norm_mod(x_ref[...], g_ref[...], sh_ref[...], sc_ref[...])
    hp_ref[...] = _pack_halves(h)
    logits = lax.dot_general(wr_ref[...], h, NT_DIMS, preferred_element_type=F32,
                             precision=lax.Precision.HIGHEST)
    scores = _sigmoid(logits)
    sel = scores + br_ref[...]
    sel3 = sel.reshape(N_GROUPS, gs, tm)
    sub = lax.broadcasted_iota(I32, (N_GROUPS, gs, tm), 1)
    m1 = jnp.max(sel3, axis=1, keepdims=True)
    i1 = jnp.min(jnp.where(sel3 == m1, sub, gs), axis=1, keepdims=True)
    m2 = jnp.max(jnp.where(sub == i1, neg, sel3), axis=1, keepdims=True)
    gscore = (m1 + m2).reshape(N_GROUPS, tm)
    gid = lax.broadcasted_iota(I32, (N_GROUPS, tm), 0)
    gmask = jnp.zeros((N_GROUPS, tm), F32)
    for _ in range(TOPK_GROUPS):
        mx = jnp.max(gscore, axis=0, keepdims=True)
        idx = jnp.min(jnp.where(gscore == mx, gid, N_GROUPS), axis=0, keepdims=True)
        pick = gid == idx
        gmask = jnp.where(pick, 1.0, gmask)
        gscore = jnp.where(pick, neg, gscore)
    emask = jnp.broadcast_to(gmask.reshape(N_GROUPS, 1, tm), (N_GROUPS, gs, tm)).reshape(ne, tm)
    cand = jnp.where(emask > 0.0, sel, neg)
    eid = lax.broadcasted_iota(I32, (ne, tm), 0)
    chosen = jnp.zeros((ne, tm), F32)
    idxs, scs = [], []
    for _ in range(TOP_K):
        mx = jnp.max(cand, axis=0, keepdims=True)
        idx = jnp.min(jnp.where(cand == mx, eid, ne), axis=0, keepdims=True)
        pick = eid == idx
        idxs.append(idx)
        scs.append(jnp.sum(jnp.where(pick, scores, 0.0), axis=0, keepdims=True))
        chosen = jnp.where(pick, 1.0, chosen)
        cand = jnp.where(pick, neg, cand)
    ssum = scs[0]
    for s in scs[1:]:
        ssum = ssum + s
    before = (lax.broadcasted_iota(I32, (tm, tm), 0) < lax.broadcasted_iota(I32, (tm, tm), 1))
    prior = jnp.dot(chosen.astype(BF16), before.astype(BF16), preferred_element_type=F32) + carry[...]
    for k in range(TOP_K):
        e_ref[k:k + 1, :] = idxs[k]
        w_ref[k:k + 1, :] = scs[k] / ssum * ROUTED_SCALE
        rk = jnp.sum(jnp.where(eid == idxs[k], prior, 0.0), axis=0, keepdims=True)
        rk_ref[k:k + 1, :] = rk.astype(I32)
    carry[...] = carry[...] + jnp.sum(chosen, axis=1, keepdims=True)
    cnt_ref[...] = jnp.broadcast_to(carry[...], cnt_ref.shape).astype(I32)


def _route(x, g, sh, sc, w_router_t, b_router):
    t, d = x.shape
    tm = _tile(t, 256, LANES)
    vec = pl.BlockSpec((1, d), lambda i: (0, 0))
    tok = lambda dt: jax.ShapeDtypeStruct((TOP_K, t), dt)
    tok_spec = pl.BlockSpec((TOP_K, tm), lambda i: (0, i))
    return pl.pallas_call(
        _route_kernel,
        grid=(t // tm,),
        in_specs=[pl.BlockSpec((tm, d), lambda i: (i, 0)), vec, vec, vec,
                  pl.BlockSpec((N_EXPERTS, d), lambda i: (0, 0)),
                  pl.BlockSpec((N_EXPERTS, 1), lambda i: (0, 0))],
        out_specs=[pl.BlockSpec((tm, d // 2), lambda i: (i, 0)), tok_spec, tok_spec, tok_spec,
                   pl.BlockSpec((N_EXPERTS, LANES), lambda i: (0, 0))],
        out_shape=[jax.ShapeDtypeStruct((t, d // 2), U32), tok(I32), tok(F32), tok(I32),
                   jax.ShapeDtypeStruct((N_EXPERTS, LANES), I32)],
        scratch_shapes=[pltpu.VMEM((N_EXPERTS, 1), F32)],
        compiler_params=_params("arbitrary"),
        name="route",
    )(x, g, sh, sc, w_router_t, b_router)


WUNITS = 4


def _experts_kernel(blk_run, run_e, meta, slot_tok, hp_hbm, wg_hbm, wu_hbm, wd_hbm, o_ref,
                    xbuf, sem, sg, su, sd, wsem, wgb, wub, wdb, done_ref):
    b = pl.program_id(0)
    nb = pl.num_programs(0)
    nused = meta[0]
    nunits = meta[1] * WUNITS
    ug = sg.shape[1]
    ud = sd.shape[1]

    def unit_copies(u):
        e = run_e[u // WUNITS]
        k = u % WUNITS
        s = u % 2
        return (pltpu.make_async_copy(wg_hbm.at[e, pl.ds(k * ug, ug)], sg.at[s], wsem.at[s]),
                pltpu.make_async_copy(wu_hbm.at[e, pl.ds(k * ug, ug)], su.at[s], wsem.at[s]),
                pltpu.make_async_copy(wd_hbm.at[e, pl.ds(k * ud, ud)], sd.at[s], wsem.at[s]))

    def start_unit(u):
        for cp in unit_copies(u):
            cp.start()

    def process_unit(u):
        for cp in unit_copies(u):
            cp.wait()

        @pl.when(u + 1 < nunits)
        def _():
            start_unit(u + 1)

        r = (u // WUNITS) % 2
        k = u % WUNITS
        s = u % 2
        wgb[r, pl.ds(pl.multiple_of(k * ug, ug), ug), :] = sg[s].astype(BF16)
        wub[r, pl.ds(pl.multiple_of(k * ug, ug), ug), :] = su[s].astype(BF16)
        wdb[r, pl.ds(pl.multiple_of(k * ud, ud), ud), :] = sd[s].astype(BF16)

    def start_gather(blk, slot):
        for r in range(MOE_BLOCK):
            tok = slot_tok[blk * MOE_BLOCK + r]
            pltpu.make_async_copy(hp_hbm.at[pl.ds(tok, 1)], xbuf.at[slot, pl.ds(r, 1)],
                                  sem.at[slot]).start(priority=r % 2)

    def wait_gather(slot):
        pltpu.make_async_copy(hp_hbm.at[pl.ds(0, MOE_BLOCK)], xbuf.at[slot], sem.at[slot]).wait()

    @pl.when(b == 0)
    def _():
        done_ref[0] = 0
        start_unit(0)
        start_gather(0, 0)

    live = (b < nused) | (b == 0)
    next_live = (b + 1 < nb) & (b + 1 < nused)
    run = blk_run[b]

    @pl.when(live)
    def _():
        done = done_ref[0]
        need = jnp.maximum(WUNITS * (run + 1) - done, 0)
        ahead = (done + need < jnp.minimum(nunits, WUNITS * (run + 2))).astype(I32)

        def body(n, carry):
            process_unit(done + n)
            return carry
        lax.fori_loop(0, need + ahead, body, 0)
        done_ref[0] = done + need + ahead

    def compute(gather_next):
        slot = b % 2
        r = run % 2
        wait_gather(slot)
        if gather_next:
            start_gather(b + 1, 1 - slot)
        x = _unpack_halves(xbuf[slot])
        hg = jnp.dot(x, wgb[r], preferred_element_type=F32)
        hu = jnp.dot(x, wub[r], preferred_element_type=F32)
        hb = (_silu(hg) * hu).astype(BF16)
        o_ref[...] = jnp.dot(hb, wdb[r], preferred_element_type=F32)

    pl.when(live & next_live)(functools.partial(compute, True))
    pl.when(live & jnp.logical_not(next_live))(functools.partial(compute, False))

    @pl.when(jnp.logical_not(live))
    def _():
        o_ref[...] = jnp.zeros_like(o_ref)


def _experts(blk_run, run_e, meta, slot_tok, hp, wg, wu, wd):
    nb = blk_run.shape[0]
    _, d, de = wg.shape
    ug, ud = d // WUNITS, de // WUNITS
    hbm = pl.BlockSpec(memory_space=pl.ANY)
    grid_spec = pltpu.PrefetchScalarGridSpec(
        num_scalar_prefetch=4,
        grid=(nb,),
        in_specs=[hbm, hbm, hbm, hbm],
        out_specs=pl.BlockSpec((MOE_BLOCK, d), lambda b, *_: (b, 0)),
        scratch_shapes=[pltpu.VMEM((2, MOE_BLOCK, d // 2), U32), pltpu.SemaphoreType.DMA((2,)),
                        pltpu.VMEM((2, ug, de), F32), pltpu.VMEM((2, ug, de), F32),
                        pltpu.VMEM((2, ud, d), F32), pltpu.SemaphoreType.DMA((2,)),
                        pltpu.VMEM((2, d, de), BF16), pltpu.VMEM((2, d, de), BF16),
                        pltpu.VMEM((2, de, d), BF16), pltpu.SMEM((1,), I32)],
    )
    return pl.pallas_call(
        _experts_kernel,
        grid_spec=grid_spec,
        out_shape=jax.ShapeDtypeStruct((nb * MOE_BLOCK, d), F32),
        compiler_params=_params("arbitrary"),
        name="experts",
    )(blk_run, run_e, meta, slot_tok, hp, wg, wu, wd)


def _shared_kernel(hp_ref, wg_ref, wu_ref, wd_ref, x_ref, ga_ref, o_ref):
    x = _unpack_halves(hp_ref[...])
    hg = jnp.dot(x, wg_ref[...], preferred_element_type=F32)
    hu = jnp.dot(x, wu_ref[...], preferred_element_type=F32)
    hb = (_silu(hg) * hu).astype(BF16)
    o_ref[...] = x_ref[...] + ga_ref[...] * jnp.dot(hb, wd_ref[...], preferred_element_type=F32)


def _shared(hp, wg, wu, wd, x, ga):
    t, d = x.shape
    tm = _tile(t, 256, LANES)
    ds_ = wg.shape[1]
    return pl.pallas_call(
        _shared_kernel,
        grid=(t // tm,),
        in_specs=[pl.BlockSpec((tm, d // 2), lambda i: (i, 0)),
                  pl.BlockSpec((d, ds_), lambda i: (0, 0)),
                  pl.BlockSpec((d, ds_), lambda i: (0, 0)),
                  pl.BlockSpec((ds_, d), lambda i: (0, 0)),
                  pl.BlockSpec((tm, d), lambda i: (i, 0)),
                  pl.BlockSpec((1, d), lambda i: (0, 0))],
        out_specs=pl.BlockSpec((tm, d), lambda i: (i, 0)),
        out_shape=jax.ShapeDtypeStruct((t, d), F32),
        compiler_params=_params("arbitrary"),
        name="shared",
    )(hp, wg, wu, wd, x, ga)


def _combine_kernel(dest, ys_hbm, w_ref, base_ref, ga_ref, gf_ref, o_ref, gbuf, sem):
    i = pl.program_id(0)
    n = pl.num_programs(0)
    tm = base_ref.shape[0]

    def start_gather(tile, slot):
        for j in range(tm * TOP_K):
            src = dest[tile * (tm * TOP_K) + j]
            pltpu.make_async_copy(ys_hbm.at[pl.ds(src, 1)], gbuf.at[slot, j % TOP_K, pl.ds(j // TOP_K, 1)],
                                  sem.at[slot]).start(priority=j % 2)

    def compute(gather_next):
        slot = i % 2
        for k in range(TOP_K):
            pltpu.make_async_copy(ys_hbm.at[pl.ds(0, tm)], gbuf.at[slot, k], sem.at[slot]).wait()
        if gather_next:
            start_gather(i + 1, 1 - slot)
        y = w_ref[:, 0:1] * gbuf[slot, 0]
        for k in range(1, TOP_K):
            y = y + w_ref[:, k:k + 1] * gbuf[slot, k]
        x = base_ref[...] + ga_ref[...] * y
        o_ref[...] = x * lax.rsqrt(jnp.mean(x * x, axis=-1, keepdims=True) + EPS) * gf_ref[...]

    @pl.when(i == 0)
    def _():
        start_gather(0, 0)

    pl.when(i + 1 < n)(functools.partial(compute, True))
    pl.when(i + 1 == n)(functools.partial(compute, False))


def _combine(dest, ys, wts, base, ga, g_final):
    t, d = base.shape
    tm = _tile(t, 64)
    grid_spec = pltpu.PrefetchScalarGridSpec(
        num_scalar_prefetch=1,
        grid=(t // tm,),
        in_specs=[pl.BlockSpec(memory_space=pl.ANY),
                  pl.BlockSpec((tm, TOP_K), lambda i, s: (i, 0)),
                  pl.BlockSpec((tm, d), lambda i, s: (i, 0)),
                  pl.BlockSpec((1, d), lambda i, s: (0, 0)),
                  pl.BlockSpec((1, d), lambda i, s: (0, 0))],
        out_specs=pl.BlockSpec((tm, d), lambda i, s: (i, 0)),
        scratch_shapes=[pltpu.VMEM((2, TOP_K, tm, d), F32), pltpu.SemaphoreType.DMA((2,))],
    )
    return pl.pallas_call(
        _combine_kernel,
        grid_spec=grid_spec,
        out_shape=jax.ShapeDtypeStruct((t, d), F32),
        compiler_params=_params("arbitrary"),
        name="combine",
    )(dest, ys, wts, base, ga, g_final)


def kernel(x, c, ctx, c_ctx, w_ada, b_ada, g_norm_mix, g_norm_ffn, w_in, w_gate_up, b_gate_up,
           g_gla_out, w_dw, b_dw, g_conv_ln, b_conv_ln, w_out, w_router, b_router,
           w_e_gate, w_e_up, w_e_down, w_s_gate, w_s_up, w_s_down, g_final):
    assert x.shape[0] == 1 and w_ada.shape[0] == 1, "single batch element, single layer"
    t, d = x.shape[1], x.shape[2]
    dk = w_gate_up.shape[3]
    heads = dk // GLA_HK
    dg = heads * GLA_HV
    dc = w_dw.shape[2]
    xl = x[0]
    xc = ctx[0]

    col_q, col_k, col_v = 0, dk, 2 * dk
    col_r = col_v + dg
    col_ca = col_r + dg
    col_cb = col_ca + dc
    col_low = col_cb + dc
    n_gate = 2 * GATE_RANK
    wi = w_in[0]
    w_all = jnp.concatenate([wi[:, :col_ca], wi[:, col_ca + n_gate:], wi[:, col_ca:col_ca + n_gate],
                             jnp.zeros((d, LANES - n_gate), F32)], axis=1).astype(BF16)

    mod = _ada(jnp.stack([c[0], c_ctx]), w_ada[0], b_ada[0])
    sh_m, sc_m, ga_m, sh_f, sc_f, ga_f = [mod[0:1, k * d:(k + 1) * d] for k in range(6)]
    csh_m, csc_m = mod[1:2, 0:d], mod[1:2, d:2 * d]

    wg_f = jnp.zeros((LANES, dk), F32).at[:GATE_RANK].set(w_gate_up[0, 0]).astype(BF16)
    wg_b = jnp.zeros((LANES, dk), F32).at[GATE_RANK:n_gate].set(w_gate_up[0, 1]).astype(BF16)
    bg_f = b_gate_up[0, 0:1]
    bg_b = b_gate_up[0, 1:2]
    cols = dict(col_q=col_q, col_k=col_k, col_v=col_v, col_low=col_low)

    gn_mix = g_norm_mix[0:1]
    proj_c = _proj(xc, gn_mix, csh_m, csc_m, w_all)
    s0 = jnp.zeros((2, heads // 2, GLA_HV, 2 * GLA_HK), F32)
    s_ctx = _gla(proj_c, wg_f, wg_b, bg_f, bg_b, s0, with_output=False, **cols)

    proj_l = _proj(xl, gn_mix, sh_m, sc_m, w_all)
    o_f, o_b = _gla(proj_l, wg_f, wg_b, bg_f, bg_b, s_ctx, with_output=True, **cols)
    mix = _mix(o_f, o_b, proj_l, g_gla_out[0:1], w_dw[0], b_dw[0:1], g_conv_ln[0:1], b_conv_ln[0:1],
               col_r, col_ca, col_cb)
    x1 = _outproj(mix, w_out[0].astype(BF16), xl, ga_m)

    hp, eidx, wts, rank, cnt = _route(x1, g_norm_ffn[0:1], sh_f, sc_f, w_router[0].T,
                                      b_router[0].reshape(N_EXPERTS, 1))
    nk = t * TOP_K
    nb = (nk + MOE_BLOCK - 1) // MOE_BLOCK + N_EXPERTS
    counts = cnt[:, 0]
    pcounts = (counts + MOE_BLOCK - 1) // MOE_BLOCK * MOE_BLOCK
    pend = jnp.cumsum(pcounts)
    pstart = pend - pcounts
    experts = jnp.arange(N_EXPERTS, dtype=I32)
    pstart_tok = jnp.sum(jnp.where(eidx[:, :, None] == experts, pstart, 0), axis=-1)
    dest = (pstart_tok + rank).T.reshape(-1).astype(I32)
    tok_ids = jnp.repeat(jnp.arange(t, dtype=I32), TOP_K)
    slot_tok = jnp.zeros((nb * MOE_BLOCK,), I32).at[dest].set(tok_ids, unique_indices=True)
    blk_start = jnp.arange(nb, dtype=I32) * MOE_BLOCK
    blk_e = jnp.minimum(jnp.sum(pend[None, :] <= blk_start[:, None], axis=1), N_EXPERTS - 1).astype(I32)
    owns = counts > 0
    run_of_e = jnp.cumsum(owns.astype(I32)) - 1
    run_e = jnp.sum(jnp.where(owns[None, :] & (run_of_e[None, :] == experts[:, None]), experts[None, :], 0),
                    axis=1).astype(I32)
    blk_run = jnp.sum(jnp.where(blk_e[:, None] == experts[None, :], run_of_e[None, :], 0), axis=1).astype(I32)
    meta = jnp.stack([pend[-1] // MOE_BLOCK, jnp.sum(owns.astype(I32))]).astype(I32)

    ys = _experts(blk_run, run_e, meta, slot_tok, hp, w_e_gate[0], w_e_up[0], w_e_down[0])
    base = _shared(hp, w_s_gate[0].astype(BF16), w_s_up[0].astype(BF16), w_s_down[0].astype(BF16), x1, ga_f)
    out = _combine(dest, ys, wts.T, base, ga_f, g_final.reshape(1, d))
    return out[None]
```

```python
import functools

import jax
import jax.numpy as jnp
from jax import lax
from jax.experimental import pallas as pl
from jax.experimental.pallas import tpu as pltpu

F32 = jnp.float32
BF16 = jnp.bfloat16
I32 = jnp.int32
U32 = jnp.uint32

EPS = 1e-6
LANES = 128
GLA_HK = 64
GLA_HV = 128
GLA_CHUNK = 64
GATE_RANK = 16
GATE_TAU = 16.0
CONV_WIDTH = 31
CONV_PAD = (CONV_WIDTH - 1) // 2
GRID_W = 64
N_EXPERTS = 64
N_GROUPS = 8
TOPK_GROUPS = 4
TOP_K = 8
ROUTED_SCALE = 2.5
MOE_BLOCK = 128
VMEM_LIMIT = 56 * 1024 * 1024

NT_DIMS = (((1,), (1,)), ((), ()))
TN_DIMS = (((0,), (0,)), ((), ()))


def _params(*sem):
    return pltpu.CompilerParams(dimension_semantics=sem, vmem_limit_bytes=VMEM_LIMIT)


def _tile(n, *preferred):
    for p in preferred:
        if n % p == 0:
            return p
    raise ValueError(f"no tile for {n} among {preferred}")


def _sigmoid(x):
    return 1.0 / (1.0 + jnp.exp(-x))


def _silu(x):
    return x * _sigmoid(x)


def _ada_kernel(cb_ref, w_ref, b_ref, o_ref, s_scr):
    @pl.when(pl.program_id(0) == 0)
    def _():
        s_scr[...] = _silu(cb_ref[...])

    d, tn = w_ref.shape
    for jt in range(tn // LANES):
        sl = slice(jt * LANES, (jt + 1) * LANES)
        w = w_ref[:, sl]
        for m in range(2):
            p = (w * s_scr[m]).reshape(d // 8, 8, LANES).sum(axis=0)
            o_ref[m:m + 1, sl] = p.sum(axis=0, keepdims=True) + b_ref[:, sl]


def _ada(c2, w_ada, b_ada):
    d, n = w_ada.shape
    tn = _tile(n, 1024, 512, LANES)
    cb = jnp.broadcast_to(c2[:, :, None], (2, d, LANES))
    return pl.pallas_call(
        _ada_kernel,
        grid=(n // tn,),
        in_specs=[pl.BlockSpec((2, d, LANES), lambda j: (0, 0, 0)),
                  pl.BlockSpec((d, tn), lambda j: (0, j)),
                  pl.BlockSpec((1, tn), lambda j: (0, j))],
        out_specs=pl.BlockSpec((2, tn), lambda j: (0, j)),
        out_shape=jax.ShapeDtypeStruct((2, n), F32),
        scratch_shapes=[pltpu.VMEM((2, d, LANES), F32)],
        compiler_params=_params("arbitrary"),
        name="ada",
    )(cb, w_ada, b_ada.reshape(1, n))


def _norm_mod(x, g, sh, sc):
    ms = jnp.mean(x * x, axis=-1, keepdims=True)
    y = x * lax.rsqrt(ms + EPS) * g
    return y * (1.0 + sc) + sh


def _proj_kernel(x_ref, g_ref, sh_ref, sc_ref, wa_ref, wb_ref, o_ref, h_scr, *, na):
    j = pl.program_id(1)

    @pl.when(j == 0)
    def _():
        h_scr[...] = _norm_mod(x_ref[...], g_ref[...], sh_ref[...], sc_ref[...]).astype(BF16)

    @pl.when(j < na)
    def _():
        o_ref[...] = jnp.dot(h_scr[...], wa_ref[...], preferred_element_type=F32)

    @pl.when(j >= na)
    def _():
        o_ref[...] = jnp.dot(h_scr[...], wb_ref[...], preferred_element_type=F32)


def _proj(x, g, sh, sc, wa, wb):
    _, t, d = x.shape
    tm = _tile(t, 512, 256, 128)
    tn = _tile(wa.shape[1], 1024, 256)
    na, nb = wa.shape[1] // tn, wb.shape[1] // tn
    assert wb.shape[1] % tn == 0
    vec = pl.BlockSpec((1, d), lambda i, j: (0, 0))
    return pl.pallas_call(
        functools.partial(_proj_kernel, na=na),
        grid=(t // tm, na + nb),
        in_specs=[pl.BlockSpec((None, tm, d), lambda i, j: (0, i, 0), pipeline_mode=pl.Buffered(1)),
                  vec, vec, vec,
                  pl.BlockSpec((d, tn), lambda i, j: (0, jnp.minimum(j, na - 1))),
                  pl.BlockSpec((d, tn), lambda i, j: (0, jnp.maximum(j - na, 0)))],
        out_specs=pl.BlockSpec((tm, tn), lambda i, j: (i, j)),
        out_shape=jax.ShapeDtypeStruct((t, (na + nb) * tn), F32),
        scratch_shapes=[pltpu.VMEM((tm, d), BF16)],
        compiler_params=_params("arbitrary", "arbitrary"),
        name="proj",
    )(x, g, sh, sc, wa, wb)


def _gla_block(q_ref, k_ref, v_ref, low_ref, wg, bg, tri, s_t, o_ref, fwd, with_output):
    c = GLA_CHUNK
    nch = q_ref.shape[0] // c
    z = jnp.dot(low_ref[...].astype(BF16), wg, preferred_element_type=F32) + bg
    g = (jnp.minimum(z, 0.0) - jnp.log(1.0 + jnp.exp(-jnp.abs(z)))) * (1.0 / GATE_TAU)
    g1 = g.astype(BF16)
    r1 = g - g1.astype(F32)
    g2 = r1.astype(BF16)
    g3 = (r1 - g2.astype(F32)).astype(BF16)
    b3 = jnp.dot(tri, jnp.concatenate([g1, g2, g3], axis=1), preferred_element_type=F32)
    b = b3[:, :LANES] + b3[:, LANES:2 * LANES] + b3[:, 2 * LANES:]

    def chunk_rows(n):
        return slice(n * c, (n + 1) * c)

    def per_chunk(row):
        return jnp.concatenate([jnp.broadcast_to(b[n * c + row:n * c + row + 1], (c, LANES))
                                for n in range(nch)], axis=0)

    tot = per_chunk(c - 1 if fwd else 0)
    lane0 = lax.broadcasted_iota(I32, (c, LANES), 1) < GLA_HK

    def stack(a, n):
        an = a[chunk_rows(n)]
        return jnp.concatenate([jnp.where(lane0, an, 0.0), jnp.where(lane0, 0.0, an)], axis=0).astype(BF16)

    k = k_ref[...]
    ks = k * jnp.exp(tot - b)
    vs = [jnp.concatenate([v_ref[chunk_rows(n), :GLA_HV], v_ref[chunk_rows(n), GLA_HV:]],
                          axis=0).astype(BF16) for n in range(nch)]
    upd = [lax.dot_general(vs[n], stack(ks, n), TN_DIMS, preferred_element_type=F32) for n in range(nch)]
    decay = jnp.exp(tot)
    order = range(nch) if fwd else range(nch - 1, -1, -1)
    s_in = [None] * nch
    for n in order:
        s_in[n] = s_t
        s_t = s_t * decay[n * c:n * c + 1] + upd[n]
    if not with_output:
        return s_t
    mid = per_chunk(c // 2)
    qs = q_ref[...] * (GLA_HK ** -0.5)
    qa = qs * jnp.exp(b - mid)
    ka = k * jnp.exp(mid - b)
    qi = qs * jnp.exp(b)
    r2 = lax.broadcasted_iota(I32, (2 * c, 2 * c), 0)
    c2 = lax.broadcasted_iota(I32, (2 * c, 2 * c), 1)
    keep = ((r2 >= c) == (c2 >= c)) & ((c2 <= r2) if fwd else (c2 >= r2))
    for n in range(nch):
        att = lax.dot_general(stack(qa, n), stack(ka, n), NT_DIMS, preferred_element_type=F32)
        att = jnp.where(keep, att, 0.0).astype(BF16)
        o = jnp.dot(att, vs[n], preferred_element_type=F32)
        o = o + lax.dot_general(stack(qi, n), s_in[n].astype(BF16), NT_DIMS, preferred_element_type=F32)
        o_ref[chunk_rows(n)] = jnp.concatenate([o[:c], o[c:]], axis=1)
    return s_t


def _gla_kernel(*refs, with_output):
    (qf, kf, vf, lf, qb, kb, vb, lb, wgf, wgb, bgf, bgb, s0_ref) = refs[:13]
    if with_output:
        of_ref, ob_ref, s_scr, tri_scr = refs[13:]
    else:
        sout_ref, s_scr, tri_scr = refs[13:]
        of_ref = ob_ref = None
    i = pl.program_id(1)
    c = GLA_CHUNK
    rows = qf.shape[0]

    @pl.when((pl.program_id(0) == 0) & (i == 0))
    def _():
        r = lax.broadcasted_iota(I32, (rows, rows), 0)
        cc = lax.broadcasted_iota(I32, (rows, rows), 1)
        shift = c.bit_length() - 1
        same = jnp.right_shift(r, shift) == jnp.right_shift(cc, shift)
        tri_scr[0] = jnp.where(same & (cc <= r), 1.0, 0.0).astype(BF16)
        tri_scr[1] = jnp.where(same & (cc >= r), 1.0, 0.0).astype(BF16)

    @pl.when(i == 0)
    def _():
        s_scr[...] = s0_ref[:, 0]

    s_scr[0] = _gla_block(qf, kf, vf, lf, wgf[...], bgf[...], tri_scr[0], s_scr[0], of_ref, True, with_output)
    s_scr[1] = _gla_block(qb, kb, vb, lb, wgb[...], bgb[...], tri_scr[1], s_scr[1], ob_ref, False, with_output)
    if not with_output:
        @pl.when(i == pl.num_programs(1) - 1)
        def _():
            sout_ref[:, 0] = s_scr[...]


def _gla(proj, wg_f, wg_b, bg_f, bg_b, s0, with_output, col_q, col_k, col_v, col_low):
    t = proj.shape[0]
    cb = _tile(t // GLA_CHUNK, 8, 4, 2, 1)
    rows = cb * GLA_CHUNK
    nb = t // rows
    npairs = wg_f.shape[1] // (2 * GLA_HK)
    pw = 2 * GLA_HK
    vw = 2 * GLA_HV
    fi = lambda p, i: i
    bi = lambda p, i: nb - 1 - i

    def spec(width, col0, blk):
        return pl.BlockSpec((rows, width), lambda p, i: (blk(p, i), col0 // width + p))

    def low_spec(blk):
        return pl.BlockSpec((rows, LANES), lambda p, i: (blk(p, i), col_low // LANES))

    in_specs = [spec(pw, col_q, fi), spec(pw, col_k, fi), spec(vw, col_v, fi), low_spec(fi),
                spec(pw, col_q, bi), spec(pw, col_k, bi), spec(vw, col_v, bi), low_spec(bi),
                pl.BlockSpec((LANES, pw), lambda p, i: (0, p)),
                pl.BlockSpec((LANES, pw), lambda p, i: (0, p)),
                pl.BlockSpec((1, pw), lambda p, i: (0, p)),
                pl.BlockSpec((1, pw), lambda p, i: (0, p)),
                pl.BlockSpec((2, 1, GLA_HV, pw), lambda p, i: (0, p, 0, 0))]
    if with_output:
        out_specs = [pl.BlockSpec((rows, vw), lambda p, i: (i, p)),
                     pl.BlockSpec((rows, vw), lambda p, i: (nb - 1 - i, p))]
        out_shape = [jax.ShapeDtypeStruct((t, npairs * 2 * GLA_HV), F32)] * 2
    else:
        out_specs = pl.BlockSpec((2, 1, GLA_HV, pw), lambda p, i: (0, p, 0, 0))
        out_shape = jax.ShapeDtypeStruct((2, npairs, GLA_HV, pw), F32)
    return pl.pallas_call(
        functools.partial(_gla_kernel, with_output=with_output),
        grid=(npairs, nb),
        in_specs=in_specs,
        out_specs=out_specs,
        out_shape=out_shape,
        scratch_shapes=[pltpu.VMEM((2, GLA_HV, pw), F32), pltpu.VMEM((2, rows, rows), BF16)],
        compiler_params=_params("arbitrary", "arbitrary"),
        name="gla_out" if with_output else "gla_state",
    )(proj, proj, proj, proj, proj, proj, proj, proj, wg_f, wg_b, bg_f, bg_b, s0)


def _mix_kernel(of_ref, ob_ref, r_ref, ca_ref, cb_ref, gout_ref, wdw_ref, bdw_ref, gln_ref, bln_ref,
                o_ref, upad, y_scr):
    tm = of_ref.shape[0]
    dg = of_ref.shape[1]
    dc = ca_ref.shape[1]
    nrow = tm // GRID_W
    lead = 16
    for h in range(dg // GLA_HV):
        sl = slice(h * GLA_HV, (h + 1) * GLA_HV)
        o = of_ref[:, sl] + ob_ref[:, sl]
        o = o * lax.rsqrt(jnp.mean(o * o, axis=-1, keepdims=True) + EPS) * gout_ref[:, sl]
        o_ref[:, sl] = (o * _silu(r_ref[:, sl])).astype(o_ref.dtype)
    for r in range(nrow):
        rs = slice(r * GRID_W, (r + 1) * GRID_W)
        upad[r, 0:lead, :] = jnp.zeros((lead, dc), F32)
        upad[r, lead:lead + GRID_W, :] = ca_ref[rs, :] * _sigmoid(cb_ref[rs, :])
        upad[r, lead + GRID_W:, :] = jnp.zeros((lead, dc), F32)
    cw = 256

    def row_body(r, carry):
        for cc in range(dc // cw):
            cs = slice(cc * cw, (cc + 1) * cw)
            acc = jnp.zeros((GRID_W, cw), F32) + bdw_ref[:, cs]
            for j in range(CONV_WIDTH):
                off = lead - CONV_PAD + j
                acc = acc + wdw_ref[j:j + 1, cs] * upad[r, off:off + GRID_W, cs]
            y_scr[r, :, cs] = acc
        return carry

    lax.fori_loop(0, nrow, row_body, 0)
    for r in range(nrow):
        y = y_scr[r]
        mu = jnp.mean(y, axis=-1, keepdims=True)
        yc = y - mu
        var = jnp.mean(yc * yc, axis=-1, keepdims=True)
        yn = yc * lax.rsqrt(var + EPS) * gln_ref[...] + bln_ref[...]
        o_ref[r * GRID_W:(r + 1) * GRID_W, dg:] = _silu(yn).astype(o_ref.dtype)


def _mix(o_f, o_b, proj, g_out, w_dw, b_dw, g_ln, b_ln, col_r, col_ca, col_cb):
    t, dg = o_f.shape
    tm = _tile(t, 256, 128, GRID_W)
    dc = w_dw.shape[1]
    nrow = tm // GRID_W
    row = lambda width: pl.BlockSpec((1, width), lambda i: (0, 0))
    return pl.pallas_call(
        _mix_kernel,
        grid=(t // tm,),
        in_specs=[pl.BlockSpec((tm, dg), lambda i: (i, 0)),
                  pl.BlockSpec((tm, dg), lambda i: (i, 0)),
                  pl.BlockSpec((tm, dg), lambda i: (i, col_r // dg)),
                  pl.BlockSpec((tm, dc), lambda i: (i, col_ca // dc)),
                  pl.BlockSpec((tm, dc), lambda i: (i, col_cb // dc)),
                  row(dg),
                  pl.BlockSpec((CONV_WIDTH, dc), lambda i: (0, 0)),
                  row(dc), row(dc), row(dc)],
        out_specs=pl.BlockSpec((tm, dg + dc), lambda i: (i, 0)),
        out_shape=jax.ShapeDtypeStruct((t, dg + dc), BF16),
        scratch_shapes=[pltpu.VMEM((nrow, GRID_W + 32, dc), F32),
                        pltpu.VMEM((nrow, GRID_W, dc), F32)],
        compiler_params=_params("arbitrary"),
        name="mix",
    )(o_f, o_b, proj, proj, proj, g_out, w_dw, b_dw, g_ln, b_ln)


def _outproj_kernel(m_ref, w_ref, x_ref, ga_ref, o_ref):
    o_ref[...] = x_ref[...] + ga_ref[...] * jnp.dot(m_ref[...], w_ref[...], preferred_element_type=F32)


def _outproj(mix, w, x, ga):
    t, k = mix.shape
    n = w.shape[1]
    tm = _tile(t, 1024, 512, 256)
    tn = _tile(n, 512, 256, LANES)
    return pl.pallas_call(
        _outproj_kernel,
        grid=(t // tm, n // tn),
        in_specs=[pl.BlockSpec((tm, k), lambda i, j: (i, 0)),
                  pl.BlockSpec((k, tn), lambda i, j: (0, j)),
                  pl.BlockSpec((None, tm, tn), lambda i, j: (0, i, j)),
                  pl.BlockSpec((1, tn), lambda i, j: (0, j))],
        out_specs=pl.BlockSpec((tm, tn), lambda i, j: (i, j)),
        out_shape=jax.ShapeDtypeStruct((t, n), F32),
        compiler_params=_params("arbitrary", "arbitrary"),
        name="out_proj",
    )(mix, w, x, ga)


def _pack_halves(h):
    c = h.shape[1] // 2
    lo = lax.bitcast_convert_type(h[:, :c].astype(BF16).astype(F32), U32)
    hi = lax.bitcast_convert_type(h[:, c:].astype(BF16).astype(F32), U32)
    return (lo >> 16) | (hi & jnp.uint32(0xFFFF0000))


def _unpack_halves(p):
    lo = lax.bitcast_convert_type(p << 16, F32).astype(BF16)
    hi = lax.bitcast_convert_type(p & jnp.uint32(0xFFFF0000), F32).astype(BF16)
    return jnp.concatenate([lo, hi], axis=1)


def _route_kernel(x_ref, g_ref, sh_ref, sc_ref, wr_ref, br_ref, hp_ref, e_ref, w_ref, rk_ref, cnt_ref,
                  carry):
    tm = x_ref.shape[0]
    ne = N_EXPERTS
    gs = ne // N_GROUPS
    neg = -jnp.inf

    @pl.when(pl.program_id(0) == 0)
    def _():
        carry[...] = jnp.zeros_like(carry)

    h = _norm_mod(x_ref[...], g_ref[...], sh_ref[...], sc_ref[...])
    hp_ref[...] = _pack_halves(h)
    logits = lax.dot_general(wr_ref[...], h, NT_DIMS, preferred_element_type=F32,
                             precision=lax.Precision.HIGHEST)
    scores = _sigmoid(logits)
    sel = scores + br_ref[...]
    sel3 = sel.reshape(N_GROUPS, gs, tm)
    sub = lax.broadcasted_iota(I32, (N_GROUPS, gs, tm), 1)
    m1 = jnp.max(sel3, axis=1, keepdims=True)
    i1 = jnp.min(jnp.where(sel3 == m1, sub, gs), axis=1, keepdims=True)
    m2 = jnp.max(jnp.where(sub == i1, neg, sel3), axis=1, keepdims=True)
    gscore = (m1 + m2).reshape(N_GROUPS, tm)
    gid = lax.broadcasted_iota(I32, (N_GROUPS, tm), 0)
    gmask = jnp.zeros((N_GROUPS, tm), F32)
    for _ in range(TOPK_GROUPS):
        mx = jnp.max(gscore, axis=0, keepdims=True)
        idx = jnp.min(jnp.where(gscore == mx, gid, N_GROUPS), axis=0, keepdims=True)
        pick = gid == idx
        gmask = jnp.where(pick, 1.0, gmask)
        gscore = jnp.where(pick, neg, gscore)
    emask = jnp.broadcast_to(gmask.reshape(N_GROUPS, 1, tm), (N_GROUPS, gs, tm)).reshape(ne, tm)
    cand = jnp.where(emask > 0.0, sel, neg)
    eid = lax.broadcasted_iota(I32, (ne, tm), 0)
    chosen = jnp.zeros((ne, tm), F32)
    idxs, scs = [], []
    for _ in range(TOP_K):
        mx = jnp.max(cand, axis=0, keepdims=True)
        idx = jnp.min(jnp.where(cand == mx, eid, ne), axis=0, keepdims=True)
        pick = eid == idx
        idxs.append(idx)
        scs.append(jnp.sum(jnp.where(pick, scores, 0.0), axis=0, keepdims=True))
        chosen = jnp.where(pick, 1.0, chosen)
        cand = jnp.where(pick, neg, cand)
    ssum = scs[0]
    for s in scs[1:]:
        ssum = ssum + s
    before = (lax.broadcasted_iota(I32, (tm, tm), 0) < lax.broadcasted_iota(I32, (tm, tm), 1))
    prior = jnp.dot(chosen.astype(BF16), before.astype(BF16), preferred_element_type=F32) + carry[...]
    for k in range(TOP_K):
        e_ref[k:k + 1, :] = idxs[k]
        w_ref[k:k + 1, :] = scs[k] / ssum * ROUTED_SCALE
        rk = jnp.sum(jnp.where(eid == idxs[k], prior, 0.0), axis=0, keepdims=True)
        rk_ref[k:k + 1, :] = rk.astype(I32)
    carry[...] = carry[...] + jnp.sum(chosen, axis=1, keepdims=True)
    cnt_ref[...] = jnp.broadcast_to(carry[...], cnt_ref.shape).astype(I32)


def _route(x, g, sh, sc, w_router_t, b_router):
    t, d = x.shape
    tm = _tile(t, 256, LANES)
    vec = pl.BlockSpec((1, d), lambda i: (0, 0))
    tok = lambda dt: jax.ShapeDtypeStruct((TOP_K, t), dt)
    tok_spec = pl.BlockSpec((TOP_K, tm), lambda i: (0, i))
    return pl.pallas_call(
        _route_kernel,
        grid=(t // tm,),
        in_specs=[pl.BlockSpec((tm, d), lambda i: (i, 0)), vec, vec, vec,
                  pl.BlockSpec((N_EXPERTS, d), lambda i: (0, 0)),
                  pl.BlockSpec((N_EXPERTS, 1), lambda i: (0, 0))],
        out_specs=[pl.BlockSpec((tm, d // 2), lambda i: (i, 0)), tok_spec, tok_spec, tok_spec,
                   pl.BlockSpec((N_EXPERTS, LANES), lambda i: (0, 0))],
        out_shape=[jax.ShapeDtypeStruct((t, d // 2), U32), tok(I32), tok(F32), tok(I32),
                   jax.ShapeDtypeStruct((N_EXPERTS, LANES), I32)],
        scratch_shapes=[pltpu.VMEM((N_EXPERTS, 1), F32)],
        compiler_params=_params("arbitrary"),
        name="route",
    )(x, g, sh, sc, w_router_t, b_router)


WUNITS = 4


def _experts_kernel(blk_run, run_e, meta, slot_tok, hp_hbm, wg_hbm, wu_hbm, wd_hbm, o_ref,
                    xa, xb, sem, sg, su, sd, wsem, wgb, wub, wdb, done_ref):
    step = pl.program_id(0)
    nblocks = 2 * pl.num_programs(0)
    nused = meta[0]
    nunits = meta[1] * WUNITS
    ug = sg.shape[1]
    ud = sd.shape[1]

    def unit_copies(u):
        e = run_e[u // WUNITS]
        k = u % WUNITS
        s = u % 2
        return (pltpu.make_async_copy(wg_hbm.at[e, pl.ds(k * ug, ug)], sg.at[s], wsem.at[s]),
                pltpu.make_async_copy(wu_hbm.at[e, pl.ds(k * ug, ug)], su.at[s], wsem.at[s]),
                pltpu.make_async_copy(wd_hbm.at[e, pl.ds(k * ud, ud)], sd.at[s], wsem.at[s]))

    def start_unit(u):
        for cp in unit_copies(u):
            cp.start()

    def process_unit(u):
        for cp in unit_copies(u):
            cp.wait()

        @pl.when(u + 1 < nunits)
        def _():
            start_unit(u + 1)

        r = (u // WUNITS) % 2
        k = u % WUNITS
        s = u % 2
        wgb[r, pl.ds(pl.multiple_of(k * ug, ug), ug), :] = sg[s].astype(BF16)
        wub[r, pl.ds(pl.multiple_of(k * ug, ug), ug), :] = su[s].astype(BF16)
        wdb[r, pl.ds(pl.multiple_of(k * ud, ud), ud), :] = sd[s].astype(BF16)

    def start_gather(blk, buf, s):
        for r in range(MOE_BLOCK):
            tok = slot_tok[blk * MOE_BLOCK + r]
            pltpu.make_async_copy(hp_hbm.at[pl.ds(tok, 1)], buf.at[pl.ds(r, 1)],
                                  sem.at[s]).start(priority=r % 2)

    def wait_gather(buf, s):
        pltpu.make_async_copy(hp_hbm.at[pl.ds(0, MOE_BLOCK)], buf, sem.at[s]).wait()

    @pl.when(step == 0)
    def _():
        done_ref[0] = 0
        start_unit(0)
        start_gather(0, xa, 0)

    def do_block(blk, rows, buf, s, nbuf, ns):
        live = blk < nused
        next_live = (blk + 1 < nblocks) & (blk + 1 < nused)
        run = blk_run[blk]

        @pl.when(live)
        def _():
            done = done_ref[0]
            need = jnp.maximum(WUNITS * (run + 1) - done, 0)
            ahead = (done + need < jnp.minimum(nunits, WUNITS * (run + 2))).astype(I32)

            def body(n, carry):
                process_unit(done + n)
                return carry
            lax.fori_loop(0, need + ahead, body, 0)
            done_ref[0] = done + need + ahead

        def compute(gather_next):
            r = run % 2
            wait_gather(buf, s)
            if gather_next:
                start_gather(blk + 1, nbuf, ns)
            x = _unpack_halves(buf[...])
            hg = jnp.dot(x, wgb[r], preferred_element_type=F32)
            hu = jnp.dot(x, wub[r], preferred_element_type=F32)
            hb = (_silu(hg) * hu).astype(BF16)
            o_ref[rows] = jnp.dot(hb, wdb[r], preferred_element_type=F32)

        pl.when(live & next_live)(functools.partial(compute, True))
        pl.when(live & jnp.logical_not(next_live))(functools.partial(compute, False))

        @pl.when(jnp.logical_not(live))
        def _():
            o_ref[rows] = jnp.zeros((MOE_BLOCK, o_ref.shape[1]), o_ref.dtype)

    do_block(2 * step, slice(0, MOE_BLOCK), xa, 0, xb, 1)
    do_block(2 * step + 1, slice(MOE_BLOCK, 2 * MOE_BLOCK), xb, 1, xa, 0)


def _experts(blk_run, run_e, meta, slot_tok, hp, wg, wu, wd):
    nb = blk_run.shape[0]
    assert nb % 2 == 0
    _, d, de = wg.shape
    ug, ud = d // WUNITS, de // WUNITS
    hbm = pl.BlockSpec(memory_space=pl.ANY)
    grid_spec = pltpu.PrefetchScalarGridSpec(
        num_scalar_prefetch=4,
        grid=(nb // 2,),
        in_specs=[hbm, hbm, hbm, hbm],
        out_specs=pl.BlockSpec((2 * MOE_BLOCK, d), lambda b, *_: (b, 0)),
        scratch_shapes=[pltpu.VMEM((MOE_BLOCK, d // 2), U32), pltpu.VMEM((MOE_BLOCK, d // 2), U32),
                        pltpu.SemaphoreType.DMA((2,)),
                        pltpu.VMEM((2, ug, de), F32), pltpu.VMEM((2, ug, de), F32),
                        pltpu.VMEM((2, ud, d), F32), pltpu.SemaphoreType.DMA((2,)),
                        pltpu.VMEM((2, d, de), BF16), pltpu.VMEM((2, d, de), BF16),
                        pltpu.VMEM((2, de, d), BF16), pltpu.SMEM((1,), I32)],
    )
    return pl.pallas_call(
        _experts_kernel,
        grid_spec=grid_spec,
        out_shape=jax.ShapeDtypeStruct((nb * MOE_BLOCK, d), F32),
        compiler_params=_params("arbitrary"),
        name="experts",
    )(blk_run, run_e, meta, slot_tok, hp, wg, wu, wd)


def _shared_kernel(hp_ref, wg_ref, wu_ref, wd_ref, x_ref, ga_ref, o_ref):
    x = _unpack_halves(hp_ref[...])
    hg = jnp.dot(x, wg_ref[...], preferred_element_type=F32)
    hu = jnp.dot(x, wu_ref[...], preferred_element_type=F32)
    hb = (_silu(hg) * hu).astype(BF16)
    o_ref[...] = x_ref[...] + ga_ref[...] * jnp.dot(hb, wd_ref[...], preferred_element_type=F32)


def _shared(hp, wg, wu, wd, x, ga):
    t, d = x.shape
    tm = _tile(t, 256, LANES)
    ds_ = wg.shape[1]
    return pl.pallas_call(
        _shared_kernel,
        grid=(t // tm,),
        in_specs=[pl.BlockSpec((tm, d // 2), lambda i: (i, 0)),
                  pl.BlockSpec((d, ds_), lambda i: (0, 0)),
                  pl.BlockSpec((d, ds_), lambda i: (0, 0)),
                  pl.BlockSpec((ds_, d), lambda i: (0, 0)),
                  pl.BlockSpec((tm, d), lambda i: (i, 0)),
                  pl.BlockSpec((1, d), lambda i: (0, 0))],
        out_specs=pl.BlockSpec((tm, d), lambda i: (i, 0)),
        out_shape=jax.ShapeDtypeStruct((t, d), F32),
        compiler_params=_params("arbitrary"),
        name="shared",
    )(hp, wg, wu, wd, x, ga)


def _combine_kernel(dest, ys_hbm, w_ref, base_ref, ga_ref, gf_ref, o_ref, buf_a, buf_b, sem):
    i = pl.program_id(0)
    n = pl.num_programs(0)
    tm = buf_a.shape[1]

    def start_gather(tile, buf, s):
        for j in range(tm * TOP_K):
            src = dest[tile * (tm * TOP_K) + j]
            pltpu.make_async_copy(ys_hbm.at[pl.ds(src, 1)], buf.at[j % TOP_K, pl.ds(j // TOP_K, 1)],
                                  sem.at[s]).start(priority=j % 2)

    def wait_gather(buf, s):
        for k in range(TOP_K):
            pltpu.make_async_copy(ys_hbm.at[pl.ds(0, tm)], buf.at[k], sem.at[s]).wait()

    def finish(buf, rows):
        y = w_ref[rows, 0:1] * buf[0]
        for k in range(1, TOP_K):
            y = y + w_ref[rows, k:k + 1] * buf[k]
        x = base_ref[rows] + ga_ref[...] * y
        o_ref[rows] = x * lax.rsqrt(jnp.mean(x * x, axis=-1, keepdims=True) + EPS) * gf_ref[...]

    def step(gather_next):
        wait_gather(buf_a, 0)
        start_gather(2 * i + 1, buf_b, 1)
        finish(buf_a, slice(0, tm))
        wait_gather(buf_b, 1)
        if gather_next:
            start_gather(2 * i + 2, buf_a, 0)
        finish(buf_b, slice(tm, 2 * tm))

    @pl.when(i == 0)
    def _():
        start_gather(0, buf_a, 0)

    pl.when(i + 1 < n)(functools.partial(step, True))
    pl.when(i + 1 == n)(functools.partial(step, False))


def _combine(dest, ys, wts, base, ga, g_final):
    t, d = base.shape
    tm = 64
    assert t % (2 * tm) == 0
    grid_spec = pltpu.PrefetchScalarGridSpec(
        num_scalar_prefetch=1,
        grid=(t // (2 * tm),),
        in_specs=[pl.BlockSpec(memory_space=pl.ANY),
                  pl.BlockSpec((2 * tm, TOP_K), lambda i, s: (i, 0)),
                  pl.BlockSpec((2 * tm, d), lambda i, s: (i, 0)),
                  pl.BlockSpec((1, d), lambda i, s: (0, 0)),
                  pl.BlockSpec((1, d), lambda i, s: (0, 0))],
        out_specs=pl.BlockSpec((None, 2 * tm, d), lambda i, s: (0, i, 0)),
        scratch_shapes=[pltpu.VMEM((TOP_K, tm, d), F32), pltpu.VMEM((TOP_K, tm, d), F32),
                        pltpu.SemaphoreType.DMA((2,))],
    )
    return pl.pallas_call(
        _combine_kernel,
        grid_spec=grid_spec,
        out_shape=jax.ShapeDtypeStruct((1, t, d), F32),
        compiler_params=_params("arbitrary"),
        name="combine",
    )(dest, ys, wts, base, ga, g_final)


def kernel(x, c, ctx, c_ctx, w_ada, b_ada, g_norm_mix, g_norm_ffn, w_in, w_gate_up, b_gate_up,
           g_gla_out, w_dw, b_dw, g_conv_ln, b_conv_ln, w_out, w_router, b_router,
           w_e_gate, w_e_up, w_e_down, w_s_gate, w_s_up, w_s_down, g_final):
    assert x.shape[0] == 1 and w_ada.shape[0] == 1, "single batch element, single layer"
    t, d = x.shape[1], x.shape[2]
    dk = w_gate_up.shape[3]
    heads = dk // GLA_HK
    dg = heads * GLA_HV
    dc = w_dw.shape[2]

    col_q, col_k, col_v = 0, dk, 2 * dk
    col_r = col_v + dg
    col_ca = col_r + dg
    col_cb = col_ca + dc
    col_low = col_cb + dc
    n_gate = 2 * GATE_RANK
    wi = w_in[0]
    w_a = wi[:, :col_ca].astype(BF16)
    n_pad = -(2 * dc + n_gate) % _tile(col_ca, 1024, 256)
    w_b = jnp.concatenate([wi[:, col_ca + n_gate:].astype(BF16), wi[:, col_ca:col_ca + n_gate].astype(BF16),
                           jnp.zeros((d, n_pad), BF16)], axis=1)

    mod = _ada(jnp.stack([c[0], c_ctx]), w_ada[0], b_ada[0])
    sh_m, sc_m, ga_m, sh_f, sc_f, ga_f = [mod[0:1, k * d:(k + 1) * d] for k in range(6)]
    csh_m, csc_m = mod[1:2, 0:d], mod[1:2, d:2 * d]

    wg_f = jnp.zeros((LANES, dk), F32).at[:GATE_RANK].set(w_gate_up[0, 0]).astype(BF16)
    wg_b = jnp.zeros((LANES, dk), F32).at[GATE_RANK:n_gate].set(w_gate_up[0, 1]).astype(BF16)
    bg_f = b_gate_up[0, 0:1]
    bg_b = b_gate_up[0, 1:2]
    cols = dict(col_q=col_q, col_k=col_k, col_v=col_v, col_low=col_low)

    gn_mix = g_norm_mix[0:1]
    proj_c = _proj(ctx, gn_mix, csh_m, csc_m, w_a, w_b)
    s0 = jnp.zeros((2, heads // 2, GLA_HV, 2 * GLA_HK), F32)
    s_ctx = _gla(proj_c, wg_f, wg_b, bg_f, bg_b, s0, with_output=False, **cols)

    proj_l = _proj(x, gn_mix, sh_m, sc_m, w_a, w_b)
    o_f, o_b = _gla(proj_l, wg_f, wg_b, bg_f, bg_b, s_ctx, with_output=True, **cols)
    mix = _mix(o_f, o_b, proj_l, g_gla_out[0:1], w_dw[0], b_dw[0:1], g_conv_ln[0:1], b_conv_ln[0:1],
               col_r, col_ca, col_cb)
    x1 = _outproj(mix, w_out[0].astype(BF16), x, ga_m)

    hp, eidx, wts, rank, cnt = _route(x1, g_norm_ffn[0:1], sh_f, sc_f, w_router[0].T,
                                      b_router[0].reshape(N_EXPERTS, 1))
    nk = t * TOP_K
    nb = (nk + MOE_BLOCK - 1) // MOE_BLOCK + N_EXPERTS
    counts = cnt[:, 0]
    pcounts = (counts + MOE_BLOCK - 1) // MOE_BLOCK * MOE_BLOCK
    pend = jnp.cumsum(pcounts)
    pstart = pend - pcounts
    experts = jnp.arange(N_EXPERTS, dtype=I32)
    pstart_tok = jnp.sum(jnp.where(eidx[:, :, None] == experts, pstart, 0), axis=-1)
    dest = (pstart_tok + rank).T.reshape(-1).astype(I32)
    tok_ids = jnp.repeat(jnp.arange(t, dtype=I32), TOP_K)
    slot_tok = jnp.zeros((nb * MOE_BLOCK,), I32).at[dest].set(tok_ids, unique_indices=True)
    blk_start = jnp.arange(nb, dtype=I32) * MOE_BLOCK
    blk_e = jnp.minimum(jnp.sum(pend[None, :] <= blk_start[:, None], axis=1), N_EXPERTS - 1).astype(I32)
    owns = counts > 0
    run_of_e = jnp.cumsum(owns.astype(I32)) - 1
    run_e = jnp.sum(jnp.where(owns[None, :] & (run_of_e[None, :] == experts[:, None]), experts[None, :], 0),
                    axis=1).astype(I32)
    blk_run = jnp.sum(jnp.where(blk_e[:, None] == experts[None, :], run_of_e[None, :], 0), axis=1).astype(I32)
    meta = jnp.stack([pend[-1] // MOE_BLOCK, jnp.sum(owns.astype(I32))]).astype(I32)

    ys = _experts(blk_run, run_e, meta, slot_tok, hp, w_e_gate[0], w_e_up[0], w_e_down[0])
    base = _shared(hp, w_s_gate[0].astype(BF16), w_s_up[0].astype(BF16), w_s_down[0].astype(BF16), x1, ga_f)
    return _combine(dest, ys, wts.T, base, ga_f, g_final.reshape(1, d))
```

```python
import functools

import jax
import jax.numpy as jnp
from jax import lax
from jax.experimental import pallas as pl
from jax.experimental.pallas import tpu as pltpu

F32 = jnp.float32
BF16 = jnp.bfloat16
I32 = jnp.int32
U32 = jnp.uint32

EPS = 1e-6
LANES = 128
GLA_HK = 64
GLA_HV = 128
GLA_CHUNK = 64
GATE_RANK = 16
GATE_TAU = 16.0
CONV_WIDTH = 31
CONV_PAD = (CONV_WIDTH - 1) // 2
GRID_W = 64
N_EXPERTS = 64
N_GROUPS = 8
TOPK_GROUPS = 4
TOP_K = 8
ROUTED_SCALE = 2.5
MOE_BLOCK = 128
VMEM_LIMIT = 56 * 1024 * 1024

NT_DIMS = (((1,), (1,)), ((), ()))
TN_DIMS = (((0,), (0,)), ((), ()))


def _params(*sem):
    return pltpu.CompilerParams(dimension_semantics=sem, vmem_limit_bytes=VMEM_LIMIT)


def _tile(n, *preferred):
    for p in preferred:
        if n % p == 0:
            return p
    raise ValueError(f"no tile for {n} among {preferred}")


def _sigmoid(x):
    return 1.0 / (1.0 + jnp.exp(-x))


def _silu(x):
    return x * _sigmoid(x)


def _ada_kernel(cb_ref, w_ref, b_ref, o_ref, s_scr):
    @pl.when(pl.program_id(0) == 0)
    def _():
        s_scr[...] = _silu(cb_ref[...])

    d, tn = w_ref.shape
    for jt in range(tn // LANES):
        sl = slice(jt * LANES, (jt + 1) * LANES)
        w = w_ref[:, sl]
        for m in range(2):
            p = (w * s_scr[m]).reshape(d // 8, 8, LANES).sum(axis=0)
            o_ref[m:m + 1, sl] = p.sum(axis=0, keepdims=True) + b_ref[:, sl]


def _ada(c2, w_ada, b_ada):
    d, n = w_ada.shape
    tn = _tile(n, 1024, 512, LANES)
    cb = jnp.broadcast_to(c2[:, :, None], (2, d, LANES))
    return pl.pallas_call(
        _ada_kernel,
        grid=(n // tn,),
        in_specs=[pl.BlockSpec((2, d, LANES), lambda j: (0, 0, 0)),
                  pl.BlockSpec((d, tn), lambda j: (0, j)),
                  pl.BlockSpec((1, tn), lambda j: (0, j))],
        out_specs=pl.BlockSpec((2, tn), lambda j: (0, j)),
        out_shape=jax.ShapeDtypeStruct((2, n), F32),
        scratch_shapes=[pltpu.VMEM((2, d, LANES), F32)],
        compiler_params=_params("arbitrary"),
        name="ada",
    )(cb, w_ada, b_ada.reshape(1, n))


def _norm_mod(x, g, sh, sc):
    ms = jnp.mean(x * x, axis=-1, keepdims=True)
    y = x * lax.rsqrt(ms + EPS) * g
    return y * (1.0 + sc) + sh


def _proj_kernel(x_ref, g_ref, sh_ref, sc_ref, wa_ref, wb_ref, o_ref, h_scr, *, na):
    j = pl.program_id(1)

    @pl.when(j == 0)
    def _():
        rows = min(256, h_scr.shape[0])
        for r in range(0, h_scr.shape[0], rows):
            h_scr[r:r + rows] = _norm_mod(x_ref[r:r + rows], g_ref[...], sh_ref[...], sc_ref[...]).astype(BF16)

    @pl.when(j < na)
    def _():
        o_ref[...] = jnp.dot(h_scr[...], wa_ref[...], preferred_element_type=F32)

    @pl.when(j >= na)
    def _():
        o_ref[...] = jnp.dot(h_scr[...], wb_ref[...], preferred_element_type=F32)


def _proj(x, g, sh, sc, wa, wb):
    _, t, d = x.shape
    tm = _tile(t, 1024, 512, 256, 128)
    tn = _tile(wa.shape[1], 512, 256)
    na, nb = wa.shape[1] // tn, wb.shape[1] // tn
    assert wb.shape[1] % tn == 0
    vec = pl.BlockSpec((1, d), lambda i, j: (0, 0))
    return pl.pallas_call(
        functools.partial(_proj_kernel, na=na),
        grid=(t // tm, na + nb),
        in_specs=[pl.BlockSpec((None, tm, d), lambda i, j: (0, i, 0), pipeline_mode=pl.Buffered(1)),
                  vec, vec, vec,
                  pl.BlockSpec((d, tn), lambda i, j: (0, jnp.minimum(j, na - 1))),
                  pl.BlockSpec((d, tn), lambda i, j: (0, jnp.maximum(j - na, 0)))],
        out_specs=pl.BlockSpec((tm, tn), lambda i, j: (i, j)),
        out_shape=jax.ShapeDtypeStruct((t, (na + nb) * tn), F32),
        scratch_shapes=[pltpu.VMEM((tm, d), BF16)],
        compiler_params=_params("arbitrary", "arbitrary"),
        name="proj",
    )(x, g, sh, sc, wa, wb)


def _gla_block(q_ref, k_ref, v_ref, low_ref, wg, bg, tri, s_t, o_ref, fwd, with_output):
    c = GLA_CHUNK
    nch = q_ref.shape[0] // c
    z = jnp.dot(low_ref[...].astype(BF16), wg, preferred_element_type=F32) + bg
    g = (jnp.minimum(z, 0.0) - jnp.log(1.0 + jnp.exp(-jnp.abs(z)))) * (1.0 / GATE_TAU)
    g1 = g.astype(BF16)
    r1 = g - g1.astype(F32)
    g2 = r1.astype(BF16)
    g3 = (r1 - g2.astype(F32)).astype(BF16)
    b3 = jnp.dot(tri, jnp.concatenate([g1, g2, g3], axis=1), preferred_element_type=F32)
    b = b3[:, :LANES] + b3[:, LANES:2 * LANES] + b3[:, 2 * LANES:]

    def chunk_rows(n):
        return slice(n * c, (n + 1) * c)

    def per_chunk(row):
        return jnp.concatenate([jnp.broadcast_to(b[n * c + row:n * c + row + 1], (c, LANES))
                                for n in range(nch)], axis=0)

    tot = per_chunk(c - 1 if fwd else 0)
    lane0 = lax.broadcasted_iota(I32, (c, LANES), 1) < GLA_HK

    def stack(a, n):
        an = a[chunk_rows(n)]
        return jnp.concatenate([jnp.where(lane0, an, 0.0), jnp.where(lane0, 0.0, an)], axis=0).astype(BF16)

    k = k_ref[...]
    ks = k * jnp.exp(tot - b)
    vs = [jnp.concatenate([v_ref[chunk_rows(n), :GLA_HV], v_ref[chunk_rows(n), GLA_HV:]],
                          axis=0).astype(BF16) for n in range(nch)]
    upd = [lax.dot_general(vs[n], stack(ks, n), TN_DIMS, preferred_element_type=F32) for n in range(nch)]
    decay = jnp.exp(tot)
    order = range(nch) if fwd else range(nch - 1, -1, -1)
    s_in = [None] * nch
    for n in order:
        s_in[n] = s_t
        s_t = s_t * decay[n * c:n * c + 1] + upd[n]
    if not with_output:
        return s_t
    mid = per_chunk(c // 2)
    qs = q_ref[...] * (GLA_HK ** -0.5)
    qa = qs * jnp.exp(b - mid)
    ka = k * jnp.exp(mid - b)
    qi = qs * jnp.exp(b)
    r2 = lax.broadcasted_iota(I32, (2 * c, 2 * c), 0)
    c2 = lax.broadcasted_iota(I32, (2 * c, 2 * c), 1)
    keep = ((r2 >= c) == (c2 >= c)) & ((c2 <= r2) if fwd else (c2 >= r2))
    for n in range(nch):
        att = lax.dot_general(stack(qa, n), stack(ka, n), NT_DIMS, preferred_element_type=F32)
        att = jnp.where(keep, att, 0.0).astype(BF16)
        o = jnp.dot(att, vs[n], preferred_element_type=F32)
        o = o + lax.dot_general(stack(qi, n), s_in[n].astype(BF16), NT_DIMS, preferred_element_type=F32)
        o_ref[chunk_rows(n)] = jnp.concatenate([o[:c], o[c:]], axis=1)
    return s_t


def _gla_kernel(*refs, with_output):
    (qf, kf, vf, lf, qb, kb, vb, lb, wgf, wgb, bgf, bgb, s0_ref) = refs[:13]
    if with_output:
        of_ref, ob_ref, s_scr, tri_scr = refs[13:]
    else:
        sout_ref, s_scr, tri_scr = refs[13:]
        of_ref = ob_ref = None
    i = pl.program_id(1)
    c = GLA_CHUNK
    rows = qf.shape[0]

    @pl.when((pl.program_id(0) == 0) & (i == 0))
    def _():
        r = lax.broadcasted_iota(I32, (rows, rows), 0)
        cc = lax.broadcasted_iota(I32, (rows, rows), 1)
        shift = c.bit_length() - 1
        same = jnp.right_shift(r, shift) == jnp.right_shift(cc, shift)
        tri_scr[0] = jnp.where(same & (cc <= r), 1.0, 0.0).astype(BF16)
        tri_scr[1] = jnp.where(same & (cc >= r), 1.0, 0.0).astype(BF16)

    @pl.when(i == 0)
    def _():
        s_scr[...] = s0_ref[:, 0]

    s_scr[0] = _gla_block(qf, kf, vf, lf, wgf[...], bgf[...], tri_scr[0], s_scr[0], of_ref, True, with_output)
    s_scr[1] = _gla_block(qb, kb, vb, lb, wgb[...], bgb[...], tri_scr[1], s_scr[1], ob_ref, False, with_output)
    if not with_output:
        @pl.when(i == pl.num_programs(1) - 1)
        def _():
            sout_ref[:, 0] = s_scr[...]


def _gla(proj, wg_f, wg_b, bg_f, bg_b, s0, with_output, col_q, col_k, col_v, col_low):
    t = proj.shape[0]
    cb = _tile(t // GLA_CHUNK, 8, 4, 2, 1)
    rows = cb * GLA_CHUNK
    nb = t // rows
    npairs = wg_f.shape[1] // (2 * GLA_HK)
    pw = 2 * GLA_HK
    vw = 2 * GLA_HV
    fi = lambda p, i: i
    bi = lambda p, i: nb - 1 - i

    def spec(width, col0, blk):
        return pl.BlockSpec((rows, width), lambda p, i: (blk(p, i), col0 // width + p))

    def low_spec(blk):
        return pl.BlockSpec((rows, LANES), lambda p, i: (blk(p, i), col_low // LANES))

    in_specs = [spec(pw, col_q, fi), spec(pw, col_k, fi), spec(vw, col_v, fi), low_spec(fi),
                spec(pw, col_q, bi), spec(pw, col_k, bi), spec(vw, col_v, bi), low_spec(bi),
                pl.BlockSpec((LANES, pw), lambda p, i: (0, p)),
                pl.BlockSpec((LANES, pw), lambda p, i: (0, p)),
                pl.BlockSpec((1, pw), lambda p, i: (0, p)),
                pl.BlockSpec((1, pw), lambda p, i: (0, p)),
                pl.BlockSpec((2, 1, GLA_HV, pw), lambda p, i: (0, p, 0, 0))]
    if with_output:
        out_specs = [pl.BlockSpec((rows, vw), lambda p, i: (i, p)),
                     pl.BlockSpec((rows, vw), lambda p, i: (nb - 1 - i, p))]
        out_shape = [jax.ShapeDtypeStruct((t, npairs * 2 * GLA_HV), F32)] * 2
    else:
        out_specs = pl.BlockSpec((2, 1, GLA_HV, pw), lambda p, i: (0, p, 0, 0))
        out_shape = jax.ShapeDtypeStruct((2, npairs, GLA_HV, pw), F32)
    return pl.pallas_call(
        functools.partial(_gla_kernel, with_output=with_output),
        grid=(npairs, nb),
        in_specs=in_specs,
        out_specs=out_specs,
        out_shape=out_shape,
        scratch_shapes=[pltpu.VMEM((2, GLA_HV, pw), F32), pltpu.VMEM((2, rows, rows), BF16)],
        compiler_params=_params("arbitrary", "arbitrary"),
        name="gla_out" if with_output else "gla_state",
    )(proj, proj, proj, proj, proj, proj, proj, proj, wg_f, wg_b, bg_f, bg_b, s0)


def _mix_kernel(of_ref, ob_ref, r_ref, ca_ref, cb_ref, gout_ref, wdw_ref, bdw_ref, gln_ref, bln_ref,
                o_ref, upad, y_scr):
    tm = of_ref.shape[0]
    dg = of_ref.shape[1]
    dc = ca_ref.shape[1]
    nrow = tm // GRID_W
    lead = 16
    for h in range(dg // GLA_HV):
        sl = slice(h * GLA_HV, (h + 1) * GLA_HV)
        o = of_ref[:, sl] + ob_ref[:, sl]
        o = o * lax.rsqrt(jnp.mean(o * o, axis=-1, keepdims=True) + EPS) * gout_ref[:, sl]
        o_ref[:, sl] = (o * _silu(r_ref[:, sl])).astype(o_ref.dtype)
    for r in range(nrow):
        rs = slice(r * GRID_W, (r + 1) * GRID_W)
        upad[r, 0:lead, :] = jnp.zeros((lead, dc), F32)
        upad[r, lead:lead + GRID_W, :] = ca_ref[rs, :] * _sigmoid(cb_ref[rs, :])
        upad[r, lead + GRID_W:, :] = jnp.zeros((lead, dc), F32)
    cw = 256

    def row_body(r, carry):
        for cc in range(dc // cw):
            cs = slice(cc * cw, (cc + 1) * cw)
            acc = jnp.zeros((GRID_W, cw), F32) + bdw_ref[:, cs]
            for j in range(CONV_WIDTH):
                off = lead - CONV_PAD + j
                acc = acc + wdw_ref[j:j + 1, cs] * upad[r, off:off + GRID_W, cs]
            y_scr[r, :, cs] = acc
        return carry

    lax.fori_loop(0, nrow, row_body, 0)
    for r in range(nrow):
        y = y_scr[r]
        mu = jnp.mean(y, axis=-1, keepdims=True)
        yc = y - mu
        var = jnp.mean(yc * yc, axis=-1, keepdims=True)
        yn = yc * lax.rsqrt(var + EPS) * gln_ref[...] + bln_ref[...]
        o_ref[r * GRID_W:(r + 1) * GRID_W, dg:] = _silu(yn).astype(o_ref.dtype)


def _mix(o_f, o_b, proj, g_out, w_dw, b_dw, g_ln, b_ln, col_r, col_ca, col_cb):
    t, dg = o_f.shape
    tm = _tile(t, 256, 128, GRID_W)
    dc = w_dw.shape[1]
    nrow = tm // GRID_W
    row = lambda width: pl.BlockSpec((1, width), lambda i: (0, 0))
    return pl.pallas_call(
        _mix_kernel,
        grid=(t // tm,),
        in_specs=[pl.BlockSpec((tm, dg), lambda i: (i, 0)),
                  pl.BlockSpec((tm, dg), lambda i: (i, 0)),
                  pl.BlockSpec((tm, dg), lambda i: (i, col_r // dg)),
                  pl.BlockSpec((tm, dc), lambda i: (i, col_ca // dc)),
                  pl.BlockSpec((tm, dc), lambda i: (i, col_cb // dc)),
                  row(dg),
                  pl.BlockSpec((CONV_WIDTH, dc), lambda i: (0, 0)),
                  row(dc), row(dc), row(dc)],
        out_specs=pl.BlockSpec((tm, dg + dc), lambda i: (i, 0)),
        out_shape=jax.ShapeDtypeStruct((t, dg + dc), BF16),
        scratch_shapes=[pltpu.VMEM((nrow, GRID_W + 32, dc), F32),
                        pltpu.VMEM((nrow, GRID_W, dc), F32)],
        compiler_params=_params("arbitrary"),
        name="mix",
    )(o_f, o_b, proj, proj, proj, g_out, w_dw, b_dw, g_ln, b_ln)


def _outproj_kernel(m_ref, w_ref, x_ref, ga_ref, o_ref):
    o_ref[...] = x_ref[...] + ga_ref[...] * jnp.dot(m_ref[...], w_ref[...], preferred_element_type=F32)


def _outproj(mix, w, x, ga):
    t, k = mix.shape
    n = w.shape[1]
    tm = _tile(t, 1024, 512, 256)
    tn = _tile(n, 512, 256, LANES)
    return pl.pallas_call(
        _outproj_kernel,
        grid=(t // tm, n // tn),
        in_specs=[pl.BlockSpec((tm, k), lambda i, j: (i, 0)),
                  pl.BlockSpec((k, tn), lambda i, j: (0, j)),
                  pl.BlockSpec((None, tm, tn), lambda i, j: (0, i, j)),
                  pl.BlockSpec((1, tn), lambda i, j: (0, j))],
        out_specs=pl.BlockSpec((tm, tn), lambda i, j: (i, j)),
        out_shape=jax.ShapeDtypeStruct((t, n), F32),
        compiler_params=_params("arbitrary", "arbitrary"),
        name="out_proj",
    )(mix, w, x, ga)


def _pack_halves(h):
    c = h.shape[1] // 2
    lo = lax.bitcast_convert_type(h[:, :c].astype(BF16).astype(F32), U32)
    hi = lax.bitcast_convert_type(h[:, c:].astype(BF16).astype(F32), U32)
    return (lo >> 16) | (hi & jnp.uint32(0xFFFF0000))


def _unpack_halves(p):
    lo = lax.bitcast_convert_type(p << 16, F32).astype(BF16)
    hi = lax.bitcast_convert_type(p & jnp.uint32(0xFFFF0000), F32).astype(BF16)
    return jnp.concatenate([lo, hi], axis=1)


def _route_kernel(x_ref, g_ref, sh_ref, sc_ref, wr_ref, br_ref, hp_ref, e_ref, w_ref, rk_ref, cnt_ref,
                  carry):
    tm = x_ref.shape[0]
    ne = N_EXPERTS
    gs = ne // N_GROUPS
    neg = -jnp.inf

    @pl.when(pl.program_id(0) == 0)
    def _():
        carry[...] = jnp.zeros_like(carry)

    h = _norm_mod(x_ref[...], g_ref[...], sh_ref[...], sc_ref[...])
    hp_ref[...] = _pack_halves(h)
    logits = lax.dot_general(wr_ref[...], h, NT_DIMS, preferred_element_type=F32,
                             precision=lax.Precision.HIGHEST)
    scores = _sigmoid(logits)
    sel = scores + br_ref[...]
    sel3 = sel.reshape(N_GROUPS, gs, tm)
    sub = lax.broadcasted_iota(I32, (N_GROUPS, gs, tm), 1)
    m1 = jnp.max(sel3, axis=1, keepdims=True)
    i1 = jnp.min(jnp.where(sel3 == m1, sub, gs), axis=1, keepdims=True)
    m2 = jnp.max(jnp.where(sub == i1, neg, sel3), axis=1, keepdims=True)
    gscore = (m1 + m2).reshape(N_GROUPS, tm)
    gid = lax.broadcasted_iota(I32, (N_GROUPS, tm), 0)
    gmask = jnp.zeros((N_GROUPS, tm), F32)
    for _ in range(TOPK_GROUPS):
        mx = jnp.max(gscore, axis=0, keepdims=True)
        idx = jnp.min(jnp.where(gscore == mx, gid, N_GROUPS), axis=0, keepdims=True)
        pick = gid == idx
        gmask = jnp.where(pick, 1.0, gmask)
        gscore = jnp.where(pick, neg, gscore)
    emask = jnp.broadcast_to(gmask.reshape(N_GROUPS, 1, tm), (N_GROUPS, gs, tm)).reshape(ne, tm)
    cand = jnp.where(emask > 0.0, sel, neg)
    eid = lax.broadcasted_iota(I32, (ne, tm), 0)
    chosen = jnp.zeros((ne, tm), F32)
    idxs, scs = [], []
    for _ in range(TOP_K):
        mx = jnp.max(cand, axis=0, keepdims=True)
        idx = jnp.min(jnp.where(cand == mx, eid, ne), axis=0, keepdims=True)
        pick = eid == idx
        idxs.append(idx)
        scs.append(jnp.sum(jnp.where(pick, scores, 0.0), axis=0, keepdims=True))
        chosen = jnp.where(pick, 1.0, chosen)
        cand = jnp.where(pick, neg, cand)
    ssum = scs[0]
    for s in scs[1:]:
        ssum = ssum + s
    before = (lax.broadcasted_iota(I32, (tm, tm), 0) < lax.broadcasted_iota(I32, (tm, tm), 1))
    prior = jnp.dot(chosen.astype(BF16), before.astype(BF16), preferred_element_type=F32) + carry[...]
    for k in range(TOP_K):
        e_ref[k:k + 1, :] = idxs[k]
        w_ref[k:k + 1, :] = scs[k] / ssum * ROUTED_SCALE
        rk = jnp.sum(jnp.where(eid == idxs[k], prior, 0.0), axis=0, keepdims=True)
        rk_ref[k:k + 1, :] = rk.astype(I32)
    carry[...] = carry[...] + jnp.sum(chosen, axis=1, keepdims=True)
    cnt_ref[...] = jnp.broadcast_to(carry[...], cnt_ref.shape).astype(I32)


def _route(x, g, sh, sc, w_router_t, b_router):
    t, d = x.shape
    tm = _tile(t, 256, LANES)
    vec = pl.BlockSpec((1, d), lambda i: (0, 0))
    tok = lambda dt: jax.ShapeDtypeStruct((TOP_K, t), dt)
    tok_spec = pl.BlockSpec((TOP_K, tm), lambda i: (0, i))
    return pl.pallas_call(
        _route_kernel,
        grid=(t // tm,),
        in_specs=[pl.BlockSpec((tm, d), lambda i: (i, 0)), vec, vec, vec,
                  pl.BlockSpec((N_EXPERTS, d), lambda i: (0, 0)),
                  pl.BlockSpec((N_EXPERTS, 1), lambda i: (0, 0))],
        out_specs=[pl.BlockSpec((tm, d // 2), lambda i: (i, 0)), tok_spec, tok_spec, tok_spec,
                   pl.BlockSpec((N_EXPERTS, LANES), lambda i: (0, 0))],
        out_shape=[jax.ShapeDtypeStruct((t, d // 2), U32), tok(I32), tok(F32), tok(I32),
                   jax.ShapeDtypeStruct((N_EXPERTS, LANES), I32)],
        scratch_shapes=[pltpu.VMEM((N_EXPERTS, 1), F32)],
        compiler_params=_params("arbitrary"),
        name="route",
    )(x, g, sh, sc, w_router_t, b_router)


WUNITS = 8
WSLOTS = 3


def _experts_kernel(blk_run, blk_goal, run_e, meta, slot_tok, hp_hbm, wg_hbm, wu_hbm, wd_hbm, o_ref,
                    xbuf, sem, sg, su, sd, wsem, wgb, wub, wdb, done_ref):
    b = pl.program_id(0)
    nb = pl.num_programs(0)
    nused = meta[0]
    nunits = meta[1] * WUNITS
    ug = sg.shape[1]
    ud = sd.shape[1]

    def unit_copies(u):
        e = run_e[u // WUNITS]
        k = u % WUNITS
        s = u % WSLOTS
        return (pltpu.make_async_copy(wg_hbm.at[e, pl.ds(k * ug, ug)], sg.at[s], wsem.at[s]),
                pltpu.make_async_copy(wu_hbm.at[e, pl.ds(k * ug, ug)], su.at[s], wsem.at[s]),
                pltpu.make_async_copy(wd_hbm.at[e, pl.ds(k * ud, ud)], sd.at[s], wsem.at[s]))

    def start_unit(u):
        for cp in unit_copies(u):
            cp.start()

    def process_unit(u):
        for cp in unit_copies(u):
            cp.wait()

        @pl.when(u + WSLOTS - 1 < nunits)
        def _():
            start_unit(u + WSLOTS - 1)

        r = (u // WUNITS) % 2
        k = u % WUNITS
        s = u % WSLOTS
        wgb[r, pl.ds(pl.multiple_of(k * ug, ug), ug), :] = sg[s].astype(BF16)
        wub[r, pl.ds(pl.multiple_of(k * ug, ug), ug), :] = su[s].astype(BF16)
        wdb[r, pl.ds(pl.multiple_of(k * ud, ud), ud), :] = sd[s].astype(BF16)

    def start_gather(blk, slot):
        for r in range(MOE_BLOCK):
            tok = slot_tok[blk * MOE_BLOCK + r]
            pltpu.make_async_copy(hp_hbm.at[pl.ds(tok, 1)], xbuf.at[slot, pl.ds(r, 1)],
                                  sem.at[slot]).start(priority=r % 2)

    def wait_gather(slot):
        pltpu.make_async_copy(hp_hbm.at[pl.ds(0, MOE_BLOCK)], xbuf.at[slot], sem.at[slot]).wait()

    live = b < nused
    slot = b % 2

    @pl.when((b == 0) & live)
    def _():
        done_ref[0] = 0
        for u in range(WSLOTS - 1):
            @pl.when(u < nunits)
            def _():
                start_unit(u)
        start_gather(0, 0)

    @pl.when(live & (b + 1 < nb) & (b + 1 < nused))
    def _():
        start_gather(b + 1, 1 - slot)

    @pl.when(live)
    def _():
        done = done_ref[0]
        goal = blk_goal[b]

        def body(n, carry):
            process_unit(done + n)
            return carry
        lax.fori_loop(0, jnp.maximum(goal - done, 0), body, 0)
        done_ref[0] = jnp.maximum(goal, done)

        r = blk_run[b] % 2
        wait_gather(slot)
        x = _unpack_halves(xbuf[slot])
        hg = jnp.dot(x, wgb[r], preferred_element_type=F32)
        hu = jnp.dot(x, wub[r], preferred_element_type=F32)
        hb = (_silu(hg) * hu).astype(BF16)
        o_ref[...] = _pack_halves(jnp.dot(hb, wdb[r], preferred_element_type=F32))

    @pl.when(jnp.logical_not(live))
    def _():
        o_ref[...] = jnp.zeros_like(o_ref)


def _experts(blk_run, blk_goal, run_e, meta, slot_tok, hp, wg, wu, wd):
    nb = blk_run.shape[0]
    _, d, de = wg.shape
    ug, ud = d // WUNITS, de // WUNITS
    hbm = pl.BlockSpec(memory_space=pl.ANY)
    grid_spec = pltpu.PrefetchScalarGridSpec(
        num_scalar_prefetch=5,
        grid=(nb,),
        in_specs=[hbm, hbm, hbm, hbm],
        out_specs=pl.BlockSpec((MOE_BLOCK, d // 2), lambda b, *_: (b, 0)),
        scratch_shapes=[pltpu.VMEM((2, MOE_BLOCK, d // 2), U32), pltpu.SemaphoreType.DMA((2,)),
                        pltpu.VMEM((WSLOTS, ug, de), F32), pltpu.VMEM((WSLOTS, ug, de), F32),
                        pltpu.VMEM((WSLOTS, ud, d), F32), pltpu.SemaphoreType.DMA((WSLOTS,)),
                        pltpu.VMEM((2, d, de), BF16), pltpu.VMEM((2, d, de), BF16),
                        pltpu.VMEM((2, de, d), BF16), pltpu.SMEM((1,), I32)],
    )
    return pl.pallas_call(
        _experts_kernel,
        grid_spec=grid_spec,
        out_shape=jax.ShapeDtypeStruct((nb * MOE_BLOCK, d // 2), U32),
        compiler_params=_params("arbitrary"),
        name="experts",
    )(blk_run, blk_goal, run_e, meta, slot_tok, hp, wg, wu, wd)


def _shared_kernel(hp_ref, wg_ref, wu_ref, wd_ref, x_ref, ga_ref, o_ref):
    x = _unpack_halves(hp_ref[...])
    hg = jnp.dot(x, wg_ref[...], preferred_element_type=F32)
    hu = jnp.dot(x, wu_ref[...], preferred_element_type=F32)
    hb = (_silu(hg) * hu).astype(BF16)
    o_ref[...] = x_ref[...] + ga_ref[...] * jnp.dot(hb, wd_ref[...], preferred_element_type=F32)


def _shared(hp, wg, wu, wd, x, ga):
    t, d = x.shape
    tm = _tile(t, 256, LANES)
    ds_ = wg.shape[1]
    return pl.pallas_call(
        _shared_kernel,
        grid=(t // tm,),
        in_specs=[pl.BlockSpec((tm, d // 2), lambda i: (i, 0)),
                  pl.BlockSpec((d, ds_), lambda i: (0, 0)),
                  pl.BlockSpec((d, ds_), lambda i: (0, 0)),
                  pl.BlockSpec((ds_, d), lambda i: (0, 0)),
                  pl.BlockSpec((tm, d), lambda i: (i, 0)),
                  pl.BlockSpec((1, d), lambda i: (0, 0))],
        out_specs=pl.BlockSpec((tm, d), lambda i: (i, 0)),
        out_shape=jax.ShapeDtypeStruct((t, d), F32),
        compiler_params=_params("arbitrary"),
        name="shared",
    )(hp, wg, wu, wd, x, ga)


def _combine_kernel(dest, ys_hbm, w_ref, base_ref, ga_ref, gf_ref, o_ref, buf_a, buf_b, sem):
    i = pl.program_id(0)
    n = pl.num_programs(0)
    tm = buf_a.shape[1]
    half = buf_a.shape[2]
    d = 2 * half

    def start_gather(tile, buf, s):
        for j in range(tm * TOP_K):
            src = dest[tile * (tm * TOP_K) + j]
            pltpu.make_async_copy(ys_hbm.at[pl.ds(src, 1)], buf.at[j % TOP_K, pl.ds(j // TOP_K, 1)],
                                  sem.at[s]).start(priority=j % 2)

    def wait_gather(buf, s):
        for k in range(TOP_K):
            pltpu.make_async_copy(ys_hbm.at[pl.ds(0, tm)], buf.at[k], sem.at[s]).wait()

    def finish(buf, rows):
        ylo = jnp.zeros((tm, half), F32)
        yhi = jnp.zeros((tm, half), F32)
        for k in range(TOP_K):
            p = buf[k]
            w = w_ref[rows, k:k + 1]
            ylo = ylo + w * lax.bitcast_convert_type(p << 16, F32)
            yhi = yhi + w * lax.bitcast_convert_type(p & jnp.uint32(0xFFFF0000), F32)
        xlo = base_ref[rows, :half] + ga_ref[:, :half] * ylo
        xhi = base_ref[rows, half:] + ga_ref[:, half:] * yhi
        ms = (jnp.sum(xlo * xlo, axis=-1, keepdims=True) + jnp.sum(xhi * xhi, axis=-1, keepdims=True)) / d
        inv = lax.rsqrt(ms + EPS)
        o_ref[rows, :half] = xlo * inv * gf_ref[:, :half]
        o_ref[rows, half:] = xhi * inv * gf_ref[:, half:]

    def step(gather_next):
        wait_gather(buf_a, 0)
        start_gather(2 * i + 1, buf_b, 1)
        finish(buf_a, slice(0, tm))
        wait_gather(buf_b, 1)
        if gather_next:
            start_gather(2 * i + 2, buf_a, 0)
        finish(buf_b, slice(tm, 2 * tm))

    @pl.when(i == 0)
    def _():
        start_gather(0, buf_a, 0)

    pl.when(i + 1 < n)(functools.partial(step, True))
    pl.when(i + 1 == n)(functools.partial(step, False))


def _combine(dest, ys, wts, base, ga, g_final):
    t, d = base.shape
    tm = 64
    assert t % (2 * tm) == 0
    grid_spec = pltpu.PrefetchScalarGridSpec(
        num_scalar_prefetch=1,
        grid=(t // (2 * tm),),
        in_specs=[pl.BlockSpec(memory_space=pl.ANY),
                  pl.BlockSpec((2 * tm, TOP_K), lambda i, s: (i, 0)),
                  pl.BlockSpec((2 * tm, d), lambda i, s: (i, 0)),
                  pl.BlockSpec((1, d), lambda i, s: (0, 0)),
                  pl.BlockSpec((1, d), lambda i, s: (0, 0))],
        out_specs=pl.BlockSpec((None, 2 * tm, d), lambda i, s: (0, i, 0)),
        scratch_shapes=[pltpu.VMEM((TOP_K, tm, d // 2), U32), pltpu.VMEM((TOP_K, tm, d // 2), U32),
                        pltpu.SemaphoreType.DMA((2,))],
    )
    return pl.pallas_call(
        _combine_kernel,
        grid_spec=grid_spec,
        out_shape=jax.ShapeDtypeStruct((1, t, d), F32),
        compiler_params=_params("arbitrary"),
        name="combine",
    )(dest, ys, wts, base, ga, g_final)


def kernel(x, c, ctx, c_ctx, w_ada, b_ada, g_norm_mix, g_norm_ffn, w_in, w_gate_up, b_gate_up,
           g_gla_out, w_dw, b_dw, g_conv_ln, b_conv_ln, w_out, w_router, b_router,
           w_e_gate, w_e_up, w_e_down, w_s_gate, w_s_up, w_s_down, g_final):
    assert x.shape[0] == 1 and w_ada.shape[0] == 1, "single batch element, single layer"
    t, d = x.shape[1], x.shape[2]
    dk = w_gate_up.shape[3]
    heads = dk // GLA_HK
    dg = heads * GLA_HV
    dc = w_dw.shape[2]

    col_q, col_k, col_v = 0, dk, 2 * dk
    col_r = col_v + dg
    col_ca = col_r + dg
    col_cb = col_ca + dc
    col_low = col_cb + dc
    n_gate = 2 * GATE_RANK
    wi = w_in[0]
    w_a = wi[:, :col_ca].astype(BF16)
    n_pad = -(2 * dc + n_gate) % _tile(col_ca, 512, 256)
    w_b = jnp.concatenate([wi[:, col_ca + n_gate:].astype(BF16), wi[:, col_ca:col_ca + n_gate].astype(BF16),
                           jnp.zeros((d, n_pad), BF16)], axis=1)

    mod = _ada(jnp.stack([c[0], c_ctx]), w_ada[0], b_ada[0])
    sh_m, sc_m, ga_m, sh_f, sc_f, ga_f = [mod[0:1, k * d:(k + 1) * d] for k in range(6)]
    csh_m, csc_m = mod[1:2, 0:d], mod[1:2, d:2 * d]

    wg_f = jnp.zeros((LANES, dk), F32).at[:GATE_RANK].set(w_gate_up[0, 0]).astype(BF16)
    wg_b = jnp.zeros((LANES, dk), F32).at[GATE_RANK:n_gate].set(w_gate_up[0, 1]).astype(BF16)
    bg_f = b_gate_up[0, 0:1]
    bg_b = b_gate_up[0, 1:2]
    cols = dict(col_q=col_q, col_k=col_k, col_v=col_v, col_low=col_low)

    gn_mix = g_norm_mix[0:1]
    proj_c = _proj(ctx, gn_mix, csh_m, csc_m, w_a, w_b)
    s0 = jnp.zeros((2, heads // 2, GLA_HV, 2 * GLA_HK), F32)
    s_ctx = _gla(proj_c, wg_f, wg_b, bg_f, bg_b, s0, with_output=False, **cols)

    proj_l = _proj(x, gn_mix, sh_m, sc_m, w_a, w_b)
    o_f, o_b = _gla(proj_l, wg_f, wg_b, bg_f, bg_b, s_ctx, with_output=True, **cols)
    mix = _mix(o_f, o_b, proj_l, g_gla_out[0:1], w_dw[0], b_dw[0:1], g_conv_ln[0:1], b_conv_ln[0:1],
               col_r, col_ca, col_cb)
    x1 = _outproj(mix, w_out[0].astype(BF16), x, ga_m)

    hp, eidx, wts, rank, cnt = _route(x1, g_norm_ffn[0:1], sh_f, sc_f, w_router[0].T,
                                      b_router[0].reshape(N_EXPERTS, 1))
    nk = t * TOP_K
    nb = (nk + MOE_BLOCK - 1) // MOE_BLOCK + N_EXPERTS
    counts = cnt[:, 0]
    pcounts = (counts + MOE_BLOCK - 1) // MOE_BLOCK * MOE_BLOCK
    pend = jnp.cumsum(pcounts)
    pstart = pend - pcounts
    experts = jnp.arange(N_EXPERTS, dtype=I32)
    pstart_tok = jnp.sum(jnp.where(eidx[:, :, None] == experts, pstart, 0), axis=-1)
    dest = (pstart_tok + rank).T.reshape(-1).astype(I32)
    tok_ids = jnp.repeat(jnp.arange(t, dtype=I32), TOP_K)
    slot_tok = jnp.zeros((nb * MOE_BLOCK,), I32).at[dest].set(tok_ids, unique_indices=True)
    blk_start = jnp.arange(nb, dtype=I32) * MOE_BLOCK
    blk_e = jnp.minimum(jnp.sum(pend[None, :] <= blk_start[:, None], axis=1), N_EXPERTS - 1).astype(I32)
    owns = counts > 0
    run_of_e = jnp.cumsum(owns.astype(I32)) - 1
    run_e = jnp.sum(jnp.where(owns[None, :] & (run_of_e[None, :] == experts[:, None]), experts[None, :], 0),
                    axis=1).astype(I32)
    onehot = blk_e[:, None] == experts[None, :]
    blk_run = jnp.sum(jnp.where(onehot, run_of_e[None, :], 0), axis=1).astype(I32)
    nruns = jnp.sum(owns.astype(I32))
    meta = jnp.stack([pend[-1] // MOE_BLOCK, nruns]).astype(I32)
    blk_first = jnp.sum(jnp.where(onehot, pstart[None, :], 0), axis=1) // MOE_BLOCK
    run_len = jnp.maximum(jnp.sum(jnp.where(onehot, pcounts[None, :], 0), axis=1) // MOE_BLOCK, 1)
    pos = jnp.arange(nb, dtype=I32) - blk_first
    share = (WUNITS * (pos + 1) + run_len - 1) // run_len
    blk_goal = jnp.minimum(WUNITS * (blk_run + 1) + share, WUNITS * nruns).astype(I32)

    ys = _experts(blk_run, blk_goal, run_e, meta, slot_tok, hp, w_e_gate[0], w_e_up[0], w_e_down[0])
    base = _shared(hp, w_s_gate[0].astype(BF16), w_s_up[0].astype(BF16), w_s_down[0].astype(BF16), x1, ga_f)
    return _combine(dest, ys, wts.T, base, ga_f, g_final.reshape(1, d))
```

```python
import functools

import jax
import jax.numpy as jnp
import numpy as np
from jax import lax
from jax.experimental import pallas as pl
from jax.experimental.pallas import tpu as pltpu

F32 = jnp.float32
BF16 = jnp.bfloat16
I32 = jnp.int32
U32 = jnp.uint32

EPS = 1e-6
LANES = 128
GLA_HK = 64
GLA_HV = 128
GLA_CHUNK = 64
GATE_RANK = 16
GATE_TAU = 16.0
CONV_WIDTH = 31
CONV_PAD = (CONV_WIDTH - 1) // 2
CONV_LEAD = 16
GRID_W = 64
N_EXPERTS = 64
N_GROUPS = 8
TOPK_GROUPS = 4
TOP_K = 8
ROUTED_SCALE = 2.5
MOE_BLOCK = 256
VMEM_LIMIT = 56 * 1024 * 1024

NT_DIMS = (((1,), (1,)), ((), ()))
TN_DIMS = (((0,), (0,)), ((), ()))


def _params(*sem):
    return pltpu.CompilerParams(dimension_semantics=sem, vmem_limit_bytes=VMEM_LIMIT)


def _tile(n, *preferred):
    for p in preferred:
        if n % p == 0:
            return p
    raise ValueError(f"no tile for {n} among {preferred}")


def _sigmoid(x):
    return 1.0 / (1.0 + jnp.exp(-x))


def _silu(x):
    return x * _sigmoid(x)


def _ada_kernel(cb_ref, w_ref, b_ref, o_ref, s_scr):
    @pl.when(pl.program_id(0) == 0)
    def _():
        s_scr[...] = _silu(cb_ref[...])

    d, tn = w_ref.shape
    for jt in range(tn // LANES):
        sl = slice(jt * LANES, (jt + 1) * LANES)
        w = w_ref[:, sl]
        for m in range(2):
            p = (w * s_scr[m]).reshape(d // 8, 8, LANES).sum(axis=0)
            o_ref[m:m + 1, sl] = p.sum(axis=0, keepdims=True) + b_ref[:, sl]


def _ada(c2, w_ada, b_ada):
    d, n = w_ada.shape
    tn = _tile(n, 1024, 512, LANES)
    cb = jnp.broadcast_to(c2[:, :, None], (2, d, LANES))
    return pl.pallas_call(
        _ada_kernel,
        grid=(n // tn,),
        in_specs=[pl.BlockSpec((2, d, LANES), lambda j: (0, 0, 0)),
                  pl.BlockSpec((d, tn), lambda j: (0, j)),
                  pl.BlockSpec((1, tn), lambda j: (0, j))],
        out_specs=pl.BlockSpec((2, tn), lambda j: (0, j)),
        out_shape=jax.ShapeDtypeStruct((2, n), F32),
        scratch_shapes=[pltpu.VMEM((2, d, LANES), F32)],
        compiler_params=_params("arbitrary"),
        name="ada",
    )(cb, w_ada, b_ada.reshape(1, n))


def _norm_mod(x, g, sh, sc):
    ms = jnp.mean(x * x, axis=-1, keepdims=True)
    y = x * lax.rsqrt(ms + EPS) * g
    return y * (1.0 + sc) + sh


def _proj_kernel(x_ref, g_ref, sh_ref, sc_ref, wl_ref, wa_ref, wb_ref, o_ref, low_ref, h_scr, *, na):
    j = pl.program_id(1)

    @pl.when(j == 0)
    def _():
        rows = min(256, h_scr.shape[0])
        for r in range(0, h_scr.shape[0], rows):
            h_scr[r:r + rows] = _norm_mod(x_ref[r:r + rows], g_ref[...], sh_ref[...], sc_ref[...]).astype(BF16)
        low_ref[...] = jnp.dot(h_scr[...], wl_ref[...], preferred_element_type=F32)

    @pl.when(j < na)
    def _():
        o_ref[...] = jnp.dot(h_scr[...], wa_ref[...], preferred_element_type=F32)

    @pl.when(j >= na)
    def _():
        o_ref[...] = jnp.dot(h_scr[...], wb_ref[...], preferred_element_type=F32)


def _proj(x, g, sh, sc, w_low, w, cols_a, wb):
    _, t, d = x.shape
    tm = _tile(t, 1024, 512, 256, 128)
    tn = _tile(cols_a, 512, 256)
    na, nb = cols_a // tn, wb.shape[1] // tn
    assert wb.shape[1] % tn == 0
    vec = pl.BlockSpec((1, d), lambda i, j: (0, 0))
    return pl.pallas_call(
        functools.partial(_proj_kernel, na=na),
        grid=(t // tm, na + nb),
        in_specs=[pl.BlockSpec((None, tm, d), lambda i, j: (0, i, 0), pipeline_mode=pl.Buffered(1)),
                  vec, vec, vec,
                  pl.BlockSpec((d, LANES), lambda i, j: (0, 0)),
                  pl.BlockSpec((d, tn), lambda i, j: (0, jnp.minimum(j, na - 1))),
                  pl.BlockSpec((d, tn), lambda i, j: (0, jnp.maximum(j - na, 0)))],
        out_specs=[pl.BlockSpec((tm, tn), lambda i, j: (i, j)),
                   pl.BlockSpec((tm, LANES), lambda i, j: (i, 0))],
        out_shape=[jax.ShapeDtypeStruct((t, (na + nb) * tn), F32),
                   jax.ShapeDtypeStruct((t, LANES), F32)],
        scratch_shapes=[pltpu.VMEM((tm, d), BF16)],
        compiler_params=_params("arbitrary", "arbitrary"),
        name="proj",
    )(x, g, sh, sc, w_low, w, wb)


def _gla_block(q_ref, k_ref, v_ref, low_ref, wg, bg, tri, s_t, o_ref, fwd, with_output):
    c = GLA_CHUNK
    nch = q_ref.shape[0] // c
    z = jnp.dot(low_ref[...].astype(BF16), wg, preferred_element_type=F32) + bg
    g = (jnp.minimum(z, 0.0) - jnp.log(1.0 + jnp.exp(-jnp.abs(z)))) * (1.0 / GATE_TAU)
    g1 = g.astype(BF16)
    r1 = g - g1.astype(F32)
    g2 = r1.astype(BF16)
    g3 = (r1 - g2.astype(F32)).astype(BF16)
    b3 = jnp.dot(tri, jnp.concatenate([g1, g2, g3], axis=1), preferred_element_type=F32)
    b = b3[:, :LANES] + b3[:, LANES:2 * LANES] + b3[:, 2 * LANES:]

    def chunk_rows(n):
        return slice(n * c, (n + 1) * c)

    def per_chunk(row):
        return jnp.concatenate([jnp.broadcast_to(b[n * c + row:n * c + row + 1], (c, LANES))
                                for n in range(nch)], axis=0)

    tot = per_chunk(c - 1 if fwd else 0)
    lane0 = lax.broadcasted_iota(I32, (c, LANES), 1) < GLA_HK

    def stack(a, n):
        an = a[chunk_rows(n)]
        return jnp.concatenate([jnp.where(lane0, an, 0.0), jnp.where(lane0, 0.0, an)], axis=0).astype(BF16)

    k = k_ref[...]
    ks = k * jnp.exp(tot - b)
    vs = [jnp.concatenate([v_ref[chunk_rows(n), :GLA_HV], v_ref[chunk_rows(n), GLA_HV:]],
                          axis=0).astype(BF16) for n in range(nch)]
    upd = [lax.dot_general(vs[n], stack(ks, n), TN_DIMS, preferred_element_type=F32) for n in range(nch)]
    decay = jnp.exp(tot)
    order = range(nch) if fwd else range(nch - 1, -1, -1)
    s_in = [None] * nch
    for n in order:
        s_in[n] = s_t
        s_t = s_t * decay[n * c:n * c + 1] + upd[n]
    if not with_output:
        return s_t
    mid = per_chunk(c // 2)
    qs = q_ref[...] * (GLA_HK ** -0.5)
    qa = qs * jnp.exp(b - mid)
    ka = k * jnp.exp(mid - b)
    qi = qs * jnp.exp(b)
    r2 = lax.broadcasted_iota(I32, (2 * c, 2 * c), 0)
    c2 = lax.broadcasted_iota(I32, (2 * c, 2 * c), 1)
    keep = ((r2 >= c) == (c2 >= c)) & ((c2 <= r2) if fwd else (c2 >= r2))
    for n in range(nch):
        att = lax.dot_general(stack(qa, n), stack(ka, n), NT_DIMS, preferred_element_type=F32)
        att = jnp.where(keep, att, 0.0).astype(BF16)
        o = jnp.dot(att, vs[n], preferred_element_type=F32)
        o = o + lax.dot_general(stack(qi, n), s_in[n].astype(BF16), NT_DIMS, preferred_element_type=F32)
        o_ref[chunk_rows(n)] = jnp.concatenate([o[:c], o[c:]], axis=1)
    return s_t


def _gla_kernel(*refs, with_output):
    (qf, kf, vf, lf, qb, kb, vb, lb, wgf, wgb, bgf, bgb, s0_ref) = refs[:13]
    if with_output:
        of_ref, ob_ref, s_scr, tri_scr = refs[13:]
    else:
        sout_ref, s_scr, tri_scr = refs[13:]
        of_ref = ob_ref = None
    i = pl.program_id(1)
    c = GLA_CHUNK
    rows = qf.shape[0]

    @pl.when((pl.program_id(0) == 0) & (i == 0))
    def _():
        r = lax.broadcasted_iota(I32, (rows, rows), 0)
        cc = lax.broadcasted_iota(I32, (rows, rows), 1)
        shift = c.bit_length() - 1
        same = jnp.right_shift(r, shift) == jnp.right_shift(cc, shift)
        tri_scr[0] = jnp.where(same & (cc <= r), 1.0, 0.0).astype(BF16)
        tri_scr[1] = jnp.where(same & (cc >= r), 1.0, 0.0).astype(BF16)

    @pl.when(i == 0)
    def _():
        s_scr[...] = s0_ref[:, 0]

    s_scr[0] = _gla_block(qf, kf, vf, lf, wgf[...], bgf[...], tri_scr[0], s_scr[0], of_ref, True, with_output)
    s_scr[1] = _gla_block(qb, kb, vb, lb, wgb[...], bgb[...], tri_scr[1], s_scr[1], ob_ref, False, with_output)
    if not with_output:
        @pl.when(i == pl.num_programs(1) - 1)
        def _():
            sout_ref[:, 0] = s_scr[...]


def _gla(proj, low, wg_f, wg_b, bg_f, bg_b, s0, with_output, col_q, col_k, col_v):
    t = proj.shape[0]
    cb = _tile(t // GLA_CHUNK, 8, 4, 2, 1)
    rows = cb * GLA_CHUNK
    nb = t // rows
    npairs = wg_f.shape[1] // (2 * GLA_HK)
    pw = 2 * GLA_HK
    vw = 2 * GLA_HV
    fi = lambda p, i: i
    bi = lambda p, i: nb - 1 - i

    def spec(width, col0, blk):
        return pl.BlockSpec((rows, width), lambda p, i: (blk(p, i), col0 // width + p))

    def low_spec(blk):
        return pl.BlockSpec((rows, LANES), lambda p, i: (blk(p, i), 0))

    in_specs = [spec(pw, col_q, fi), spec(pw, col_k, fi), spec(vw, col_v, fi), low_spec(fi),
                spec(pw, col_q, bi), spec(pw, col_k, bi), spec(vw, col_v, bi), low_spec(bi),
                pl.BlockSpec((LANES, pw), lambda p, i: (0, p)),
                pl.BlockSpec((LANES, pw), lambda p, i: (0, p)),
                pl.BlockSpec((1, pw), lambda p, i: (0, p)),
                pl.BlockSpec((1, pw), lambda p, i: (0, p)),
                pl.BlockSpec((2, 1, GLA_HV, pw), lambda p, i: (0, p, 0, 0))]
    if with_output:
        out_specs = [pl.BlockSpec((rows, vw), lambda p, i: (i, p)),
                     pl.BlockSpec((rows, vw), lambda p, i: (nb - 1 - i, p))]
        out_shape = [jax.ShapeDtypeStruct((t, npairs * 2 * GLA_HV), F32)] * 2
    else:
        out_specs = pl.BlockSpec((2, 1, GLA_HV, pw), lambda p, i: (0, p, 0, 0))
        out_shape = jax.ShapeDtypeStruct((2, npairs, GLA_HV, pw), F32)
    return pl.pallas_call(
        functools.partial(_gla_kernel, with_output=with_output),
        grid=(npairs, nb),
        in_specs=in_specs,
        out_specs=out_specs,
        out_shape=out_shape,
        scratch_shapes=[pltpu.VMEM((2, GLA_HV, pw), F32), pltpu.VMEM((2, rows, rows), BF16)],
        compiler_params=_params("arbitrary", "arbitrary"),
        name="gla_out" if with_output else "gla_state",
    )(proj, proj, proj, low, proj, proj, proj, low, wg_f, wg_b, bg_f, bg_b, s0)


def _mix_kernel(of_ref, ob_ref, r_ref, ca_ref, cb_ref, gout_ref, wdw_ref, bdw_ref, gln_ref, bln_ref, pick_ref,
                o_ref, upad, y_scr):
    tm = of_ref.shape[0]
    dg = of_ref.shape[1]
    dc = ca_ref.shape[1]
    nrow = tm // GRID_W
    lead = CONV_LEAD
    for h in range(dg // GLA_HV):
        sl = slice(h * GLA_HV, (h + 1) * GLA_HV)
        o = of_ref[:, sl] + ob_ref[:, sl]
        o = o * lax.rsqrt(jnp.mean(o * o, axis=-1, keepdims=True) + EPS) * gout_ref[:, sl]
        o_ref[:, sl] = (o * _silu(r_ref[:, sl])).astype(o_ref.dtype)
    for r in range(nrow):
        rs = slice(r * GRID_W, (r + 1) * GRID_W)
        upad[r, 0:lead, :] = jnp.zeros((lead, dc), F32)
        upad[r, lead:lead + GRID_W, :] = ca_ref[rs, :] * _sigmoid(cb_ref[rs, :])
        upad[r, lead + GRID_W:, :] = jnp.zeros((lead, dc), F32)
    cw = 256
    span = upad.shape[1]
    kpad = pick_ref.shape[1] - CONV_WIDTH * span

    def row_body(r, carry):
        for cc in range(dc // cw):
            cs = slice(cc * cw, (cc + 1) * cw)
            u = upad[r, :, cs]
            taps = [(wdw_ref[j:j + 1, cs] * u).astype(BF16) for j in range(CONV_WIDTH)]
            taps.append(jnp.zeros((kpad, cw), BF16))
            y_scr[r, :, cs] = (jnp.dot(pick_ref[...], jnp.concatenate(taps, axis=0), preferred_element_type=F32)
                               + bdw_ref[:, cs])
        return carry

    lax.fori_loop(0, nrow, row_body, 0)
    for r in range(nrow):
        y = y_scr[r]
        mu = jnp.mean(y, axis=-1, keepdims=True)
        yc = y - mu
        var = jnp.mean(yc * yc, axis=-1, keepdims=True)
        yn = yc * lax.rsqrt(var + EPS) * gln_ref[...] + bln_ref[...]
        o_ref[r * GRID_W:(r + 1) * GRID_W, dg:] = _silu(yn).astype(o_ref.dtype)


def _mix(o_f, o_b, proj, g_out, w_dw, b_dw, g_ln, b_ln, col_r, col_ca, col_cb):
    t, dg = o_f.shape
    tm = _tile(t, 256, 128, GRID_W)
    dc = w_dw.shape[1]
    nrow = tm // GRID_W
    span = GRID_W + 2 * CONV_LEAD
    kdim = -(-CONV_WIDTH * span // LANES) * LANES
    pick = np.zeros((GRID_W, kdim), np.float32)
    for j in range(CONV_WIDTH):
        pick[np.arange(GRID_W), j * span + np.arange(GRID_W) + (CONV_LEAD - CONV_PAD + j)] = 1.0
    pick = jnp.asarray(pick, BF16)
    row = lambda width: pl.BlockSpec((1, width), lambda i: (0, 0))
    return pl.pallas_call(
        _mix_kernel,
        grid=(t // tm,),
        in_specs=[pl.BlockSpec((tm, dg), lambda i: (i, 0)),
                  pl.BlockSpec((tm, dg), lambda i: (i, 0)),
                  pl.BlockSpec((tm, dg), lambda i: (i, col_r // dg)),
                  pl.BlockSpec((tm, dc), lambda i: (i, col_ca // dc)),
                  pl.BlockSpec((tm, dc), lambda i: (i, col_cb // dc)),
                  row(dg),
                  pl.BlockSpec((CONV_WIDTH, dc), lambda i: (0, 0)),
                  row(dc), row(dc), row(dc),
                  pl.BlockSpec((GRID_W, kdim), lambda i: (0, 0))],
        out_specs=pl.BlockSpec((tm, dg + dc), lambda i: (i, 0)),
        out_shape=jax.ShapeDtypeStruct((t, dg + dc), BF16),
        scratch_shapes=[pltpu.VMEM((nrow, span, dc), F32),
                        pltpu.VMEM((nrow, GRID_W, dc), F32)],
        compiler_params=_params("arbitrary"),
        name="mix",
    )(o_f, o_b, proj, proj, proj, g_out, w_dw, b_dw, g_ln, b_ln, pick)


def _outproj_kernel(m_ref, w_ref, x_ref, ga_ref, o_ref):
    o_ref[...] = x_ref[...] + ga_ref[...] * jnp.dot(m_ref[...], w_ref[...], preferred_element_type=F32)


def _outproj(mix, w, x, ga):
    t, k = mix.shape
    n = w.shape[1]
    tm = _tile(t, 1024, 512, 256)
    tn = _tile(n, 512, 256, LANES)
    return pl.pallas_call(
        _outproj_kernel,
        grid=(t // tm, n // tn),
        in_specs=[pl.BlockSpec((tm, k), lambda i, j: (i, 0)),
                  pl.BlockSpec((k, tn), lambda i, j: (0, j)),
                  pl.BlockSpec((None, tm, tn), lambda i, j: (0, i, j)),
                  pl.BlockSpec((1, tn), lambda i, j: (0, j))],
        out_specs=pl.BlockSpec((tm, tn), lambda i, j: (i, j)),
        out_shape=jax.ShapeDtypeStruct((t, n), F32),
        compiler_params=_params("arbitrary", "arbitrary"),
        name="out_proj",
    )(mix, w, x, ga)


def _pack_halves(h):
    c = h.shape[1] // 2
    lo = lax.bitcast_convert_type(h[:, :c].astype(BF16).astype(F32), U32)
    hi = lax.bitcast_convert_type(h[:, c:].astype(BF16).astype(F32), U32)
    return (lo >> 16) | (hi & jnp.uint32(0xFFFF0000))


def _unpack_halves(p):
    lo = lax.bitcast_convert_type(p << 16, F32).astype(BF16)
    hi = lax.bitcast_convert_type(p & jnp.uint32(0xFFFF0000), F32).astype(BF16)
    return jnp.concatenate([lo, hi], axis=1)


def _route_kernel(x_ref, g_ref, sh_ref, sc_ref, wr_ref, br_ref, hp_ref, e_ref, w_ref, rk_ref, cnt_ref,
                  carry):
    tm = x_ref.shape[0]
    ne = N_EXPERTS
    gs = ne // N_GROUPS
    neg = -jnp.inf

    @pl.when(pl.program_id(0) == 0)
    def _():
        carry[...] = jnp.zeros_like(carry)

    h = _norm_mod(x_ref[...], g_ref[...], sh_ref[...], sc_ref[...])
    hp_ref[...] = _pack_halves(h)
    logits = lax.dot_general(wr_ref[...], h, NT_DIMS, preferred_element_type=F32,
                             precision=lax.Precision.HIGHEST)
    scores = _sigmoid(logits)
    sel = scores + br_ref[...]
    sel3 = sel.reshape(N_GROUPS, gs, tm)
    sub = lax.broadcasted_iota(I32, (N_GROUPS, gs, tm), 1)
    m1 = jnp.max(sel3, axis=1, keepdims=True)
    i1 = jnp.min(jnp.where(sel3 == m1, sub, gs), axis=1, keepdims=True)
    m2 = jnp.max(jnp.where(sub == i1, neg, sel3), axis=1, keepdims=True)
    gscore = (m1 + m2).reshape(N_GROUPS, tm)
    gid = lax.broadcasted_iota(I32, (N_GROUPS, tm), 0)
    gmask = jnp.zeros((N_GROUPS, tm), F32)
    for _ in range(TOPK_GROUPS):
        mx = jnp.max(gscore, axis=0, keepdims=True)
        idx = jnp.min(jnp.where(gscore == mx, gid, N_GROUPS), axis=0, keepdims=True)
        pick = gid == idx
        gmask = jnp.where(pick, 1.0, gmask)
        gscore = jnp.where(pick, neg, gscore)
    emask = jnp.broadcast_to(gmask.reshape(N_GROUPS, 1, tm), (N_GROUPS, gs, tm)).reshape(ne, tm)
    cand = jnp.where(emask > 0.0, sel, neg)
    eid = lax.broadcasted_iota(I32, (ne, tm), 0)
    chosen = jnp.zeros((ne, tm), F32)
    idxs, scs = [], []
    for _ in range(TOP_K):
        mx = jnp.max(cand, axis=0, keepdims=True)
        idx = jnp.min(jnp.where(cand == mx, eid, ne), axis=0, keepdims=True)
        pick = eid == idx
        idxs.append(idx)
        scs.append(jnp.sum(jnp.where(pick, scores, 0.0), axis=0, keepdims=True))
        chosen = jnp.where(pick, 1.0, chosen)
        cand = jnp.where(pick, neg, cand)
    ssum = scs[0]
    for s in scs[1:]:
        ssum = ssum + s
    before = (lax.broadcasted_iota(I32, (tm, tm), 0) < lax.broadcasted_iota(I32, (tm, tm), 1))
    prior = jnp.dot(chosen.astype(BF16), before.astype(BF16), preferred_element_type=F32) + carry[...]
    for k in range(TOP_K):
        e_ref[k:k + 1, :] = idxs[k]
        w_ref[k:k + 1, :] = scs[k] / ssum * ROUTED_SCALE
        rk = jnp.sum(jnp.where(eid == idxs[k], prior, 0.0), axis=0, keepdims=True)
        rk_ref[k:k + 1, :] = rk.astype(I32)
    carry[...] = carry[...] + jnp.sum(chosen, axis=1, keepdims=True)
    cnt_ref[...] = jnp.broadcast_to(carry[...], cnt_ref.shape).astype(I32)


def _route(x, g, sh, sc, w_router_t, b_router):
    t, d = x.shape
    tm = _tile(t, 256, LANES)
    vec = pl.BlockSpec((1, d), lambda i: (0, 0))
    tok = lambda dt: jax.ShapeDtypeStruct((TOP_K, t), dt)
    tok_spec = pl.BlockSpec((TOP_K, tm), lambda i: (0, i))
    return pl.pallas_call(
        _route_kernel,
        grid=(t // tm,),
        in_specs=[pl.BlockSpec((tm, d), lambda i: (i, 0)), vec, vec, vec,
                  pl.BlockSpec((N_EXPERTS, d), lambda i: (0, 0)),
                  pl.BlockSpec((N_EXPERTS, 1), lambda i: (0, 0))],
        out_specs=[pl.BlockSpec((tm, d // 2), lambda i: (i, 0)), tok_spec, tok_spec, tok_spec,
                   pl.BlockSpec((N_EXPERTS, LANES), lambda i: (0, 0))],
        out_shape=[jax.ShapeDtypeStruct((t, d // 2), U32), tok(I32), tok(F32), tok(I32),
                   jax.ShapeDtypeStruct((N_EXPERTS, LANES), I32)],
        scratch_shapes=[pltpu.VMEM((N_EXPERTS, 1), F32)],
        compiler_params=_params("arbitrary"),
        name="route",
    )(x, g, sh, sc, w_router_t, b_router)


WUNITS = 8
WSLOTS = 3


def _experts_kernel(blk_run, blk_goal, run_e, meta, slot_tok, hp_hbm, wg_hbm, wu_hbm, wd_hbm, o_ref,
                    xbuf, sem, sg, su, sd, wsem, wgb, wub, wdb, done_ref):
    b = pl.program_id(0)
    nb = pl.num_programs(0)
    nused = meta[0]
    nunits = meta[1] * WUNITS
    ug = sg.shape[1]
    ud = sd.shape[1]

    def unit_copies(u):
        e = run_e[u // WUNITS]
        k = u % WUNITS
        s = u % WSLOTS
        return (pltpu.make_async_copy(wg_hbm.at[e, pl.ds(k * ug, ug)], sg.at[s], wsem.at[s]),
                pltpu.make_async_copy(wu_hbm.at[e, pl.ds(k * ug, ug)], su.at[s], wsem.at[s]),
                pltpu.make_async_copy(wd_hbm.at[e, pl.ds(k * ud, ud)], sd.at[s], wsem.at[s]))

    def start_unit(u):
        for cp in unit_copies(u):
            cp.start()

    def process_unit(u):
        for cp in unit_copies(u):
            cp.wait()

        @pl.when(u + WSLOTS - 1 < nunits)
        def _():
            start_unit(u + WSLOTS - 1)

        r = (u // WUNITS) % 2
        k = u % WUNITS
        s = u % WSLOTS
        wgb[r, pl.ds(pl.multiple_of(k * ug, ug), ug), :] = sg[s].astype(BF16)
        wub[r, pl.ds(pl.multiple_of(k * ug, ug), ug), :] = su[s].astype(BF16)
        wdb[r, pl.ds(pl.multiple_of(k * ud, ud), ud), :] = sd[s].astype(BF16)

    def start_gather(blk, slot):
        for r in range(MOE_BLOCK):
            tok = slot_tok[blk * MOE_BLOCK + r]
            pltpu.make_async_copy(hp_hbm.at[pl.ds(tok, 1)], xbuf.at[slot, pl.ds(r, 1)],
                                  sem.at[slot]).start(priority=r % 2)

    def wait_gather(slot):
        part = 128
        for r0 in range(0, MOE_BLOCK, part):
            pltpu.make_async_copy(hp_hbm.at[pl.ds(0, part)], xbuf.at[slot, pl.ds(r0, part)], sem.at[slot]).wait()

    live = b < nused
    slot = b % 2

    @pl.when((b == 0) & live)
    def _():
        done_ref[0] = 0
        for u in range(WSLOTS - 1):
            @pl.when(u < nunits)
            def _():
                start_unit(u)
        start_gather(0, 0)

    next_live = live & (b + 1 < nb) & (b + 1 < nused)
    for parity in range(2):
        @pl.when(next_live & (slot == parity))
        def _():
            start_gather(b + 1, 1 - parity)

    @pl.when(live)
    def _():
        done = done_ref[0]
        goal = blk_goal[b]

        def body(n, carry):
            process_unit(done + n)
            return carry
        lax.fori_loop(0, jnp.maximum(goal - done, 0), body, 0)
        done_ref[0] = jnp.maximum(goal, done)

        r = blk_run[b] % 2
        wait_gather(slot)
        x = _unpack_halves(xbuf[slot])
        hg = jnp.dot(x, wgb[r], preferred_element_type=F32)
        hu = jnp.dot(x, wub[r], preferred_element_type=F32)
        hb = (_silu(hg) * hu).astype(BF16)
        o_ref[...] = _pack_halves(jnp.dot(hb, wdb[r], preferred_element_type=F32))

    @pl.when(jnp.logical_not(live))
    def _():
        o_ref[...] = jnp.zeros_like(o_ref)


def _experts(blk_run, blk_goal, run_e, meta, slot_tok, hp, wg, wu, wd):
    nb = blk_run.shape[0]
    _, d, de = wg.shape
    ug, ud = d // WUNITS, de // WUNITS
    hbm = pl.BlockSpec(memory_space=pl.ANY)
    grid_spec = pltpu.PrefetchScalarGridSpec(
        num_scalar_prefetch=5,
        grid=(nb,),
        in_specs=[hbm, hbm, hbm, hbm],
        out_specs=pl.BlockSpec((MOE_BLOCK, d // 2), lambda b, *_: (b, 0)),
        scratch_shapes=[pltpu.VMEM((2, MOE_BLOCK, d // 2), U32), pltpu.SemaphoreType.DMA((2,)),
                        pltpu.VMEM((WSLOTS, ug, de), F32), pltpu.VMEM((WSLOTS, ug, de), F32),
                        pltpu.VMEM((WSLOTS, ud, d), F32), pltpu.SemaphoreType.DMA((WSLOTS,)),
                        pltpu.VMEM((2, d, de), BF16), pltpu.VMEM((2, d, de), BF16),
                        pltpu.VMEM((2, de, d), BF16), pltpu.SMEM((1,), I32)],
    )
    return pl.pallas_call(
        _experts_kernel,
        grid_spec=grid_spec,
        out_shape=jax.ShapeDtypeStruct((nb * MOE_BLOCK, d // 2), U32),
        compiler_params=_params("arbitrary"),
        name="experts",
    )(blk_run, blk_goal, run_e, meta, slot_tok, hp, wg, wu, wd)


def _shared_kernel(hp_ref, wg_ref, wu_ref, wd_ref, x_ref, ga_ref, o_ref):
    x = _unpack_halves(hp_ref[...])
    hg = jnp.dot(x, wg_ref[...], preferred_element_type=F32)
    hu = jnp.dot(x, wu_ref[...], preferred_element_type=F32)
    hb = (_silu(hg) * hu).astype(BF16)
    o_ref[...] = x_ref[...] + ga_ref[...] * jnp.dot(hb, wd_ref[...], preferred_element_type=F32)


def _shared(hp, wg, wu, wd, x, ga):
    t, d = x.shape
    tm = _tile(t, 256, LANES)
    ds_ = wg.shape[1]
    return pl.pallas_call(
        _shared_kernel,
        grid=(t // tm,),
        in_specs=[pl.BlockSpec((tm, d // 2), lambda i: (i, 0)),
                  pl.BlockSpec((d, ds_), lambda i: (0, 0)),
                  pl.BlockSpec((d, ds_), lambda i: (0, 0)),
                  pl.BlockSpec((ds_, d), lambda i: (0, 0)),
                  pl.BlockSpec((tm, d), lambda i: (i, 0)),
                  pl.BlockSpec((1, d), lambda i: (0, 0))],
        out_specs=pl.BlockSpec((tm, d), lambda i: (i, 0)),
        out_shape=jax.ShapeDtypeStruct((t, d), F32),
        compiler_params=_params("arbitrary"),
        name="shared",
    )(hp, wg, wu, wd, x, ga)


def _combine_kernel(dest, ys_hbm, w_ref, base_ref, ga_ref, gf_ref, o_ref, buf_a, buf_b, sem):
    i = pl.program_id(0)
    n = pl.num_programs(0)
    tm = buf_a.shape[1]
    half = buf_a.shape[2]
    d = 2 * half

    def start_gather(tile, buf, s):
        for j in range(tm * TOP_K):
            src = dest[tile * (tm * TOP_K) + j]
            pltpu.make_async_copy(ys_hbm.at[pl.ds(src, 1)], buf.at[j % TOP_K, pl.ds(j // TOP_K, 1)],
                                  sem.at[s]).start(priority=j % 2)

    def wait_gather(buf, s):
        for k in range(TOP_K):
            pltpu.make_async_copy(ys_hbm.at[pl.ds(0, tm)], buf.at[k], sem.at[s]).wait()

    def finish(buf, rows):
        ylo = jnp.zeros((tm, half), F32)
        yhi = jnp.zeros((tm, half), F32)
        for k in range(TOP_K):
            p = buf[k]
            w = w_ref[rows, k:k + 1]
            ylo = ylo + w * lax.bitcast_convert_type(p << 16, F32)
            yhi = yhi + w * lax.bitcast_convert_type(p & jnp.uint32(0xFFFF0000), F32)
        xlo = base_ref[rows, :half] + ga_ref[:, :half] * ylo
        xhi = base_ref[rows, half:] + ga_ref[:, half:] * yhi
        ms = (jnp.sum(xlo * xlo, axis=-1, keepdims=True) + jnp.sum(xhi * xhi, axis=-1, keepdims=True)) / d
        inv = lax.rsqrt(ms + EPS)
        o_ref[rows, :half] = xlo * inv * gf_ref[:, :half]
        o_ref[rows, half:] = xhi * inv * gf_ref[:, half:]

    def step(gather_next):
        wait_gather(buf_a, 0)
        start_gather(2 * i + 1, buf_b, 1)
        finish(buf_a, slice(0, tm))
        wait_gather(buf_b, 1)
        if gather_next:
            start_gather(2 * i + 2, buf_a, 0)
        finish(buf_b, slice(tm, 2 * tm))

    @pl.when(i == 0)
    def _():
        start_gather(0, buf_a, 0)

    pl.when(i + 1 < n)(functools.partial(step, True))
    pl.when(i + 1 == n)(functools.partial(step, False))


def _combine(dest, ys, wts, base, ga, g_final):
    t, d = base.shape
    tm = 64
    assert t % (2 * tm) == 0
    grid_spec = pltpu.PrefetchScalarGridSpec(
        num_scalar_prefetch=1,
        grid=(t // (2 * tm),),
        in_specs=[pl.BlockSpec(memory_space=pl.ANY),
                  pl.BlockSpec((2 * tm, TOP_K), lambda i, s: (i, 0)),
                  pl.BlockSpec((2 * tm, d), lambda i, s: (i, 0)),
                  pl.BlockSpec((1, d), lambda i, s: (0, 0)),
                  pl.BlockSpec((1, d), lambda i, s: (0, 0))],
        out_specs=pl.BlockSpec((None, 2 * tm, d), lambda i, s: (0, i, 0)),
        scratch_shapes=[pltpu.VMEM((TOP_K, tm, d // 2), U32), pltpu.VMEM((TOP_K, tm, d // 2), U32),
                        pltpu.SemaphoreType.DMA((2,))],
    )
    return pl.pallas_call(
        _combine_kernel,
        grid_spec=grid_spec,
        out_shape=jax.ShapeDtypeStruct((1, t, d), F32),
        compiler_params=_params("arbitrary"),
        name="combine",
    )(dest, ys, wts, base, ga, g_final)


def kernel(x, c, ctx, c_ctx, w_ada, b_ada, g_norm_mix, g_norm_ffn, w_in, w_gate_up, b_gate_up,
           g_gla_out, w_dw, b_dw, g_conv_ln, b_conv_ln, w_out, w_router, b_router,
           w_e_gate, w_e_up, w_e_down, w_s_gate, w_s_up, w_s_down, g_final):
    assert x.shape[0] == 1 and w_ada.shape[0] == 1, "single batch element, single layer"
    t, d = x.shape[1], x.shape[2]
    dk = w_gate_up.shape[3]
    heads = dk // GLA_HK
    dg = heads * GLA_HV
    dc = w_dw.shape[2]

    col_q, col_k, col_v = 0, dk, 2 * dk
    col_r = col_v + dg
    col_ca = col_r + dg
    col_cb = col_ca + dc
    n_gate = 2 * GATE_RANK
    w16 = w_in[0].astype(BF16)
    w_b = w16[:, col_ca + n_gate:]
    w_low = jnp.pad(w16[:, col_ca:col_ca + n_gate], ((0, 0), (0, LANES - n_gate)))

    mod = _ada(jnp.stack([c[0], c_ctx]), w_ada[0], b_ada[0])
    sh_m, sc_m, ga_m, sh_f, sc_f, ga_f = [mod[0:1, k * d:(k + 1) * d] for k in range(6)]
    csh_m, csc_m = mod[1:2, 0:d], mod[1:2, d:2 * d]

    wg_f = jnp.zeros((LANES, dk), F32).at[:GATE_RANK].set(w_gate_up[0, 0]).astype(BF16)
    wg_b = jnp.zeros((LANES, dk), F32).at[GATE_RANK:n_gate].set(w_gate_up[0, 1]).astype(BF16)
    bg_f = b_gate_up[0, 0:1]
    bg_b = b_gate_up[0, 1:2]
    cols = dict(col_q=col_q, col_k=col_k, col_v=col_v)

    gn_mix = g_norm_mix[0:1]
    proj_c, low_c = _proj(ctx, gn_mix, csh_m, csc_m, w_low, w16, col_ca, w_b)
    s0 = jnp.zeros((2, heads // 2, GLA_HV, 2 * GLA_HK), F32)
    s_ctx = _gla(proj_c, low_c, wg_f, wg_b, bg_f, bg_b, s0, with_output=False, **cols)

    proj_l, low_l = _proj(x, gn_mix, sh_m, sc_m, w_low, w16, col_ca, w_b)
    o_f, o_b = _gla(proj_l, low_l, wg_f, wg_b, bg_f, bg_b, s_ctx, with_output=True, **cols)
    mix = _mix(o_f, o_b, proj_l, g_gla_out[0:1], w_dw[0], b_dw[0:1], g_conv_ln[0:1], b_conv_ln[0:1],
               col_r, col_ca, col_cb)
    x1 = _outproj(mix, w_out[0].astype(BF16), x, ga_m)

    hp, eidx, wts, rank, cnt = _route(x1, g_norm_ffn[0:1], sh_f, sc_f, w_router[0].T,
                                      b_router[0].reshape(N_EXPERTS, 1))
    nk = t * TOP_K
    nb = (nk + MOE_BLOCK - 1) // MOE_BLOCK + N_EXPERTS
    counts = cnt[:, 0]
    pcounts = (counts + MOE_BLOCK - 1) // MOE_BLOCK * MOE_BLOCK
    pend = jnp.cumsum(pcounts)
    pstart = pend - pcounts
    experts = jnp.arange(N_EXPERTS, dtype=I32)
    pstart_tok = jnp.sum(jnp.where(eidx[:, :, None] == experts, pstart, 0), axis=-1)
    dest = (pstart_tok + rank).T.reshape(-1).astype(I32)
    tok_ids = jnp.repeat(jnp.arange(t, dtype=I32), TOP_K)
    slot_tok = jnp.zeros((nb * MOE_BLOCK,), I32).at[dest].set(tok_ids, unique_indices=True)
    blk_start = jnp.arange(nb, dtype=I32) * MOE_BLOCK
    blk_e = jnp.minimum(jnp.sum(pend[None, :] <= blk_start[:, None], axis=1), N_EXPERTS - 1).astype(I32)
    owns = counts > 0
    run_of_e = jnp.cumsum(owns.astype(I32)) - 1
    run_e = jnp.sum(jnp.where(owns[None, :] & (run_of_e[None, :] == experts[:, None]), experts[None, :], 0),
                    axis=1).astype(I32)
    onehot = blk_e[:, None] == experts[None, :]
    blk_run = jnp.sum(jnp.where(onehot, run_of_e[None, :], 0), axis=1).astype(I32)
    nruns = jnp.sum(owns.astype(I32))
    meta = jnp.stack([pend[-1] // MOE_BLOCK, nruns]).astype(I32)
    blk_first = jnp.sum(jnp.where(onehot, pstart[None, :], 0), axis=1) // MOE_BLOCK
    run_len = jnp.maximum(jnp.sum(jnp.where(onehot, pcounts[None, :], 0), axis=1) // MOE_BLOCK, 1)
    pos = jnp.arange(nb, dtype=I32) - blk_first
    share = (WUNITS * (pos + 1) + run_len - 1) // run_len
    blk_goal = jnp.minimum(WUNITS * (blk_run + 1) + share, WUNITS * nruns).astype(I32)

    ys = _experts(blk_run, blk_goal, run_e, meta, slot_tok, hp, w_e_gate[0], w_e_up[0], w_e_down[0])
    base = _shared(hp, w_s_gate[0].astype(BF16), w_s_up[0].astype(BF16), w_s_down[0].astype(BF16), x1, ga_f)
    return _combine(dest, ys, wts.T, base, ga_f, g_final.reshape(1, d))
```

```python
import functools

import jax
import jax.numpy as jnp
import numpy as np
from jax import lax
from jax.experimental import pallas as pl
from jax.experimental.pallas import tpu as pltpu

F32 = jnp.float32
BF16 = jnp.bfloat16
I32 = jnp.int32
U32 = jnp.uint32

EPS = 1e-6
LANES = 128
GLA_HK = 64
GLA_HV = 128
GLA_CHUNK = 64
GATE_RANK = 16
GATE_TAU = 16.0
CONV_WIDTH = 31
CONV_PAD = (CONV_WIDTH - 1) // 2
CONV_LEAD = 16
GRID_W = 64
N_EXPERTS = 64
N_GROUPS = 8
TOPK_GROUPS = 4
TOP_K = 8
ROUTED_SCALE = 2.5
MOE_BLOCK = 128
VMEM_LIMIT = 56 * 1024 * 1024

NT_DIMS = (((1,), (1,)), ((), ()))
TN_DIMS = (((0,), (0,)), ((), ()))


def _params(*sem):
    return pltpu.CompilerParams(dimension_semantics=sem, vmem_limit_bytes=VMEM_LIMIT)


def _tile(n, *preferred):
    for p in preferred:
        if n % p == 0:
            return p
    raise ValueError(f"no tile for {n} among {preferred}")


def _sigmoid(x):
    return 1.0 / (1.0 + jnp.exp(-x))


def _silu(x):
    return x * _sigmoid(x)


def _ada_kernel(cb_ref, w_ref, b_ref, o_ref, s_scr):
    @pl.when(pl.program_id(0) == 0)
    def _():
        s_scr[...] = _silu(cb_ref[...])

    d, tn = w_ref.shape
    for jt in range(tn // LANES):
        sl = slice(jt * LANES, (jt + 1) * LANES)
        w = w_ref[:, sl]
        for m in range(2):
            p = (w * s_scr[m]).reshape(d // 8, 8, LANES).sum(axis=0)
            o_ref[m:m + 1, sl] = p.sum(axis=0, keepdims=True) + b_ref[:, sl]


def _ada(c2, w_ada, b_ada):
    d, n = w_ada.shape
    tn = _tile(n, 1024, 512, LANES)
    cb = jnp.broadcast_to(c2[:, :, None], (2, d, LANES))
    return pl.pallas_call(
        _ada_kernel,
        grid=(n // tn,),
        in_specs=[pl.BlockSpec((2, d, LANES), lambda j: (0, 0, 0)),
                  pl.BlockSpec((d, tn), lambda j: (0, j)),
                  pl.BlockSpec((1, tn), lambda j: (0, j))],
        out_specs=pl.BlockSpec((2, tn), lambda j: (0, j)),
        out_shape=jax.ShapeDtypeStruct((2, n), F32),
        scratch_shapes=[pltpu.VMEM((2, d, LANES), F32)],
        compiler_params=_params("arbitrary"),
        name="ada",
    )(cb, w_ada, b_ada.reshape(1, n))


def _norm_mod(x, g, sh, sc):
    ms = jnp.mean(x * x, axis=-1, keepdims=True)
    y = x * lax.rsqrt(ms + EPS) * g
    return y * (1.0 + sc) + sh


def _proj_kernel(x_ref, g_ref, sh_ref, sc_ref, wl_ref, wa_ref, wb_ref, o_ref, low_ref, h_scr, *, na):
    j = pl.program_id(1)

    @pl.when(j == 0)
    def _():
        rows = min(256, h_scr.shape[0])
        for r in range(0, h_scr.shape[0], rows):
            h_scr[r:r + rows] = _norm_mod(x_ref[r:r + rows], g_ref[...], sh_ref[...], sc_ref[...]).astype(BF16)
        low_ref[...] = jnp.dot(h_scr[...], wl_ref[...], preferred_element_type=F32)

    @pl.when(j < na)
    def _():
        o_ref[...] = jnp.dot(h_scr[...], wa_ref[...], preferred_element_type=F32)

    @pl.when(j >= na)
    def _():
        o_ref[...] = jnp.dot(h_scr[...], wb_ref[...], preferred_element_type=F32)


def _proj(x, g, sh, sc, w_low, w, cols_a, wb):
    _, t, d = x.shape
    tm = _tile(t, 1024, 512, 256, 128)
    tn = _tile(cols_a, 512, 256)
    na, nb = cols_a // tn, wb.shape[1] // tn
    assert wb.shape[1] % tn == 0
    vec = pl.BlockSpec((1, d), lambda i, j: (0, 0))
    return pl.pallas_call(
        functools.partial(_proj_kernel, na=na),
        grid=(t // tm, na + nb),
        in_specs=[pl.BlockSpec((None, tm, d), lambda i, j: (0, i, 0), pipeline_mode=pl.Buffered(1)),
                  vec, vec, vec,
                  pl.BlockSpec((d, LANES), lambda i, j: (0, 0)),
                  pl.BlockSpec((d, tn), lambda i, j: (0, jnp.minimum(j, na - 1))),
                  pl.BlockSpec((d, tn), lambda i, j: (0, jnp.maximum(j - na, 0)))],
        out_specs=[pl.BlockSpec((tm, tn), lambda i, j: (i, j)),
                   pl.BlockSpec((tm, LANES), lambda i, j: (i, 0))],
        out_shape=[jax.ShapeDtypeStruct((t, (na + nb) * tn), F32),
                   jax.ShapeDtypeStruct((t, LANES), F32)],
        scratch_shapes=[pltpu.VMEM((tm, d), BF16)],
        compiler_params=_params("arbitrary", "arbitrary"),
        name="proj",
    )(x, g, sh, sc, w_low, w, wb)


def _gla_block(q_ref, k_ref, v_ref, low_ref, wg, bg, tri, s_t, o_ref, fwd, with_output):
    c = GLA_CHUNK
    nch = q_ref.shape[0] // c
    z = jnp.dot(low_ref[...].astype(BF16), wg, preferred_element_type=F32) + bg
    g = (jnp.minimum(z, 0.0) - jnp.log(1.0 + jnp.exp(-jnp.abs(z)))) * (1.0 / GATE_TAU)
    g1 = g.astype(BF16)
    r1 = g - g1.astype(F32)
    g2 = r1.astype(BF16)
    g3 = (r1 - g2.astype(F32)).astype(BF16)
    b3 = jnp.dot(tri, jnp.concatenate([g1, g2, g3], axis=1), preferred_element_type=F32)
    b = b3[:, :LANES] + b3[:, LANES:2 * LANES] + b3[:, 2 * LANES:]

    def chunk_rows(n):
        return slice(n * c, (n + 1) * c)

    def per_chunk(row):
        return jnp.concatenate([jnp.broadcast_to(b[n * c + row:n * c + row + 1], (c, LANES))
                                for n in range(nch)], axis=0)

    tot = per_chunk(c - 1 if fwd else 0)
    lane0 = lax.broadcasted_iota(I32, (c, LANES), 1) < GLA_HK

    def stack(a, n):
        an = a[chunk_rows(n)]
        return jnp.concatenate([jnp.where(lane0, an, 0.0), jnp.where(lane0, 0.0, an)], axis=0).astype(BF16)

    k = k_ref[...]
    ks = k * jnp.exp(tot - b)
    vs = [jnp.concatenate([v_ref[chunk_rows(n), :GLA_HV], v_ref[chunk_rows(n), GLA_HV:]],
                          axis=0).astype(BF16) for n in range(nch)]
    upd = [lax.dot_general(vs[n], stack(ks, n), TN_DIMS, preferred_element_type=F32) for n in range(nch)]
    decay = jnp.exp(tot)
    order = range(nch) if fwd else range(nch - 1, -1, -1)
    s_in = [None] * nch
    for n in order:
        s_in[n] = s_t
        s_t = s_t * decay[n * c:n * c + 1] + upd[n]
    if not with_output:
        return s_t
    mid = per_chunk(c // 2)
    qs = q_ref[...] * (GLA_HK ** -0.5)
    qa = qs * jnp.exp(b - mid)
    ka = k * jnp.exp(mid - b)
    qi = qs * jnp.exp(b)
    r2 = lax.broadcasted_iota(I32, (2 * c, 2 * c), 0)
    c2 = lax.broadcasted_iota(I32, (2 * c, 2 * c), 1)
    keep = ((r2 >= c) == (c2 >= c)) & ((c2 <= r2) if fwd else (c2 >= r2))
    for n in range(nch):
        att = lax.dot_general(stack(qa, n), stack(ka, n), NT_DIMS, preferred_element_type=F32)
        att = jnp.where(keep, att, 0.0).astype(BF16)
        o = jnp.dot(att, vs[n], preferred_element_type=F32)
        o = o + lax.dot_general(stack(qi, n), s_in[n].astype(BF16), NT_DIMS, preferred_element_type=F32)
        o_ref[chunk_rows(n)] = jnp.concatenate([o[:c], o[c:]], axis=1)
    return s_t


def _gla_kernel(*refs, with_output):
    (qf, kf, vf, lf, qb, kb, vb, lb, wgf, wgb, bgf, bgb, s0_ref) = refs[:13]
    if with_output:
        of_ref, ob_ref, s_scr, tri_scr = refs[13:]
    else:
        sout_ref, s_scr, tri_scr = refs[13:]
        of_ref = ob_ref = None
    i = pl.program_id(1)
    c = GLA_CHUNK
    rows = qf.shape[0]

    @pl.when((pl.program_id(0) == 0) & (i == 0))
    def _():
        r = lax.broadcasted_iota(I32, (rows, rows), 0)
        cc = lax.broadcasted_iota(I32, (rows, rows), 1)
        shift = c.bit_length() - 1
        same = jnp.right_shift(r, shift) == jnp.right_shift(cc, shift)
        tri_scr[0] = jnp.where(same & (cc <= r), 1.0, 0.0).astype(BF16)
        tri_scr[1] = jnp.where(same & (cc >= r), 1.0, 0.0).astype(BF16)

    @pl.when(i == 0)
    def _():
        s_scr[...] = s0_ref[:, 0]

    s_scr[0] = _gla_block(qf, kf, vf, lf, wgf[...], bgf[...], tri_scr[0], s_scr[0], of_ref, True, with_output)
    s_scr[1] = _gla_block(qb, kb, vb, lb, wgb[...], bgb[...], tri_scr[1], s_scr[1], ob_ref, False, with_output)
    if not with_output:
        @pl.when(i == pl.num_programs(1) - 1)
        def _():
            sout_ref[:, 0] = s_scr[...]


def _gla(proj, low, wg_f, wg_b, bg_f, bg_b, s0, with_output, col_q, col_k, col_v):
    t = proj.shape[0]
    cb = _tile(t // GLA_CHUNK, 8, 4, 2, 1)
    rows = cb * GLA_CHUNK
    nb = t // rows
    npairs = wg_f.shape[1] // (2 * GLA_HK)
    pw = 2 * GLA_HK
    vw = 2 * GLA_HV
    fi = lambda p, i: i
    bi = lambda p, i: nb - 1 - i

    def spec(width, col0, blk):
        return pl.BlockSpec((rows, width), lambda p, i: (blk(p, i), col0 // width + p))

    def low_spec(blk):
        return pl.BlockSpec((rows, LANES), lambda p, i: (blk(p, i), 0))

    in_specs = [spec(pw, col_q, fi), spec(pw, col_k, fi), spec(vw, col_v, fi), low_spec(fi),
                spec(pw, col_q, bi), spec(pw, col_k, bi), spec(vw, col_v, bi), low_spec(bi),
                pl.BlockSpec((LANES, pw), lambda p, i: (0, p)),
                pl.BlockSpec((LANES, pw), lambda p, i: (0, p)),
                pl.BlockSpec((1, pw), lambda p, i: (0, p)),
                pl.BlockSpec((1, pw), lambda p, i: (0, p)),
                pl.BlockSpec((2, 1, GLA_HV, pw), lambda p, i: (0, p, 0, 0))]
    if with_output:
        out_specs = [pl.BlockSpec((rows, vw), lambda p, i: (i, p)),
                     pl.BlockSpec((rows, vw), lambda p, i: (nb - 1 - i, p))]
        out_shape = [jax.ShapeDtypeStruct((t, npairs * 2 * GLA_HV), F32)] * 2
    else:
        out_specs = pl.BlockSpec((2, 1, GLA_HV, pw), lambda p, i: (0, p, 0, 0))
        out_shape = jax.ShapeDtypeStruct((2, npairs, GLA_HV, pw), F32)
    return pl.pallas_call(
        functools.partial(_gla_kernel, with_output=with_output),
        grid=(npairs, nb),
        in_specs=in_specs,
        out_specs=out_specs,
        out_shape=out_shape,
        scratch_shapes=[pltpu.VMEM((2, GLA_HV, pw), F32), pltpu.VMEM((2, rows, rows), BF16)],
        compiler_params=_params("arbitrary", "arbitrary"),
        name="gla_out" if with_output else "gla_state",
    )(proj, proj, proj, low, proj, proj, proj, low, wg_f, wg_b, bg_f, bg_b, s0)


def _mix_kernel(of_ref, ob_ref, r_ref, ca_ref, cb_ref, gout_ref, wdw_ref, bdw_ref, gln_ref, bln_ref, pick_ref,
                o_ref, upad, y_scr):
    tm = of_ref.shape[0]
    dg = of_ref.shape[1]
    dc = ca_ref.shape[1]
    nrow = tm // GRID_W
    lead = CONV_LEAD
    for h in range(dg // GLA_HV):
        sl = slice(h * GLA_HV, (h + 1) * GLA_HV)
        o = of_ref[:, sl] + ob_ref[:, sl]
        o = o * lax.rsqrt(jnp.mean(o * o, axis=-1, keepdims=True) + EPS) * gout_ref[:, sl]
        o_ref[:, sl] = (o * _silu(r_ref[:, sl])).astype(o_ref.dtype)
    for r in range(nrow):
        rs = slice(r * GRID_W, (r + 1) * GRID_W)
        upad[r, 0:lead, :] = jnp.zeros((lead, dc), F32)
        upad[r, lead:lead + GRID_W, :] = ca_ref[rs, :] * _sigmoid(cb_ref[rs, :])
        upad[r, lead + GRID_W:, :] = jnp.zeros((lead, dc), F32)
    cw = 256
    span = upad.shape[1]
    kpad = pick_ref.shape[1] - CONV_WIDTH * span

    def row_body(r, carry):
        for cc in range(dc // cw):
            cs = slice(cc * cw, (cc + 1) * cw)
            u = upad[r, :, cs]
            taps = [(wdw_ref[j:j + 1, cs] * u).astype(BF16) for j in range(CONV_WIDTH)]
            taps.append(jnp.zeros((kpad, cw), BF16))
            y_scr[r, :, cs] = (jnp.dot(pick_ref[...], jnp.concatenate(taps, axis=0), preferred_element_type=F32)
                               + bdw_ref[:, cs])
        return carry

    lax.fori_loop(0, nrow, row_body, 0)
    for r in range(nrow):
        y = y_scr[r]
        mu = jnp.mean(y, axis=-1, keepdims=True)
        yc = y - mu
        var = jnp.mean(yc * yc, axis=-1, keepdims=True)
        yn = yc * lax.rsqrt(var + EPS) * gln_ref[...] + bln_ref[...]
        o_ref[r * GRID_W:(r + 1) * GRID_W, dg:] = _silu(yn).astype(o_ref.dtype)


def _mix(o_f, o_b, proj, g_out, w_dw, b_dw, g_ln, b_ln, col_r, col_ca, col_cb):
    t, dg = o_f.shape
    tm = _tile(t, 256, 128, GRID_W)
    dc = w_dw.shape[1]
    nrow = tm // GRID_W
    span = GRID_W + 2 * CONV_LEAD
    kdim = -(-CONV_WIDTH * span // LANES) * LANES
    pick = np.zeros((GRID_W, kdim), np.float32)
    for j in range(CONV_WIDTH):
        pick[np.arange(GRID_W), j * span + np.arange(GRID_W) + (CONV_LEAD - CONV_PAD + j)] = 1.0
    pick = jnp.asarray(pick, BF16)
    row = lambda width: pl.BlockSpec((1, width), lambda i: (0, 0))
    return pl.pallas_call(
        _mix_kernel,
        grid=(t // tm,),
        in_specs=[pl.BlockSpec((tm, dg), lambda i: (i, 0)),
                  pl.BlockSpec((tm, dg), lambda i: (i, 0)),
                  pl.BlockSpec((tm, dg), lambda i: (i, col_r // dg)),
                  pl.BlockSpec((tm, dc), lambda i: (i, col_ca // dc)),
                  pl.BlockSpec((tm, dc), lambda i: (i, col_cb // dc)),
                  row(dg),
                  pl.BlockSpec((CONV_WIDTH, dc), lambda i: (0, 0)),
                  row(dc), row(dc), row(dc),
                  pl.BlockSpec((GRID_W, kdim), lambda i: (0, 0))],
        out_specs=pl.BlockSpec((tm, dg + dc), lambda i: (i, 0)),
        out_shape=jax.ShapeDtypeStruct((t, dg + dc), BF16),
        scratch_shapes=[pltpu.VMEM((nrow, span, dc), F32),
                        pltpu.VMEM((nrow, GRID_W, dc), F32)],
        compiler_params=_params("arbitrary"),
        name="mix",
    )(o_f, o_b, proj, proj, proj, g_out, w_dw, b_dw, g_ln, b_ln, pick)


def _outproj_kernel(m_ref, w_ref, x_ref, ga_ref, o_ref):
    o_ref[...] = x_ref[...] + ga_ref[...] * jnp.dot(m_ref[...], w_ref[...], preferred_element_type=F32)


def _outproj(mix, w, x, ga):
    t, k = mix.shape
    n = w.shape[1]
    tm = _tile(t, 1024, 512, 256)
    tn = _tile(n, 512, 256, LANES)
    return pl.pallas_call(
        _outproj_kernel,
        grid=(t // tm, n // tn),
        in_specs=[pl.BlockSpec((tm, k), lambda i, j: (i, 0)),
                  pl.BlockSpec((k, tn), lambda i, j: (0, j)),
                  pl.BlockSpec((None, tm, tn), lambda i, j: (0, i, j)),
                  pl.BlockSpec((1, tn), lambda i, j: (0, j))],
        out_specs=pl.BlockSpec((tm, tn), lambda i, j: (i, j)),
        out_shape=jax.ShapeDtypeStruct((t, n), F32),
        compiler_params=_params("arbitrary", "arbitrary"),
        name="out_proj",
    )(mix, w, x, ga)


def _pack_halves(h):
    c = h.shape[1] // 2
    lo = lax.bitcast_convert_type(h[:, :c].astype(BF16).astype(F32), U32)
    hi = lax.bitcast_convert_type(h[:, c:].astype(BF16).astype(F32), U32)
    return (lo >> 16) | (hi & jnp.uint32(0xFFFF0000))


def _unpack_halves(p):
    lo = lax.bitcast_convert_type(p << 16, F32).astype(BF16)
    hi = lax.bitcast_convert_type(p & jnp.uint32(0xFFFF0000), F32).astype(BF16)
    return jnp.concatenate([lo, hi], axis=1)


def _route_kernel(x_ref, g_ref, sh_ref, sc_ref, wr_ref, br_ref, hp_ref, e_ref, w_ref, rk_ref, cnt_ref,
                  carry):
    tm = x_ref.shape[0]
    ne = N_EXPERTS
    gs = ne // N_GROUPS
    neg = -jnp.inf

    @pl.when(pl.program_id(0) == 0)
    def _():
        carry[...] = jnp.zeros_like(carry)

    h = _norm_mod(x_ref[...], g_ref[...], sh_ref[...], sc_ref[...])
    hp_ref[...] = _pack_halves(h)
    logits = lax.dot_general(wr_ref[...], h, NT_DIMS, preferred_element_type=F32,
                             precision=lax.Precision.HIGHEST)
    scores = _sigmoid(logits)
    sel = scores + br_ref[...]
    sel3 = sel.reshape(N_GROUPS, gs, tm)
    sub = lax.broadcasted_iota(I32, (N_GROUPS, gs, tm), 1)
    m1 = jnp.max(sel3, axis=1, keepdims=True)
    i1 = jnp.min(jnp.where(sel3 == m1, sub, gs), axis=1, keepdims=True)
    m2 = jnp.max(jnp.where(sub == i1, neg, sel3), axis=1, keepdims=True)
    gscore = (m1 + m2).reshape(N_GROUPS, tm)
    gid = lax.broadcasted_iota(I32, (N_GROUPS, tm), 0)
    gmask = jnp.zeros((N_GROUPS, tm), F32)
    for _ in range(TOPK_GROUPS):
        mx = jnp.max(gscore, axis=0, keepdims=True)
        idx = jnp.min(jnp.where(gscore == mx, gid, N_GROUPS), axis=0, keepdims=True)
        pick = gid == idx
        gmask = jnp.where(pick, 1.0, gmask)
        gscore = jnp.where(pick, neg, gscore)
    emask = jnp.broadcast_to(gmask.reshape(N_GROUPS, 1, tm), (N_GROUPS, gs, tm)).reshape(ne, tm)
    cand = jnp.where(emask > 0.0, sel, neg)
    eid = lax.broadcasted_iota(I32, (ne, tm), 0)
    chosen = jnp.zeros((ne, tm), F32)
    idxs, scs = [], []
    for _ in range(TOP_K):
        mx = jnp.max(cand, axis=0, keepdims=True)
        idx = jnp.min(jnp.where(cand == mx, eid, ne), axis=0, keepdims=True)
        pick = eid == idx
        idxs.append(idx)
        scs.append(jnp.sum(jnp.where(pick, scores, 0.0), axis=0, keepdims=True))
        chosen = jnp.where(pick, 1.0, chosen)
        cand = jnp.where(pick, neg, cand)
    ssum = scs[0]
    for s in scs[1:]:
        ssum = ssum + s
    before = (lax.broadcasted_iota(I32, (tm, tm), 0) < lax.broadcasted_iota(I32, (tm, tm), 1))
    prior = jnp.dot(chosen.astype(BF16), before.astype(BF16), preferred_element_type=F32) + carry[...]
    for k in range(TOP_K):
        e_ref[k:k + 1, :] = idxs[k]
        w_ref[k:k + 1, :] = scs[k] / ssum * ROUTED_SCALE
        rk = jnp.sum(jnp.where(eid == idxs[k], prior, 0.0), axis=0, keepdims=True)
        rk_ref[k:k + 1, :] = rk.astype(I32)
    carry[...] = carry[...] + jnp.sum(chosen, axis=1, keepdims=True)
    cnt_ref[...] = jnp.broadcast_to(carry[...], cnt_ref.shape).astype(I32)


def _route(x, g, sh, sc, w_router_t, b_router):
    t, d = x.shape
    tm = _tile(t, 256, LANES)
    vec = pl.BlockSpec((1, d), lambda i: (0, 0))
    tok = lambda dt: jax.ShapeDtypeStruct((TOP_K, t), dt)
    tok_spec = pl.BlockSpec((TOP_K, tm), lambda i: (0, i))
    return pl.pallas_call(
        _route_kernel,
        grid=(t // tm,),
        in_specs=[pl.BlockSpec((tm, d), lambda i: (i, 0)), vec, vec, vec,
                  pl.BlockSpec((N_EXPERTS, d), lambda i: (0, 0)),
                  pl.BlockSpec((N_EXPERTS, 1), lambda i: (0, 0))],
        out_specs=[pl.BlockSpec((tm, d // 2), lambda i: (i, 0)), tok_spec, tok_spec, tok_spec,
                   pl.BlockSpec((N_EXPERTS, LANES), lambda i: (0, 0))],
        out_shape=[jax.ShapeDtypeStruct((t, d // 2), U32), tok(I32), tok(F32), tok(I32),
                   jax.ShapeDtypeStruct((N_EXPERTS, LANES), I32)],
        scratch_shapes=[pltpu.VMEM((N_EXPERTS, 1), F32)],
        compiler_params=_params("arbitrary"),
        name="route",
    )(x, g, sh, sc, w_router_t, b_router)


WUNITS = 8
WSLOTS = 3


def _experts_kernel(blk_run, blk_goal, run_e, meta, slot_tok, hp_hbm, wg_hbm, wu_hbm, wd_hbm, o_ref,
                    xbuf, sem, sg, su, sd, wsem, wgb, wub, wdb, done_ref):
    b = pl.program_id(0)
    nb = pl.num_programs(0)
    nused = meta[0]
    nunits = meta[1] * WUNITS
    ug = sg.shape[1]
    ud = sd.shape[1]

    def unit_copies(u):
        e = run_e[u // WUNITS]
        k = u % WUNITS
        s = u % WSLOTS
        return (pltpu.make_async_copy(wg_hbm.at[e, pl.ds(k * ug, ug)], sg.at[s], wsem.at[s]),
                pltpu.make_async_copy(wu_hbm.at[e, pl.ds(k * ug, ug)], su.at[s], wsem.at[s]),
                pltpu.make_async_copy(wd_hbm.at[e, pl.ds(k * ud, ud)], sd.at[s], wsem.at[s]))

    def start_unit(u):
        for cp in unit_copies(u):
            cp.start()

    def process_unit(u):
        for cp in unit_copies(u):
            cp.wait()

        @pl.when(u + WSLOTS - 1 < nunits)
        def _():
            start_unit(u + WSLOTS - 1)

        r = (u // WUNITS) % 2
        k = u % WUNITS
        s = u % WSLOTS
        wgb[r, pl.ds(pl.multiple_of(k * ug, ug), ug), :] = sg[s].astype(BF16)
        wub[r, pl.ds(pl.multiple_of(k * ug, ug), ug), :] = su[s].astype(BF16)
        wdb[r, pl.ds(pl.multiple_of(k * ud, ud), ud), :] = sd[s].astype(BF16)

    def start_rows(blk, slot, rows):
        for r in rows:
            tok = slot_tok[blk * MOE_BLOCK + r]
            pltpu.make_async_copy(hp_hbm.at[pl.ds(tok, 1)], xbuf.at[slot, pl.ds(r, 1)],
                                  sem.at[slot]).start(priority=r % 2)

    def wait_gather(slot):
        pltpu.make_async_copy(hp_hbm.at[pl.ds(0, MOE_BLOCK)], xbuf.at[slot], sem.at[slot]).wait()

    live = b < nused
    next_live = live & (b + 1 < nb) & (b + 1 < nused)

    @pl.when((b == 0) & live)
    def _():
        done_ref[0] = 0
        for u in range(WSLOTS - 1):
            @pl.when(u < nunits)
            def _():
                start_unit(u)
        start_rows(0, 0, range(MOE_BLOCK))

    @pl.when(live)
    def _():
        done = done_ref[0]
        goal = blk_goal[b]

        def body(n, carry):
            process_unit(done + n)
            return carry
        lax.fori_loop(0, jnp.maximum(goal - done, 0), body, 0)
        done_ref[0] = jnp.maximum(goal, done)

    def compute(slot, gather_next):
        r = blk_run[b] % 2
        half = xbuf.shape[2]
        group = MOE_BLOCK // 4

        def issue(q):
            if gather_next:
                start_rows(b + 1, 1 - slot, range(q * group, (q + 1) * group))

        wait_gather(slot)
        x = _unpack_halves(xbuf[slot])
        issue(0)
        hg = jnp.dot(x, wgb[r], preferred_element_type=F32)
        issue(1)
        hu = jnp.dot(x, wub[r], preferred_element_type=F32)
        issue(2)
        hb = (_silu(hg) * hu).astype(BF16)
        for c in range(2):
            lo = slice(c * half // 2, (c + 1) * half // 2)
            hi = slice(half + c * half // 2, half + (c + 1) * half // 2)
            o = jnp.concatenate([jnp.dot(hb, wdb[r, :, lo], preferred_element_type=F32),
                                 jnp.dot(hb, wdb[r, :, hi], preferred_element_type=F32)], axis=1)
            o_ref[:, lo] = _pack_halves(o)
            if c == 0:
                issue(3)

    for slot in range(2):
        mine = live & (b % 2 == slot)
        pl.when(mine & next_live)(functools.partial(compute, slot, True))
        pl.when(mine & jnp.logical_not(next_live))(functools.partial(compute, slot, False))

    @pl.when(jnp.logical_not(live))
    def _():
        o_ref[...] = jnp.zeros_like(o_ref)


def _experts(blk_run, blk_goal, run_e, meta, slot_tok, hp, wg, wu, wd):
    nb = blk_run.shape[0]
    _, d, de = wg.shape
    ug, ud = d // WUNITS, de // WUNITS
    hbm = pl.BlockSpec(memory_space=pl.ANY)
    grid_spec = pltpu.PrefetchScalarGridSpec(
        num_scalar_prefetch=5,
        grid=(nb,),
        in_specs=[hbm, hbm, hbm, hbm],
        out_specs=pl.BlockSpec((MOE_BLOCK, d // 2), lambda b, *_: (b, 0)),
        scratch_shapes=[pltpu.VMEM((2, MOE_BLOCK, d // 2), U32), pltpu.SemaphoreType.DMA((2,)),
                        pltpu.VMEM((WSLOTS, ug, de), F32), pltpu.VMEM((WSLOTS, ug, de), F32),
                        pltpu.VMEM((WSLOTS, ud, d), F32), pltpu.SemaphoreType.DMA((WSLOTS,)),
                        pltpu.VMEM((2, d, de), BF16), pltpu.VMEM((2, d, de), BF16),
                        pltpu.VMEM((2, de, d), BF16), pltpu.SMEM((1,), I32)],
    )
    return pl.pallas_call(
        _experts_kernel,
        grid_spec=grid_spec,
        out_shape=jax.ShapeDtypeStruct((nb * MOE_BLOCK, d // 2), U32),
        compiler_params=_params("arbitrary"),
        name="experts",
    )(blk_run, blk_goal, run_e, meta, slot_tok, hp, wg, wu, wd)


def _shared_kernel(hp_ref, wg_ref, wu_ref, wd_ref, x_ref, ga_ref, o_ref):
    x = _unpack_halves(hp_ref[...])
    hg = jnp.dot(x, wg_ref[...], preferred_element_type=F32)
    hu = jnp.dot(x, wu_ref[...], preferred_element_type=F32)
    hb = (_silu(hg) * hu).astype(BF16)
    o_ref[...] = x_ref[...] + ga_ref[...] * jnp.dot(hb, wd_ref[...], preferred_element_type=F32)


def _shared(hp, wg, wu, wd, x, ga):
    t, d = x.shape
    tm = _tile(t, 256, LANES)
    ds_ = wg.shape[1]
    return pl.pallas_call(
        _shared_kernel,
        grid=(t // tm,),
        in_specs=[pl.BlockSpec((tm, d // 2), lambda i: (i, 0)),
                  pl.BlockSpec((d, ds_), lambda i: (0, 0)),
                  pl.BlockSpec((d, ds_), lambda i: (0, 0)),
                  pl.BlockSpec((ds_, d), lambda i: (0, 0)),
                  pl.BlockSpec((tm, d), lambda i: (i, 0)),
                  pl.BlockSpec((1, d), lambda i: (0, 0))],
        out_specs=pl.BlockSpec((tm, d), lambda i: (i, 0)),
        out_shape=jax.ShapeDtypeStruct((t, d), F32),
        compiler_params=_params("arbitrary"),
        name="shared",
    )(hp, wg, wu, wd, x, ga)


def _combine_kernel(dest, ys_hbm, w_ref, base_ref, ga_ref, gf_ref, o_ref, buf_a, buf_b, sem):
    i = pl.program_id(0)
    n = pl.num_programs(0)
    tm = buf_a.shape[1]
    half = buf_a.shape[2]
    d = 2 * half

    def start_gather(tile, buf, s):
        for j in range(tm * TOP_K):
            src = dest[tile * (tm * TOP_K) + j]
            pltpu.make_async_copy(ys_hbm.at[pl.ds(src, 1)], buf.at[j % TOP_K, pl.ds(j // TOP_K, 1)],
                                  sem.at[s]).start(priority=j % 2)

    def wait_gather(buf, s):
        for k in range(TOP_K):
            pltpu.make_async_copy(ys_hbm.at[pl.ds(0, tm)], buf.at[k], sem.at[s]).wait()

    def finish(buf, rows):
        ylo = jnp.zeros((tm, half), F32)
        yhi = jnp.zeros((tm, half), F32)
        for k in range(TOP_K):
            p = buf[k]
            w = w_ref[rows, k:k + 1]
            ylo = ylo + w * lax.bitcast_convert_type(p << 16, F32)
            yhi = yhi + w * lax.bitcast_convert_type(p & jnp.uint32(0xFFFF0000), F32)
        xlo = base_ref[rows, :half] + ga_ref[:, :half] * ylo
        xhi = base_ref[rows, half:] + ga_ref[:, half:] * yhi
        ms = (jnp.sum(xlo * xlo, axis=-1, keepdims=True) + jnp.sum(xhi * xhi, axis=-1, keepdims=True)) / d
        inv = lax.rsqrt(ms + EPS)
        o_ref[rows, :half] = xlo * inv * gf_ref[:, :half]
        o_ref[rows, half:] = xhi * inv * gf_ref[:, half:]

    def step(gather_next):
        wait_gather(buf_a, 0)
        start_gather(2 * i + 1, buf_b, 1)
        finish(buf_a, slice(0, tm))
        wait_gather(buf_b, 1)
        if gather_next:
            start_gather(2 * i + 2, buf_a, 0)
        finish(buf_b, slice(tm, 2 * tm))

    @pl.when(i == 0)
    def _():
        start_gather(0, buf_a, 0)

    pl.when(i + 1 < n)(functools.partial(step, True))
    pl.when(i + 1 == n)(functools.partial(step, False))


def _combine(dest, ys, wts, base, ga, g_final):
    t, d = base.shape
    tm = 64
    assert t % (2 * tm) == 0
    grid_spec = pltpu.PrefetchScalarGridSpec(
        num_scalar_prefetch=1,
        grid=(t // (2 * tm),),
        in_specs=[pl.BlockSpec(memory_space=pl.ANY),
                  pl.BlockSpec((2 * tm, TOP_K), lambda i, s: (i, 0)),
                  pl.BlockSpec((2 * tm, d), lambda i, s: (i, 0)),
                  pl.BlockSpec((1, d), lambda i, s: (0, 0)),
                  pl.BlockSpec((1, d), lambda i, s: (0, 0))],
        out_specs=pl.BlockSpec((None, 2 * tm, d), lambda i, s: (0, i, 0)),
        scratch_shapes=[pltpu.VMEM((TOP_K, tm, d // 2), U32), pltpu.VMEM((TOP_K, tm, d // 2), U32),
                        pltpu.SemaphoreType.DMA((2,))],
    )
    return pl.pallas_call(
        _combine_kernel,
        grid_spec=grid_spec,
        out_shape=jax.ShapeDtypeStruct((1, t, d), F32),
        compiler_params=_params("arbitrary"),
        name="combine",
    )(dest, ys, wts, base, ga, g_final)


def kernel(x, c, ctx, c_ctx, w_ada, b_ada, g_norm_mix, g_norm_ffn, w_in, w_gate_up, b_gate_up,
           g_gla_out, w_dw, b_dw, g_conv_ln, b_conv_ln, w_out, w_router, b_router,
           w_e_gate, w_e_up, w_e_down, w_s_gate, w_s_up, w_s_down, g_final):
    assert x.shape[0] == 1 and w_ada.shape[0] == 1, "single batch element, single layer"
    t, d = x.shape[1], x.shape[2]
    dk = w_gate_up.shape[3]
    heads = dk // GLA_HK
    dg = heads * GLA_HV
    dc = w_dw.shape[2]

    col_q, col_k, col_v = 0, dk, 2 * dk
    col_r = col_v + dg
    col_ca = col_r + dg
    col_cb = col_ca + dc
    n_gate = 2 * GATE_RANK
    w16 = w_in[0].astype(BF16)
    w_b = w16[:, col_ca + n_gate:]
    w_low = jnp.pad(w16[:, col_ca:col_ca + n_gate], ((0, 0), (0, LANES - n_gate)))

    mod = _ada(jnp.stack([c[0], c_ctx]), w_ada[0], b_ada[0])
    sh_m, sc_m, ga_m, sh_f, sc_f, ga_f = [mod[0:1, k * d:(k + 1) * d] for k in range(6)]
    csh_m, csc_m = mod[1:2, 0:d], mod[1:2, d:2 * d]

    wg_f = jnp.zeros((LANES, dk), F32).at[:GATE_RANK].set(w_gate_up[0, 0]).astype(BF16)
    wg_b = jnp.zeros((LANES, dk), F32).at[GATE_RANK:n_gate].set(w_gate_up[0, 1]).astype(BF16)
    bg_f = b_gate_up[0, 0:1]
    bg_b = b_gate_up[0, 1:2]
    cols = dict(col_q=col_q, col_k=col_k, col_v=col_v)

    gn_mix = g_norm_mix[0:1]
    proj_c, low_c = _proj(ctx, gn_mix, csh_m, csc_m, w_low, w16, col_ca, w_b)
    s0 = jnp.zeros((2, heads // 2, GLA_HV, 2 * GLA_HK), F32)
    s_ctx = _gla(proj_c, low_c, wg_f, wg_b, bg_f, bg_b, s0, with_output=False, **cols)

    proj_l, low_l = _proj(x, gn_mix, sh_m, sc_m, w_low, w16, col_ca, w_b)
    o_f, o_b = _gla(proj_l, low_l, wg_f, wg_b, bg_f, bg_b, s_ctx, with_output=True, **cols)
    mix = _mix(o_f, o_b, proj_l, g_gla_out[0:1], w_dw[0], b_dw[0:1], g_conv_ln[0:1], b_conv_ln[0:1],
               col_r, col_ca, col_cb)
    x1 = _outproj(mix, w_out[0].astype(BF16), x, ga_m)

    hp, eidx, wts, rank, cnt = _route(x1, g_norm_ffn[0:1], sh_f, sc_f, w_router[0].T,
                                      b_router[0].reshape(N_EXPERTS, 1))
    nk = t * TOP_K
    nb = (nk + MOE_BLOCK - 1) // MOE_BLOCK + N_EXPERTS
    counts = cnt[:, 0]
    pcounts = (counts + MOE_BLOCK - 1) // MOE_BLOCK * MOE_BLOCK
    pend = jnp.cumsum(pcounts)
    pstart = pend - pcounts
    experts = jnp.arange(N_EXPERTS, dtype=I32)
    pstart_tok = jnp.sum(jnp.where(eidx[:, :, None] == experts, pstart, 0), axis=-1)
    dest = (pstart_tok + rank).T.reshape(-1).astype(I32)
    tok_ids = jnp.repeat(jnp.arange(t, dtype=I32), TOP_K)
    slot_tok = jnp.zeros((nb * MOE_BLOCK,), I32).at[dest].set(tok_ids, unique_indices=True,
                                                               mode="promise_in_bounds")
    blk_start = jnp.arange(nb, dtype=I32) * MOE_BLOCK
    blk_e = jnp.minimum(jnp.sum(pend[None, :] <= blk_start[:, None], axis=1), N_EXPERTS - 1).astype(I32)
    owns = counts > 0
    run_of_e = jnp.cumsum(owns.astype(I32)) - 1
    run_e = jnp.sum(jnp.where(owns[None, :] & (run_of_e[None, :] == experts[:, None]), experts[None, :], 0),
                    axis=1).astype(I32)
    onehot = blk_e[:, None] == experts[None, :]
    blk_run = jnp.sum(jnp.where(onehot, run_of_e[None, :], 0), axis=1).astype(I32)
    nruns = jnp.sum(owns.astype(I32))
    meta = jnp.stack([pend[-1] // MOE_BLOCK, nruns]).astype(I32)
    blk_first = jnp.sum(jnp.where(onehot, pstart[None, :], 0), axis=1) // MOE_BLOCK
    run_len = jnp.maximum(jnp.sum(jnp.where(onehot, pcounts[None, :], 0), axis=1) // MOE_BLOCK, 1)
    pos = jnp.arange(nb, dtype=I32) - blk_first
    share = (WUNITS * (pos + 1) + run_len - 1) // run_len
    blk_goal = jnp.minimum(WUNITS * (blk_run + 1) + share, WUNITS * nruns).astype(I32)

    ys = _experts(blk_run, blk_goal, run_e, meta, slot_tok, hp, w_e_gate[0], w_e_up[0], w_e_down[0])
    base = _shared(hp, w_s_gate[0].astype(BF16), w_s_up[0].astype(BF16), w_s_down[0].astype(BF16), x1, ga_f)
    return _combine(dest, ys, wts.T, base, ga_f, g_final.reshape(1, d))
```

```python
import functools

import jax
import jax.numpy as jnp
import numpy as np
from jax import lax
from jax.experimental import pallas as pl
from jax.experimental.pallas import tpu as pltpu

F32 = jnp.float32
BF16 = jnp.bfloat16
I32 = jnp.int32
U32 = jnp.uint32

EPS = 1e-6
LANES = 128
GLA_HK = 64
GLA_HV = 128
GLA_CHUNK = 64
GATE_RANK = 16
GATE_TAU = 16.0
CONV_WIDTH = 31
CONV_PAD = (CONV_WIDTH - 1) // 2
CONV_LEAD = 16
GRID_W = 64
N_EXPERTS = 64
N_GROUPS = 8
TOPK_GROUPS = 4
TOP_K = 8
ROUTED_SCALE = 2.5
MOE_BLOCK = 128
VMEM_LIMIT = 56 * 1024 * 1024

NT_DIMS = (((1,), (1,)), ((), ()))
TN_DIMS = (((0,), (0,)), ((), ()))


def _params(*sem):
    return pltpu.CompilerParams(dimension_semantics=sem, vmem_limit_bytes=VMEM_LIMIT)


def _tile(n, *preferred):
    for p in preferred:
        if n % p == 0:
            return p
    raise ValueError(f"no tile for {n} among {preferred}")


def _sigmoid(x):
    return 1.0 / (1.0 + jnp.exp(-x))


def _silu(x):
    return x * _sigmoid(x)


def _ada_kernel(cb_ref, w_ref, b_ref, o_ref, s_scr):
    @pl.when(pl.program_id(0) == 0)
    def _():
        s_scr[...] = _silu(cb_ref[...])

    d, tn = w_ref.shape
    for jt in range(tn // LANES):
        sl = slice(jt * LANES, (jt + 1) * LANES)
        w = w_ref[:, sl]
        for m in range(2):
            p = (w * s_scr[m]).reshape(d // 8, 8, LANES).sum(axis=0)
            o_ref[m:m + 1, sl] = p.sum(axis=0, keepdims=True) + b_ref[:, sl]


def _ada(c2, w_ada, b_ada):
    d, n = w_ada.shape
    tn = _tile(n, 1024, 512, LANES)
    cb = jnp.broadcast_to(c2[:, :, None], (2, d, LANES))
    return pl.pallas_call(
        _ada_kernel,
        grid=(n // tn,),
        in_specs=[pl.BlockSpec((2, d, LANES), lambda j: (0, 0, 0)),
                  pl.BlockSpec((d, tn), lambda j: (0, j)),
                  pl.BlockSpec((1, tn), lambda j: (0, j))],
        out_specs=pl.BlockSpec((2, tn), lambda j: (0, j)),
        out_shape=jax.ShapeDtypeStruct((2, n), F32),
        scratch_shapes=[pltpu.VMEM((2, d, LANES), F32)],
        compiler_params=_params("arbitrary"),
        name="ada",
    )(cb, w_ada, b_ada.reshape(1, n))


def _norm_mod(x, g, sh, sc):
    ms = jnp.mean(x * x, axis=-1, keepdims=True)
    y = x * lax.rsqrt(ms + EPS) * g
    return y * (1.0 + sc) + sh


def _proj_kernel(x_ref, g_ref, sh_ref, sc_ref, wl_ref, wa_ref, wb_ref, o_ref, low_ref, h_scr, *, na):
    j = pl.program_id(1)

    @pl.when(j == 0)
    def _():
        rows = min(256, h_scr.shape[0])
        for r in range(0, h_scr.shape[0], rows):
            h_scr[r:r + rows] = _norm_mod(x_ref[r:r + rows], g_ref[...], sh_ref[...], sc_ref[...]).astype(BF16)
        low_ref[...] = jnp.dot(h_scr[...], wl_ref[...], preferred_element_type=F32)

    @pl.when(j < na)
    def _():
        o_ref[...] = jnp.dot(h_scr[...], wa_ref[...], preferred_element_type=F32)

    @pl.when(j >= na)
    def _():
        o_ref[...] = jnp.dot(h_scr[...], wb_ref[...], preferred_element_type=F32)


def _proj(x, g, sh, sc, w_low, w, cols_a, wb):
    _, t, d = x.shape
    tm = _tile(t, 1024, 512, 256, 128)
    tn = _tile(cols_a, 512, 256)
    na, nb = cols_a // tn, wb.shape[1] // tn
    assert wb.shape[1] % tn == 0
    vec = pl.BlockSpec((1, d), lambda i, j: (0, 0))
    return pl.pallas_call(
        functools.partial(_proj_kernel, na=na),
        grid=(t // tm, na + nb),
        in_specs=[pl.BlockSpec((None, tm, d), lambda i, j: (0, i, 0), pipeline_mode=pl.Buffered(1)),
                  vec, vec, vec,
                  pl.BlockSpec((d, LANES), lambda i, j: (0, 0)),
                  pl.BlockSpec((d, tn), lambda i, j: (0, jnp.minimum(j, na - 1))),
                  pl.BlockSpec((d, tn), lambda i, j: (0, jnp.maximum(j - na, 0)))],
        out_specs=[pl.BlockSpec((tm, tn), lambda i, j: (i, j)),
                   pl.BlockSpec((tm, LANES), lambda i, j: (i, 0))],
        out_shape=[jax.ShapeDtypeStruct((t, (na + nb) * tn), F32),
                   jax.ShapeDtypeStruct((t, LANES), F32)],
        scratch_shapes=[pltpu.VMEM((tm, d), BF16)],
        compiler_params=_params("arbitrary", "arbitrary"),
        name="proj",
    )(x, g, sh, sc, w_low, w, wb)


def _gla_block(q_ref, k_ref, v_ref, low_ref, wg, bg, tri, s_t, o_ref, fwd, with_output):
    c = GLA_CHUNK
    nch = q_ref.shape[0] // c
    z = jnp.dot(low_ref[...].astype(BF16), wg, preferred_element_type=F32) + bg
    g = (jnp.minimum(z, 0.0) - jnp.log(1.0 + jnp.exp(-jnp.abs(z)))) * (1.0 / GATE_TAU)
    g1 = g.astype(BF16)
    r1 = g - g1.astype(F32)
    g2 = r1.astype(BF16)
    g3 = (r1 - g2.astype(F32)).astype(BF16)
    b3 = jnp.dot(tri, jnp.concatenate([g1, g2, g3], axis=1), preferred_element_type=F32)
    b = b3[:, :LANES] + b3[:, LANES:2 * LANES] + b3[:, 2 * LANES:]

    def chunk_rows(n):
        return slice(n * c, (n + 1) * c)

    def per_chunk(row):
        return jnp.concatenate([jnp.broadcast_to(b[n * c + row:n * c + row + 1], (c, LANES))
                                for n in range(nch)], axis=0)

    tot = per_chunk(c - 1 if fwd else 0)
    lane0 = lax.broadcasted_iota(I32, (c, LANES), 1) < GLA_HK

    def stack(a, n):
        an = a[chunk_rows(n)]
        return jnp.concatenate([jnp.where(lane0, an, 0.0), jnp.where(lane0, 0.0, an)], axis=0).astype(BF16)

    k = k_ref[...]
    ks = k * jnp.exp(tot - b)
    vs = [jnp.concatenate([v_ref[chunk_rows(n), :GLA_HV], v_ref[chunk_rows(n), GLA_HV:]],
                          axis=0).astype(BF16) for n in range(nch)]
    upd = [lax.dot_general(vs[n], stack(ks, n), TN_DIMS, preferred_element_type=F32) for n in range(nch)]
    decay = jnp.exp(tot)
    order = range(nch) if fwd else range(nch - 1, -1, -1)
    s_in = [None] * nch
    for n in order:
        s_in[n] = s_t
        s_t = s_t * decay[n * c:n * c + 1] + upd[n]
    if not with_output:
        return s_t
    mid = per_chunk(c // 2)
    qs = q_ref[...] * (GLA_HK ** -0.5)
    qa = qs * jnp.exp(b - mid)
    ka = k * jnp.exp(mid - b)
    qi = qs * jnp.exp(b)
    r2 = lax.broadcasted_iota(I32, (2 * c, 2 * c), 0)
    c2 = lax.broadcasted_iota(I32, (2 * c, 2 * c), 1)
    keep = ((r2 >= c) == (c2 >= c)) & ((c2 <= r2) if fwd else (c2 >= r2))
    for n in range(nch):
        att = lax.dot_general(stack(qa, n), stack(ka, n), NT_DIMS, preferred_element_type=F32)
        att = jnp.where(keep, att, 0.0).astype(BF16)
        o = jnp.dot(att, vs[n], preferred_element_type=F32)
        o = o + lax.dot_general(stack(qi, n), s_in[n].astype(BF16), NT_DIMS, preferred_element_type=F32)
        o_ref[chunk_rows(n)] = jnp.concatenate([o[:c], o[c:]], axis=1)
    return s_t


def _gla_kernel(*refs, with_output):
    (qf, kf, vf, lf, qb, kb, vb, lb, wgf, wgb, bgf, bgb, s0_ref) = refs[:13]
    if with_output:
        of_ref, ob_ref, s_scr, tri_scr = refs[13:]
    else:
        sout_ref, s_scr, tri_scr = refs[13:]
        of_ref = ob_ref = None
    i = pl.program_id(1)
    c = GLA_CHUNK
    rows = qf.shape[0]

    @pl.when((pl.program_id(0) == 0) & (i == 0))
    def _():
        r = lax.broadcasted_iota(I32, (rows, rows), 0)
        cc = lax.broadcasted_iota(I32, (rows, rows), 1)
        shift = c.bit_length() - 1
        same = jnp.right_shift(r, shift) == jnp.right_shift(cc, shift)
        tri_scr[0] = jnp.where(same & (cc <= r), 1.0, 0.0).astype(BF16)
        tri_scr[1] = jnp.where(same & (cc >= r), 1.0, 0.0).astype(BF16)

    @pl.when(i == 0)
    def _():
        s_scr[...] = s0_ref[:, 0]

    s_scr[0] = _gla_block(qf, kf, vf, lf, wgf[...], bgf[...], tri_scr[0], s_scr[0], of_ref, True, with_output)
    s_scr[1] = _gla_block(qb, kb, vb, lb, wgb[...], bgb[...], tri_scr[1], s_scr[1], ob_ref, False, with_output)
    if not with_output:
        @pl.when(i == pl.num_programs(1) - 1)
        def _():
            sout_ref[:, 0] = s_scr[...]


def _gla(proj, low, wg_f, wg_b, bg_f, bg_b, s0, with_output, col_q, col_k, col_v):
    t = proj.shape[0]
    cb = _tile(t // GLA_CHUNK, 8, 4, 2, 1)
    rows = cb * GLA_CHUNK
    nb = t // rows
    npairs = wg_f.shape[1] // (2 * GLA_HK)
    pw = 2 * GLA_HK
    vw = 2 * GLA_HV
    fi = lambda p, i: i
    bi = lambda p, i: nb - 1 - i

    def spec(width, col0, blk):
        return pl.BlockSpec((rows, width), lambda p, i: (blk(p, i), col0 // width + p))

    def low_spec(blk):
        return pl.BlockSpec((rows, LANES), lambda p, i: (blk(p, i), 0))

    in_specs = [spec(pw, col_q, fi), spec(pw, col_k, fi), spec(vw, col_v, fi), low_spec(fi),
                spec(pw, col_q, bi), spec(pw, col_k, bi), spec(vw, col_v, bi), low_spec(bi),
                pl.BlockSpec((LANES, pw), lambda p, i: (0, p)),
                pl.BlockSpec((LANES, pw), lambda p, i: (0, p)),
                pl.BlockSpec((1, pw), lambda p, i: (0, p)),
                pl.BlockSpec((1, pw), lambda p, i: (0, p)),
                pl.BlockSpec((2, 1, GLA_HV, pw), lambda p, i: (0, p, 0, 0))]
    if with_output:
        out_specs = [pl.BlockSpec((rows, vw), lambda p, i: (i, p)),
                     pl.BlockSpec((rows, vw), lambda p, i: (nb - 1 - i, p))]
        out_shape = [jax.ShapeDtypeStruct((t, npairs * 2 * GLA_HV), F32)] * 2
    else:
        out_specs = pl.BlockSpec((2, 1, GLA_HV, pw), lambda p, i: (0, p, 0, 0))
        out_shape = jax.ShapeDtypeStruct((2, npairs, GLA_HV, pw), F32)
    return pl.pallas_call(
        functools.partial(_gla_kernel, with_output=with_output),
        grid=(npairs, nb),
        in_specs=in_specs,
        out_specs=out_specs,
        out_shape=out_shape,
        scratch_shapes=[pltpu.VMEM((2, GLA_HV, pw), F32), pltpu.VMEM((2, rows, rows), BF16)],
        compiler_params=_params("arbitrary", "arbitrary"),
        name="gla_out" if with_output else "gla_state",
    )(proj, proj, proj, low, proj, proj, proj, low, wg_f, wg_b, bg_f, bg_b, s0)


def _mix_kernel(of_ref, ob_ref, r_ref, ca_ref, cb_ref, gout_ref, wdw_ref, bdw_ref, gln_ref, bln_ref, pick_ref,
                o_ref, upad, y_scr):
    tm = of_ref.shape[0]
    dg = of_ref.shape[1]
    dc = ca_ref.shape[1]
    nrow = tm // GRID_W
    lead = CONV_LEAD
    for h in range(dg // GLA_HV):
        sl = slice(h * GLA_HV, (h + 1) * GLA_HV)
        o = of_ref[:, sl] + ob_ref[:, sl]
        o = o * lax.rsqrt(jnp.mean(o * o, axis=-1, keepdims=True) + EPS) * gout_ref[:, sl]
        o_ref[:, sl] = (o * _silu(r_ref[:, sl])).astype(o_ref.dtype)
    for r in range(nrow):
        rs = slice(r * GRID_W, (r + 1) * GRID_W)
        upad[r, 0:lead, :] = jnp.zeros((lead, dc), F32)
        upad[r, lead:lead + GRID_W, :] = ca_ref[rs, :] * _sigmoid(cb_ref[rs, :])
        upad[r, lead + GRID_W:, :] = jnp.zeros((lead, dc), F32)
    cw = 256
    span = upad.shape[1]
    kpad = pick_ref.shape[1] - CONV_WIDTH * span

    def row_body(r, carry):
        for cc in range(dc // cw):
            cs = slice(cc * cw, (cc + 1) * cw)
            u = upad[r, :, cs]
            taps = [(wdw_ref[j:j + 1, cs] * u).astype(BF16) for j in range(CONV_WIDTH)]
            taps.append(jnp.zeros((kpad, cw), BF16))
            y_scr[r, :, cs] = (jnp.dot(pick_ref[...], jnp.concatenate(taps, axis=0), preferred_element_type=F32)
                               + bdw_ref[:, cs])
        return carry

    lax.fori_loop(0, nrow, row_body, 0)
    for r in range(nrow):
        y = y_scr[r]
        mu = jnp.mean(y, axis=-1, keepdims=True)
        yc = y - mu
        var = jnp.mean(yc * yc, axis=-1, keepdims=True)
        yn = yc * lax.rsqrt(var + EPS) * gln_ref[...] + bln_ref[...]
        o_ref[r * GRID_W:(r + 1) * GRID_W, dg:] = _silu(yn).astype(o_ref.dtype)


def _mix(o_f, o_b, proj, g_out, w_dw, b_dw, g_ln, b_ln, col_r, col_ca, col_cb):
    t, dg = o_f.shape
    tm = _tile(t, 256, 128, GRID_W)
    dc = w_dw.shape[1]
    nrow = tm // GRID_W
    span = GRID_W + 2 * CONV_LEAD
    kdim = -(-CONV_WIDTH * span // LANES) * LANES
    pick = np.zeros((GRID_W, kdim), np.float32)
    for j in range(CONV_WIDTH):
        pick[np.arange(GRID_W), j * span + np.arange(GRID_W) + (CONV_LEAD - CONV_PAD + j)] = 1.0
    pick = jnp.asarray(pick, BF16)
    row = lambda width: pl.BlockSpec((1, width), lambda i: (0, 0))
    return pl.pallas_call(
        _mix_kernel,
        grid=(t // tm,),
        in_specs=[pl.BlockSpec((tm, dg), lambda i: (i, 0)),
                  pl.BlockSpec((tm, dg), lambda i: (i, 0)),
                  pl.BlockSpec((tm, dg), lambda i: (i, col_r // dg)),
                  pl.BlockSpec((tm, dc), lambda i: (i, col_ca // dc)),
                  pl.BlockSpec((tm, dc), lambda i: (i, col_cb // dc)),
                  row(dg),
                  pl.BlockSpec((CONV_WIDTH, dc), lambda i: (0, 0)),
                  row(dc), row(dc), row(dc),
                  pl.BlockSpec((GRID_W, kdim), lambda i: (0, 0))],
        out_specs=pl.BlockSpec((tm, dg + dc), lambda i: (i, 0)),
        out_shape=jax.ShapeDtypeStruct((t, dg + dc), BF16),
        scratch_shapes=[pltpu.VMEM((nrow, span, dc), F32),
                        pltpu.VMEM((nrow, GRID_W, dc), F32)],
        compiler_params=_params("arbitrary"),
        name="mix",
    )(o_f, o_b, proj, proj, proj, g_out, w_dw, b_dw, g_ln, b_ln, pick)


def _outproj_kernel(m_ref, w_ref, x_ref, ga_ref, o_ref):
    o_ref[...] = x_ref[...] + ga_ref[...] * jnp.dot(m_ref[...], w_ref[...], preferred_element_type=F32)


def _outproj(mix, w, x, ga):
    t, k = mix.shape
    n = w.shape[1]
    tm = _tile(t, 1024, 512, 256)
    tn = _tile(n, 512, 256, LANES)
    return pl.pallas_call(
        _outproj_kernel,
        grid=(t // tm, n // tn),
        in_specs=[pl.BlockSpec((tm, k), lambda i, j: (i, 0)),
                  pl.BlockSpec((k, tn), lambda i, j: (0, j)),
                  pl.BlockSpec((None, tm, tn), lambda i, j: (0, i, j)),
                  pl.BlockSpec((1, tn), lambda i, j: (0, j))],
        out_specs=pl.BlockSpec((tm, tn), lambda i, j: (i, j)),
        out_shape=jax.ShapeDtypeStruct((t, n), F32),
        compiler_params=_params("arbitrary", "arbitrary"),
        name="out_proj",
    )(mix, w, x, ga)


def _pack_halves(h):
    c = h.shape[1] // 2
    lo = lax.bitcast_convert_type(h[:, :c].astype(BF16).astype(F32), U32)
    hi = lax.bitcast_convert_type(h[:, c:].astype(BF16).astype(F32), U32)
    return (lo >> 16) | (hi & jnp.uint32(0xFFFF0000))


def _unpack_halves(p):
    lo = lax.bitcast_convert_type(p << 16, F32).astype(BF16)
    hi = lax.bitcast_convert_type(p & jnp.uint32(0xFFFF0000), F32).astype(BF16)
    return jnp.concatenate([lo, hi], axis=1)


def _route_kernel(x_ref, g_ref, sh_ref, sc_ref, wr_ref, br_ref, hp_ref, e_ref, w_ref, rk_ref, cnt_ref,
                  carry):
    tm = x_ref.shape[0]
    ne = N_EXPERTS
    gs = ne // N_GROUPS
    neg = -jnp.inf

    @pl.when(pl.program_id(0) == 0)
    def _():
        carry[...] = jnp.zeros_like(carry)

    h = _norm_mod(x_ref[...], g_ref[...], sh_ref[...], sc_ref[...])
    hp_ref[...] = _pack_halves(h)
    logits = lax.dot_general(wr_ref[...], h, NT_DIMS, preferred_element_type=F32,
                             precision=lax.Precision.HIGHEST)
    scores = _sigmoid(logits)
    sel = scores + br_ref[...]
    sel3 = sel.reshape(N_GROUPS, gs, tm)
    sub = lax.broadcasted_iota(I32, (N_GROUPS, gs, tm), 1)
    m1 = jnp.max(sel3, axis=1, keepdims=True)
    i1 = jnp.min(jnp.where(sel3 == m1, sub, gs), axis=1, keepdims=True)
    m2 = jnp.max(jnp.where(sub == i1, neg, sel3), axis=1, keepdims=True)
    gscore = (m1 + m2).reshape(N_GROUPS, tm)
    gid = lax.broadcasted_iota(I32, (N_GROUPS, tm), 0)
    gmask = jnp.zeros((N_GROUPS, tm), F32)
    for _ in range(TOPK_GROUPS):
        mx = jnp.max(gscore, axis=0, keepdims=True)
        idx = jnp.min(jnp.where(gscore == mx, gid, N_GROUPS), axis=0, keepdims=True)
        pick = gid == idx
        gmask = jnp.where(pick, 1.0, gmask)
        gscore = jnp.where(pick, neg, gscore)
    emask = jnp.broadcast_to(gmask.reshape(N_GROUPS, 1, tm), (N_GROUPS, gs, tm)).reshape(ne, tm)
    cand = jnp.where(emask > 0.0, sel, neg)
    eid = lax.broadcasted_iota(I32, (ne, tm), 0)
    chosen = jnp.zeros((ne, tm), F32)
    idxs, scs = [], []
    for _ in range(TOP_K):
        mx = jnp.max(cand, axis=0, keepdims=True)
        idx = jnp.min(jnp.where(cand == mx, eid, ne), axis=0, keepdims=True)
        pick = eid == idx
        idxs.append(idx)
        scs.append(jnp.sum(jnp.where(pick, scores, 0.0), axis=0, keepdims=True))
        chosen = jnp.where(pick, 1.0, chosen)
        cand = jnp.where(pick, neg, cand)
    ssum = scs[0]
    for s in scs[1:]:
        ssum = ssum + s
    before = (lax.broadcasted_iota(I32, (tm, tm), 0) < lax.broadcasted_iota(I32, (tm, tm), 1))
    prior = jnp.dot(chosen.astype(BF16), before.astype(BF16), preferred_element_type=F32) + carry[...]
    for k in range(TOP_K):
        e_ref[k:k + 1, :] = idxs[k]
        w_ref[k:k + 1, :] = scs[k] / ssum * ROUTED_SCALE
        rk = jnp.sum(jnp.where(eid == idxs[k], prior, 0.0), axis=0, keepdims=True)
        rk_ref[k:k + 1, :] = rk.astype(I32)
    carry[...] = carry[...] + jnp.sum(chosen, axis=1, keepdims=True)
    cnt_ref[...] = jnp.broadcast_to(carry[...], cnt_ref.shape).astype(I32)


def _route(x, g, sh, sc, w_router_t, b_router):
    t, d = x.shape
    tm = _tile(t, 256, LANES)
    vec = pl.BlockSpec((1, d), lambda i: (0, 0))
    tok = lambda dt: jax.ShapeDtypeStruct((TOP_K, t), dt)
    tok_spec = pl.BlockSpec((TOP_K, tm), lambda i: (0, i))
    return pl.pallas_call(
        _route_kernel,
        grid=(t // tm,),
        in_specs=[pl.BlockSpec((tm, d), lambda i: (i, 0)), vec, vec, vec,
                  pl.BlockSpec((N_EXPERTS, d), lambda i: (0, 0)),
                  pl.BlockSpec((N_EXPERTS, 1), lambda i: (0, 0))],
        out_specs=[pl.BlockSpec((tm, d // 2), lambda i: (i, 0)), tok_spec, tok_spec, tok_spec,
                   pl.BlockSpec((N_EXPERTS, LANES), lambda i: (0, 0))],
        out_shape=[jax.ShapeDtypeStruct((t, d // 2), U32), tok(I32), tok(F32), tok(I32),
                   jax.ShapeDtypeStruct((N_EXPERTS, LANES), I32)],
        scratch_shapes=[pltpu.VMEM((N_EXPERTS, 1), F32)],
        compiler_params=_params("arbitrary"),
        name="route",
    )(x, g, sh, sc, w_router_t, b_router)


WUNITS = 8
WSLOTS = 3
XSLOTS = 3


def _experts_kernel(blk_run, blk_goal, run_e, meta, slot_tok, hp_hbm, wg_hbm, wu_hbm, wd_hbm, o_ref,
                    xbuf, sem, sg, su, sd, wsem, wgb, wub, wdb, done_ref):
    b = pl.program_id(0)
    nb = pl.num_programs(0)
    nused = meta[0]
    nunits = meta[1] * WUNITS
    ug = sg.shape[1]
    ud = sd.shape[1]

    def unit_copies(u):
        e = run_e[u // WUNITS]
        k = u % WUNITS
        s = u % WSLOTS
        return (pltpu.make_async_copy(wg_hbm.at[e, pl.ds(k * ug, ug)], sg.at[s], wsem.at[s]),
                pltpu.make_async_copy(wu_hbm.at[e, pl.ds(k * ug, ug)], su.at[s], wsem.at[s]),
                pltpu.make_async_copy(wd_hbm.at[e, pl.ds(k * ud, ud)], sd.at[s], wsem.at[s]))

    def start_unit(u):
        for cp in unit_copies(u):
            cp.start(priority=1)

    def process_unit(u):
        for cp in unit_copies(u):
            cp.wait()

        @pl.when(u + WSLOTS - 1 < nunits)
        def _():
            start_unit(u + WSLOTS - 1)

        r = (u // WUNITS) % 2
        k = u % WUNITS
        s = u % WSLOTS
        wgb[r, pl.ds(pl.multiple_of(k * ug, ug), ug), :] = sg[s].astype(BF16)
        wub[r, pl.ds(pl.multiple_of(k * ug, ug), ug), :] = su[s].astype(BF16)
        wdb[r, pl.ds(pl.multiple_of(k * ud, ud), ud), :] = sd[s].astype(BF16)

    def start_gather(blk, slot):
        for r in range(MOE_BLOCK):
            tok = slot_tok[blk * MOE_BLOCK + r]
            pltpu.make_async_copy(hp_hbm.at[pl.ds(tok, 1)], xbuf.at[slot, pl.ds(r, 1)],
                                  sem.at[slot]).start(priority=0)

    def wait_gather(slot):
        pltpu.make_async_copy(hp_hbm.at[pl.ds(0, MOE_BLOCK)], xbuf.at[slot], sem.at[slot]).wait()

    live = b < nused
    slot = lax.rem(b, XSLOTS)

    @pl.when((b == 0) & live)
    def _():
        done_ref[0] = 0
        for u in range(WSLOTS - 1):
            @pl.when(u < nunits)
            def _():
                start_unit(u)
        for blk in range(XSLOTS - 1):
            @pl.when(blk < nused)
            def _():
                start_gather(blk, blk)

    ahead = b + XSLOTS - 1
    for m in range(XSLOTS):
        @pl.when(live & (ahead < nb) & (ahead < nused) & (slot == m))
        def _():
            start_gather(ahead, (m + XSLOTS - 1) % XSLOTS)

    @pl.when(live)
    def _():
        done = done_ref[0]
        goal = blk_goal[b]

        def body(n, carry):
            process_unit(done + n)
            return carry
        lax.fori_loop(0, jnp.maximum(goal - done, 0), body, 0)
        done_ref[0] = jnp.maximum(goal, done)

        r = blk_run[b] % 2
        wait_gather(slot)
        x = _unpack_halves(xbuf[slot])
        hg = jnp.dot(x, wgb[r], preferred_element_type=F32)
        hu = jnp.dot(x, wub[r], preferred_element_type=F32)
        hb = (_silu(hg) * hu).astype(BF16)
        o_ref[...] = _pack_halves(jnp.dot(hb, wdb[r], preferred_element_type=F32))

    @pl.when(jnp.logical_not(live))
    def _():
        o_ref[...] = jnp.zeros_like(o_ref)


def _experts(blk_run, blk_goal, run_e, meta, slot_tok, hp, wg, wu, wd):
    nb = blk_run.shape[0]
    _, d, de = wg.shape
    ug, ud = d // WUNITS, de // WUNITS
    hbm = pl.BlockSpec(memory_space=pl.ANY)
    grid_spec = pltpu.PrefetchScalarGridSpec(
        num_scalar_prefetch=5,
        grid=(nb,),
        in_specs=[hbm, hbm, hbm, hbm],
        out_specs=pl.BlockSpec((MOE_BLOCK, d // 2), lambda b, *_: (b, 0)),
        scratch_shapes=[pltpu.VMEM((XSLOTS, MOE_BLOCK, d // 2), U32), pltpu.SemaphoreType.DMA((XSLOTS,)),
                        pltpu.VMEM((WSLOTS, ug, de), F32), pltpu.VMEM((WSLOTS, ug, de), F32),
                        pltpu.VMEM((WSLOTS, ud, d), F32), pltpu.SemaphoreType.DMA((WSLOTS,)),
                        pltpu.VMEM((2, d, de), BF16), pltpu.VMEM((2, d, de), BF16),
                        pltpu.VMEM((2, de, d), BF16), pltpu.SMEM((1,), I32)],
    )
    return pl.pallas_call(
        _experts_kernel,
        grid_spec=grid_spec,
        out_shape=jax.ShapeDtypeStruct((nb * MOE_BLOCK, d // 2), U32),
        compiler_params=_params("arbitrary"),
        name="experts",
    )(blk_run, blk_goal, run_e, meta, slot_tok, hp, wg, wu, wd)


def _shared_kernel(hp_ref, wg_ref, wu_ref, wd_ref, x_ref, ga_ref, o_ref):
    x = _unpack_halves(hp_ref[...])
    hg = jnp.dot(x, wg_ref[...], preferred_element_type=F32)
    hu = jnp.dot(x, wu_ref[...], preferred_element_type=F32)
    hb = (_silu(hg) * hu).astype(BF16)
    o_ref[...] = x_ref[...] + ga_ref[...] * jnp.dot(hb, wd_ref[...], preferred_element_type=F32)


def _shared(hp, wg, wu, wd, x, ga):
    t, d = x.shape
    tm = _tile(t, 256, LANES)
    ds_ = wg.shape[1]
    return pl.pallas_call(
        _shared_kernel,
        grid=(t // tm,),
        in_specs=[pl.BlockSpec((tm, d // 2), lambda i: (i, 0)),
                  pl.BlockSpec((d, ds_), lambda i: (0, 0)),
                  pl.BlockSpec((d, ds_), lambda i: (0, 0)),
                  pl.BlockSpec((ds_, d), lambda i: (0, 0)),
                  pl.BlockSpec((tm, d), lambda i: (i, 0)),
                  pl.BlockSpec((1, d), lambda i: (0, 0))],
        out_specs=pl.BlockSpec((tm, d), lambda i: (i, 0)),
        out_shape=jax.ShapeDtypeStruct((t, d), F32),
        compiler_params=_params("arbitrary"),
        name="shared",
    )(hp, wg, wu, wd, x, ga)


def _combine_kernel(dest, ys_hbm, w_ref, base_ref, ga_ref, gf_ref, o_ref, buf_a, buf_b, sem):
    i = pl.program_id(0)
    n = pl.num_programs(0)
    tm = buf_a.shape[1]
    half = buf_a.shape[2]
    d = 2 * half

    def start_gather(tile, buf, s):
        for j in range(tm * TOP_K):
            src = dest[tile * (tm * TOP_K) + j]
            pltpu.make_async_copy(ys_hbm.at[pl.ds(src, 1)], buf.at[j % TOP_K, pl.ds(j // TOP_K, 1)],
                                  sem.at[s]).start(priority=j % 2)

    def wait_gather(buf, s):
        for k in range(TOP_K):
            pltpu.make_async_copy(ys_hbm.at[pl.ds(0, tm)], buf.at[k], sem.at[s]).wait()

    def finish(buf, rows):
        ylo = jnp.zeros((tm, half), F32)
        yhi = jnp.zeros((tm, half), F32)
        for k in range(TOP_K):
            p = buf[k]
            w = w_ref[rows, k:k + 1]
            ylo = ylo + w * lax.bitcast_convert_type(p << 16, F32)
            yhi = yhi + w * lax.bitcast_convert_type(p & jnp.uint32(0xFFFF0000), F32)
        xlo = base_ref[rows, :half] + ga_ref[:, :half] * ylo
        xhi = base_ref[rows, half:] + ga_ref[:, half:] * yhi
        ms = (jnp.sum(xlo * xlo, axis=-1, keepdims=True) + jnp.sum(xhi * xhi, axis=-1, keepdims=True)) / d
        inv = lax.rsqrt(ms + EPS)
        o_ref[rows, :half] = xlo * inv * gf_ref[:, :half]
        o_ref[rows, half:] = xhi * inv * gf_ref[:, half:]

    def step(gather_next):
        wait_gather(buf_a, 0)
        start_gather(2 * i + 1, buf_b, 1)
        finish(buf_a, slice(0, tm))
        wait_gather(buf_b, 1)
        if gather_next:
            start_gather(2 * i + 2, buf_a, 0)
        finish(buf_b, slice(tm, 2 * tm))

    @pl.when(i == 0)
    def _():
        start_gather(0, buf_a, 0)

    pl.when(i + 1 < n)(functools.partial(step, True))
    pl.when(i + 1 == n)(functools.partial(step, False))


def _combine(dest, ys, wts, base, ga, g_final):
    t, d = base.shape
    tm = 64
    assert t % (2 * tm) == 0
    grid_spec = pltpu.PrefetchScalarGridSpec(
        num_scalar_prefetch=1,
        grid=(t // (2 * tm),),
        in_specs=[pl.BlockSpec(memory_space=pl.ANY),
                  pl.BlockSpec((2 * tm, TOP_K), lambda i, s: (i, 0)),
                  pl.BlockSpec((2 * tm, d), lambda i, s: (i, 0)),
                  pl.BlockSpec((1, d), lambda i, s: (0, 0)),
                  pl.BlockSpec((1, d), lambda i, s: (0, 0))],
        out_specs=pl.BlockSpec((None, 2 * tm, d), lambda i, s: (0, i, 0)),
        scratch_shapes=[pltpu.VMEM((TOP_K, tm, d // 2), U32), pltpu.VMEM((TOP_K, tm, d // 2), U32),
                        pltpu.SemaphoreType.DMA((2,))],
    )
    return pl.pallas_call(
        _combine_kernel,
        grid_spec=grid_spec,
        out_shape=jax.ShapeDtypeStruct((1, t, d), F32),
        compiler_params=_params("arbitrary"),
        name="combine",
    )(dest, ys, wts, base, ga, g_final)


def kernel(x, c, ctx, c_ctx, w_ada, b_ada, g_norm_mix, g_norm_ffn, w_in, w_gate_up, b_gate_up,
           g_gla_out, w_dw, b_dw, g_conv_ln, b_conv_ln, w_out, w_router, b_router,
           w_e_gate, w_e_up, w_e_down, w_s_gate, w_s_up, w_s_down, g_final):
    assert x.shape[0] == 1 and w_ada.shape[0] == 1, "single batch element, single layer"
    t, d = x.shape[1], x.shape[2]
    dk = w_gate_up.shape[3]
    heads = dk // GLA_HK
    dg = heads * GLA_HV
    dc = w_dw.shape[2]

    col_q, col_k, col_v = 0, dk, 2 * dk
    col_r = col_v + dg
    col_ca = col_r + dg
    col_cb = col_ca + dc
    n_gate = 2 * GATE_RANK
    w16 = w_in[0].astype(BF16)
    w_b = w16[:, col_ca + n_gate:]
    w_low = jnp.pad(w16[:, col_ca:col_ca + n_gate], ((0, 0), (0, LANES - n_gate)))

    mod = _ada(jnp.stack([c[0], c_ctx]), w_ada[0], b_ada[0])
    sh_m, sc_m, ga_m, sh_f, sc_f, ga_f = [mod[0:1, k * d:(k + 1) * d] for k in range(6)]
    csh_m, csc_m = mod[1:2, 0:d], mod[1:2, d:2 * d]

    wg_f = jnp.zeros((LANES, dk), F32).at[:GATE_RANK].set(w_gate_up[0, 0]).astype(BF16)
    wg_b = jnp.zeros((LANES, dk), F32).at[GATE_RANK:n_gate].set(w_gate_up[0, 1]).astype(BF16)
    bg_f = b_gate_up[0, 0:1]
    bg_b = b_gate_up[0, 1:2]
    cols = dict(col_q=col_q, col_k=col_k, col_v=col_v)

    gn_mix = g_norm_mix[0:1]
    proj_c, low_c = _proj(ctx, gn_mix, csh_m, csc_m, w_low, w16, col_ca, w_b)
    s0 = jnp.zeros((2, heads // 2, GLA_HV, 2 * GLA_HK), F32)
    s_ctx = _gla(proj_c, low_c, wg_f, wg_b, bg_f, bg_b, s0, with_output=False, **cols)

    proj_l, low_l = _proj(x, gn_mix, sh_m, sc_m, w_low, w16, col_ca, w_b)
    o_f, o_b = _gla(proj_l, low_l, wg_f, wg_b, bg_f, bg_b, s_ctx, with_output=True, **cols)
    mix = _mix(o_f, o_b, proj_l, g_gla_out[0:1], w_dw[0], b_dw[0:1], g_conv_ln[0:1], b_conv_ln[0:1],
               col_r, col_ca, col_cb)
    x1 = _outproj(mix, w_out[0].astype(BF16), x, ga_m)

    hp, eidx, wts, rank, cnt = _route(x1, g_norm_ffn[0:1], sh_f, sc_f, w_router[0].T,
                                      b_router[0].reshape(N_EXPERTS, 1))
    nk = t * TOP_K
    nb = (nk + MOE_BLOCK - 1) // MOE_BLOCK + N_EXPERTS
    counts = cnt[:, 0]
    pcounts = (counts + MOE_BLOCK - 1) // MOE_BLOCK * MOE_BLOCK
    pend = jnp.cumsum(pcounts)
    pstart = pend - pcounts
    experts = jnp.arange(N_EXPERTS, dtype=I32)
    pstart_tok = jnp.sum(jnp.where(eidx[:, :, None] == experts, pstart, 0), axis=-1)
    dest = (pstart_tok + rank).T.reshape(-1).astype(I32)
    tok_ids = jnp.repeat(jnp.arange(t, dtype=I32), TOP_K)
    slot_tok = jnp.zeros((nb * MOE_BLOCK,), I32).at[dest].set(tok_ids, unique_indices=True,
                                                               mode="promise_in_bounds")
    blk_start = jnp.arange(nb, dtype=I32) * MOE_BLOCK
    blk_e = jnp.minimum(jnp.sum(pend[None, :] <= blk_start[:, None], axis=1), N_EXPERTS - 1).astype(I32)
    owns = counts > 0
    run_of_e = jnp.cumsum(owns.astype(I32)) - 1
    run_e = jnp.sum(jnp.where(owns[None, :] & (run_of_e[None, :] == experts[:, None]), experts[None, :], 0),
                    axis=1).astype(I32)
    onehot = blk_e[:, None] == experts[None, :]
    blk_run = jnp.sum(jnp.where(onehot, run_of_e[None, :], 0), axis=1).astype(I32)
    nruns = jnp.sum(owns.astype(I32))
    meta = jnp.stack([pend[-1] // MOE_BLOCK, nruns]).astype(I32)
    blk_first = jnp.sum(jnp.where(onehot, pstart[None, :], 0), axis=1) // MOE_BLOCK
    run_len = jnp.maximum(jnp.sum(jnp.where(onehot, pcounts[None, :], 0), axis=1) // MOE_BLOCK, 1)
    pos = jnp.arange(nb, dtype=I32) - blk_first
    share = (WUNITS * (pos + 1) + run_len - 1) // run_len
    blk_goal = jnp.minimum(WUNITS * (blk_run + 1) + share, WUNITS * nruns).astype(I32)

    ys = _experts(blk_run, blk_goal, run_e, meta, slot_tok, hp, w_e_gate[0], w_e_up[0], w_e_down[0])
    base = _shared(hp, w_s_gate[0].astype(BF16), w_s_up[0].astype(BF16), w_s_down[0].astype(BF16), x1, ga_f)
    return _combine(dest, ys, wts.T, base, ga_f, g_final.reshape(1, d))
```

```python
import functools

import jax
import jax.numpy as jnp
import numpy as np
from jax import lax
from jax.experimental import pallas as pl
from jax.experimental.pallas import tpu as pltpu

F32 = jnp.float32
BF16 = jnp.bfloat16
I32 = jnp.int32
U32 = jnp.uint32

EPS = 1e-6
LANES = 128
GLA_HK = 64
GLA_HV = 128
GLA_CHUNK = 64
GATE_RANK = 16
GATE_TAU = 16.0
CONV_WIDTH = 31
CONV_PAD = (CONV_WIDTH - 1) // 2
CONV_LEAD = 16
GRID_W = 64
N_EXPERTS = 64
N_GROUPS = 8
TOPK_GROUPS = 4
TOP_K = 8
ROUTED_SCALE = 2.5
MOE_BLOCK = 128
VMEM_LIMIT = 56 * 1024 * 1024

NT_DIMS = (((1,), (1,)), ((), ()))
TN_DIMS = (((0,), (0,)), ((), ()))


def _params(*sem):
    return pltpu.CompilerParams(dimension_semantics=sem, vmem_limit_bytes=VMEM_LIMIT)


def _tile(n, *preferred):
    for p in preferred:
        if n % p == 0:
            return p
    raise ValueError(f"no tile for {n} among {preferred}")


def _sigmoid(x):
    return 1.0 / (1.0 + jnp.exp(-x))


def _silu(x):
    return x * _sigmoid(x)


def _ada_kernel(cb_ref, w_ref, b_ref, o_ref, s_scr):
    @pl.when(pl.program_id(0) == 0)
    def _():
        s_scr[...] = _silu(cb_ref[...])

    d, tn = w_ref.shape
    for jt in range(tn // LANES):
        sl = slice(jt * LANES, (jt + 1) * LANES)
        w = w_ref[:, sl]
        for m in range(2):
            p = (w * s_scr[m]).reshape(d // 8, 8, LANES).sum(axis=0)
            o_ref[m:m + 1, sl] = p.sum(axis=0, keepdims=True) + b_ref[:, sl]


def _ada(c2, w_ada, b_ada):
    d, n = w_ada.shape
    tn = _tile(n, 1024, 512, LANES)
    cb = jnp.broadcast_to(c2[:, :, None], (2, d, LANES))
    return pl.pallas_call(
        _ada_kernel,
        grid=(n // tn,),
        in_specs=[pl.BlockSpec((2, d, LANES), lambda j: (0, 0, 0)),
                  pl.BlockSpec((d, tn), lambda j: (0, j)),
                  pl.BlockSpec((1, tn), lambda j: (0, j))],
        out_specs=pl.BlockSpec((2, tn), lambda j: (0, j)),
        out_shape=jax.ShapeDtypeStruct((2, n), F32),
        scratch_shapes=[pltpu.VMEM((2, d, LANES), F32)],
        compiler_params=_params("arbitrary"),
        name="ada",
    )(cb, w_ada, b_ada.reshape(1, n))


def _norm_mod(x, g, sh, sc):
    ms = jnp.mean(x * x, axis=-1, keepdims=True)
    y = x * lax.rsqrt(ms + EPS) * g
    return y * (1.0 + sc) + sh


def _proj_kernel(x_ref, g_ref, sh_ref, sc_ref, wl_ref, wa_ref, wb_ref, o_ref, low_ref, h_scr, *, na):
    j = pl.program_id(1)

    @pl.when(j == 0)
    def _():
        rows = min(256, h_scr.shape[0])
        for r in range(0, h_scr.shape[0], rows):
            h_scr[r:r + rows] = _norm_mod(x_ref[r:r + rows], g_ref[...], sh_ref[...], sc_ref[...]).astype(BF16)
        low_ref[...] = jnp.dot(h_scr[...], wl_ref[...], preferred_element_type=F32)

    @pl.when(j < na)
    def _():
        o_ref[...] = jnp.dot(h_scr[...], wa_ref[...], preferred_element_type=F32)

    @pl.when(j >= na)
    def _():
        o_ref[...] = jnp.dot(h_scr[...], wb_ref[...], preferred_element_type=F32)


def _proj(x, g, sh, sc, w_low, w, cols_a, wb):
    _, t, d = x.shape
    tm = _tile(t, 1024, 512, 256, 128)
    tn = _tile(cols_a, 512, 256)
    na, nb = cols_a // tn, wb.shape[1] // tn
    assert wb.shape[1] % tn == 0
    vec = pl.BlockSpec((1, d), lambda i, j: (0, 0))
    return pl.pallas_call(
        functools.partial(_proj_kernel, na=na),
        grid=(t // tm, na + nb),
        in_specs=[pl.BlockSpec((None, tm, d), lambda i, j: (0, i, 0), pipeline_mode=pl.Buffered(1)),
                  vec, vec, vec,
                  pl.BlockSpec((d, LANES), lambda i, j: (0, 0)),
                  pl.BlockSpec((d, tn), lambda i, j: (0, jnp.minimum(j, na - 1))),
                  pl.BlockSpec((d, tn), lambda i, j: (0, jnp.maximum(j - na, 0)))],
        out_specs=[pl.BlockSpec((tm, tn), lambda i, j: (i, j)),
                   pl.BlockSpec((tm, LANES), lambda i, j: (i, 0))],
        out_shape=[jax.ShapeDtypeStruct((t, (na + nb) * tn), F32),
                   jax.ShapeDtypeStruct((t, LANES), F32)],
        scratch_shapes=[pltpu.VMEM((tm, d), BF16)],
        compiler_params=_params("arbitrary", "arbitrary"),
        name="proj",
    )(x, g, sh, sc, w_low, w, wb)


def _gla_block(q_ref, k_ref, v_ref, low_ref, wg, bg, tri, s_t, o_ref, fwd, with_output):
    c = GLA_CHUNK
    nch = q_ref.shape[0] // c
    z = jnp.dot(low_ref[...].astype(BF16), wg, preferred_element_type=F32) + bg
    g = (jnp.minimum(z, 0.0) - jnp.log(1.0 + jnp.exp(-jnp.abs(z)))) * (1.0 / GATE_TAU)
    g1 = g.astype(BF16)
    r1 = g - g1.astype(F32)
    g2 = r1.astype(BF16)
    g3 = (r1 - g2.astype(F32)).astype(BF16)
    b3 = jnp.dot(tri, jnp.concatenate([g1, g2, g3], axis=1), preferred_element_type=F32)
    b = b3[:, :LANES] + b3[:, LANES:2 * LANES] + b3[:, 2 * LANES:]

    def chunk_rows(n):
        return slice(n * c, (n + 1) * c)

    def per_chunk(row):
        return jnp.concatenate([jnp.broadcast_to(b[n * c + row:n * c + row + 1], (c, LANES))
                                for n in range(nch)], axis=0)

    tot = per_chunk(c - 1 if fwd else 0)
    lane0 = lax.broadcasted_iota(I32, (c, LANES), 1) < GLA_HK

    def stack(a, n):
        an = a[chunk_rows(n)]
        return jnp.concatenate([jnp.where(lane0, an, 0.0), jnp.where(lane0, 0.0, an)], axis=0).astype(BF16)

    k = k_ref[...]
    ks = k * jnp.exp(tot - b)
    vs = [jnp.concatenate([v_ref[chunk_rows(n), :GLA_HV], v_ref[chunk_rows(n), GLA_HV:]],
                          axis=0).astype(BF16) for n in range(nch)]
    upd = [lax.dot_general(vs[n], stack(ks, n), TN_DIMS, preferred_element_type=F32) for n in range(nch)]
    decay = jnp.exp(tot)
    order = range(nch) if fwd else range(nch - 1, -1, -1)
    s_in = [None] * nch
    for n in order:
        s_in[n] = s_t
        s_t = s_t * decay[n * c:n * c + 1] + upd[n]
    if not with_output:
        return s_t
    mid = per_chunk(c // 2)
    qs = q_ref[...] * (GLA_HK ** -0.5)
    qa = qs * jnp.exp(b - mid)
    ka = k * jnp.exp(mid - b)
    qi = qs * jnp.exp(b)
    r2 = lax.broadcasted_iota(I32, (2 * c, 2 * c), 0)
    c2 = lax.broadcasted_iota(I32, (2 * c, 2 * c), 1)
    keep = ((r2 >= c) == (c2 >= c)) & ((c2 <= r2) if fwd else (c2 >= r2))
    for n in range(nch):
        att = lax.dot_general(stack(qa, n), stack(ka, n), NT_DIMS, preferred_element_type=F32)
        att = jnp.where(keep, att, 0.0).astype(BF16)
        o = jnp.dot(att, vs[n], preferred_element_type=F32)
        o = o + lax.dot_general(stack(qi, n), s_in[n].astype(BF16), NT_DIMS, preferred_element_type=F32)
        o_ref[chunk_rows(n)] = jnp.concatenate([o[:c], o[c:]], axis=1)
    return s_t


def _gla_kernel(*refs, with_output):
    (qf, kf, vf, lf, qb, kb, vb, lb, wgf, wgb, bgf, bgb, s0_ref) = refs[:13]
    if with_output:
        of_ref, ob_ref, s_scr, tri_scr = refs[13:]
    else:
        sout_ref, s_scr, tri_scr = refs[13:]
        of_ref = ob_ref = None
    i = pl.program_id(1)
    c = GLA_CHUNK
    rows = qf.shape[0]

    @pl.when((pl.program_id(0) == 0) & (i == 0))
    def _():
        r = lax.broadcasted_iota(I32, (rows, rows), 0)
        cc = lax.broadcasted_iota(I32, (rows, rows), 1)
        shift = c.bit_length() - 1
        same = jnp.right_shift(r, shift) == jnp.right_shift(cc, shift)
        tri_scr[0] = jnp.where(same & (cc <= r), 1.0, 0.0).astype(BF16)
        tri_scr[1] = jnp.where(same & (cc >= r), 1.0, 0.0).astype(BF16)

    @pl.when(i == 0)
    def _():
        s_scr[...] = s0_ref[:, 0]

    s_scr[0] = _gla_block(qf, kf, vf, lf, wgf[...], bgf[...], tri_scr[0], s_scr[0], of_ref, True, with_output)
    s_scr[1] = _gla_block(qb, kb, vb, lb, wgb[...], bgb[...], tri_scr[1], s_scr[1], ob_ref, False, with_output)
    if not with_output:
        @pl.when(i == pl.num_programs(1) - 1)
        def _():
            sout_ref[:, 0] = s_scr[...]


def _gla(proj, low, wg_f, wg_b, bg_f, bg_b, s0, with_output, col_q, col_k, col_v):
    t = proj.shape[0]
    cb = _tile(t // GLA_CHUNK, 8, 4, 2, 1)
    rows = cb * GLA_CHUNK
    nb = t // rows
    npairs = wg_f.shape[1] // (2 * GLA_HK)
    pw = 2 * GLA_HK
    vw = 2 * GLA_HV
    fi = lambda p, i: i
    bi = lambda p, i: nb - 1 - i

    def spec(width, col0, blk):
        return pl.BlockSpec((rows, width), lambda p, i: (blk(p, i), col0 // width + p))

    def low_spec(blk):
        return pl.BlockSpec((rows, LANES), lambda p, i: (blk(p, i), 0))

    in_specs = [spec(pw, col_q, fi), spec(pw, col_k, fi), spec(vw, col_v, fi), low_spec(fi),
                spec(pw, col_q, bi), spec(pw, col_k, bi), spec(vw, col_v, bi), low_spec(bi),
                pl.BlockSpec((LANES, pw), lambda p, i: (0, p)),
                pl.BlockSpec((LANES, pw), lambda p, i: (0, p)),
                pl.BlockSpec((1, pw), lambda p, i: (0, p)),
                pl.BlockSpec((1, pw), lambda p, i: (0, p)),
                pl.BlockSpec((2, 1, GLA_HV, pw), lambda p, i: (0, p, 0, 0))]
    if with_output:
        out_specs = [pl.BlockSpec((rows, vw), lambda p, i: (i, p)),
                     pl.BlockSpec((rows, vw), lambda p, i: (nb - 1 - i, p))]
        out_shape = [jax.ShapeDtypeStruct((t, npairs * 2 * GLA_HV), F32)] * 2
    else:
        out_specs = pl.BlockSpec((2, 1, GLA_HV, pw), lambda p, i: (0, p, 0, 0))
        out_shape = jax.ShapeDtypeStruct((2, npairs, GLA_HV, pw), F32)
    return pl.pallas_call(
        functools.partial(_gla_kernel, with_output=with_output),
        grid=(npairs, nb),
        in_specs=in_specs,
        out_specs=out_specs,
        out_shape=out_shape,
        scratch_shapes=[pltpu.VMEM((2, GLA_HV, pw), F32), pltpu.VMEM((2, rows, rows), BF16)],
        compiler_params=_params("arbitrary", "arbitrary"),
        name="gla_out" if with_output else "gla_state",
    )(proj, proj, proj, low, proj, proj, proj, low, wg_f, wg_b, bg_f, bg_b, s0)


def _mix_kernel(of_ref, ob_ref, r_ref, ca_ref, cb_ref, gout_ref, wdw_ref, bdw_ref, gln_ref, bln_ref, pick_ref,
                o_ref, upad, y_scr):
    tm = of_ref.shape[0]
    dg = of_ref.shape[1]
    dc = ca_ref.shape[1]
    nrow = tm // GRID_W
    lead = CONV_LEAD
    for h in range(dg // GLA_HV):
        sl = slice(h * GLA_HV, (h + 1) * GLA_HV)
        o = of_ref[:, sl] + ob_ref[:, sl]
        o = o * lax.rsqrt(jnp.mean(o * o, axis=-1, keepdims=True) + EPS) * gout_ref[:, sl]
        o_ref[:, sl] = (o * _silu(r_ref[:, sl])).astype(o_ref.dtype)
    for r in range(nrow):
        rs = slice(r * GRID_W, (r + 1) * GRID_W)
        upad[r, 0:lead, :] = jnp.zeros((lead, dc), F32)
        upad[r, lead:lead + GRID_W, :] = ca_ref[rs, :] * _sigmoid(cb_ref[rs, :])
        upad[r, lead + GRID_W:, :] = jnp.zeros((lead, dc), F32)
    cw = 256
    span = upad.shape[1]
    kpad = pick_ref.shape[1] - CONV_WIDTH * span

    def row_body(r, carry):
        for cc in range(dc // cw):
            cs = slice(cc * cw, (cc + 1) * cw)
            u = upad[r, :, cs]
            taps = [(wdw_ref[j:j + 1, cs] * u).astype(BF16) for j in range(CONV_WIDTH)]
            taps.append(jnp.zeros((kpad, cw), BF16))
            y_scr[r, :, cs] = (jnp.dot(pick_ref[...], jnp.concatenate(taps, axis=0), preferred_element_type=F32)
                               + bdw_ref[:, cs])
        return carry

    lax.fori_loop(0, nrow, row_body, 0)
    for r in range(nrow):
        y = y_scr[r]
        mu = jnp.mean(y, axis=-1, keepdims=True)
        yc = y - mu
        var = jnp.mean(yc * yc, axis=-1, keepdims=True)
        yn = yc * lax.rsqrt(var + EPS) * gln_ref[...] + bln_ref[...]
        o_ref[r * GRID_W:(r + 1) * GRID_W, dg:] = _silu(yn).astype(o_ref.dtype)


def _mix(o_f, o_b, proj, g_out, w_dw, b_dw, g_ln, b_ln, col_r, col_ca, col_cb):
    t, dg = o_f.shape
    tm = _tile(t, 256, 128, GRID_W)
    dc = w_dw.shape[1]
    nrow = tm // GRID_W
    span = GRID_W + 2 * CONV_LEAD
    kdim = -(-CONV_WIDTH * span // LANES) * LANES
    pick = np.zeros((GRID_W, kdim), np.float32)
    for j in range(CONV_WIDTH):
        pick[np.arange(GRID_W), j * span + np.arange(GRID_W) + (CONV_LEAD - CONV_PAD + j)] = 1.0
    pick = jnp.asarray(pick, BF16)
    row = lambda width: pl.BlockSpec((1, width), lambda i: (0, 0))
    return pl.pallas_call(
        _mix_kernel,
        grid=(t // tm,),
        in_specs=[pl.BlockSpec((tm, dg), lambda i: (i, 0)),
                  pl.BlockSpec((tm, dg), lambda i: (i, 0)),
                  pl.BlockSpec((tm, dg), lambda i: (i, col_r // dg)),
                  pl.BlockSpec((tm, dc), lambda i: (i, col_ca // dc)),
                  pl.BlockSpec((tm, dc), lambda i: (i, col_cb // dc)),
                  row(dg),
                  pl.BlockSpec((CONV_WIDTH, dc), lambda i: (0, 0)),
                  row(dc), row(dc), row(dc),
                  pl.BlockSpec((GRID_W, kdim), lambda i: (0, 0))],
        out_specs=pl.BlockSpec((tm, dg + dc), lambda i: (i, 0)),
        out_shape=jax.ShapeDtypeStruct((t, dg + dc), BF16),
        scratch_shapes=[pltpu.VMEM((nrow, span, dc), F32),
                        pltpu.VMEM((nrow, GRID_W, dc), F32)],
        compiler_params=_params("arbitrary"),
        name="mix",
    )(o_f, o_b, proj, proj, proj, g_out, w_dw, b_dw, g_ln, b_ln, pick)


def _outproj_kernel(m_ref, w_ref, x_ref, ga_ref, o_ref):
    o_ref[...] = x_ref[...] + ga_ref[...] * jnp.dot(m_ref[...], w_ref[...], preferred_element_type=F32)


def _outproj(mix, w, x, ga):
    t, k = mix.shape
    n = w.shape[1]
    tm = _tile(t, 1024, 512, 256)
    tn = _tile(n, 512, 256, LANES)
    return pl.pallas_call(
        _outproj_kernel,
        grid=(t // tm, n // tn),
        in_specs=[pl.BlockSpec((tm, k), lambda i, j: (i, 0)),
                  pl.BlockSpec((k, tn), lambda i, j: (0, j)),
                  pl.BlockSpec((None, tm, tn), lambda i, j: (0, i, j)),
                  pl.BlockSpec((1, tn), lambda i, j: (0, j))],
        out_specs=pl.BlockSpec((tm, tn), lambda i, j: (i, j)),
        out_shape=jax.ShapeDtypeStruct((t, n), F32),
        compiler_params=_params("arbitrary", "arbitrary"),
        name="out_proj",
    )(mix, w, x, ga)


def _pack_halves(h):
    c = h.shape[1] // 2
    lo = lax.bitcast_convert_type(h[:, :c].astype(BF16).astype(F32), U32)
    hi = lax.bitcast_convert_type(h[:, c:].astype(BF16).astype(F32), U32)
    return (lo >> 16) | (hi & jnp.uint32(0xFFFF0000))


def _unpack_halves(p):
    lo = lax.bitcast_convert_type(p << 16, F32).astype(BF16)
    hi = lax.bitcast_convert_type(p & jnp.uint32(0xFFFF0000), F32).astype(BF16)
    return jnp.concatenate([lo, hi], axis=1)


def _route_kernel(x_ref, g_ref, sh_ref, sc_ref, wr_ref, br_ref, hp_ref, e_ref, w_ref, rk_ref, cnt_ref,
                  carry):
    tm = x_ref.shape[0]
    ne = N_EXPERTS
    gs = ne // N_GROUPS
    neg = -jnp.inf

    @pl.when(pl.program_id(0) == 0)
    def _():
        carry[...] = jnp.zeros_like(carry)

    h = _norm_mod(x_ref[...], g_ref[...], sh_ref[...], sc_ref[...])
    hp_ref[...] = _pack_halves(h)
    logits = lax.dot_general(wr_ref[...], h, NT_DIMS, preferred_element_type=F32,
                             precision=lax.Precision.HIGHEST)
    scores = _sigmoid(logits)
    sel = scores + br_ref[...]
    sel3 = sel.reshape(N_GROUPS, gs, tm)
    sub = lax.broadcasted_iota(I32, (N_GROUPS, gs, tm), 1)
    m1 = jnp.max(sel3, axis=1, keepdims=True)
    i1 = jnp.min(jnp.where(sel3 == m1, sub, gs), axis=1, keepdims=True)
    m2 = jnp.max(jnp.where(sub == i1, neg, sel3), axis=1, keepdims=True)
    gscore = (m1 + m2).reshape(N_GROUPS, tm)
    gid = lax.broadcasted_iota(I32, (N_GROUPS, tm), 0)
    gmask = jnp.zeros((N_GROUPS, tm), F32)
    for _ in range(TOPK_GROUPS):
        mx = jnp.max(gscore, axis=0, keepdims=True)
        idx = jnp.min(jnp.where(gscore == mx, gid, N_GROUPS), axis=0, keepdims=True)
        pick = gid == idx
        gmask = jnp.where(pick, 1.0, gmask)
        gscore = jnp.where(pick, neg, gscore)
    emask = jnp.broadcast_to(gmask.reshape(N_GROUPS, 1, tm), (N_GROUPS, gs, tm)).reshape(ne, tm)
    cand = jnp.where(emask > 0.0, sel, neg)
    eid = lax.broadcasted_iota(I32, (ne, tm), 0)
    chosen = jnp.zeros((ne, tm), F32)
    idxs, scs = [], []
    for _ in range(TOP_K):
        mx = jnp.max(cand, axis=0, keepdims=True)
        idx = jnp.min(jnp.where(cand == mx, eid, ne), axis=0, keepdims=True)
        pick = eid == idx
        idxs.append(idx)
        scs.append(jnp.sum(jnp.where(pick, scores, 0.0), axis=0, keepdims=True))
        chosen = jnp.where(pick, 1.0, chosen)
        cand = jnp.where(pick, neg, cand)
    ssum = scs[0]
    for s in scs[1:]:
        ssum = ssum + s
    before = (lax.broadcasted_iota(I32, (tm, tm), 0) < lax.broadcasted_iota(I32, (tm, tm), 1))
    prior = jnp.dot(chosen.astype(BF16), before.astype(BF16), preferred_element_type=F32) + carry[...]
    for k in range(TOP_K):
        e_ref[k:k + 1, :] = idxs[k]
        w_ref[k:k + 1, :] = scs[k] / ssum * ROUTED_SCALE
        rk = jnp.sum(jnp.where(eid == idxs[k], prior, 0.0), axis=0, keepdims=True)
        rk_ref[k:k + 1, :] = rk.astype(I32)
    carry[...] = carry[...] + jnp.sum(chosen, axis=1, keepdims=True)
    cnt_ref[...] = jnp.broadcast_to(carry[...], cnt_ref.shape).astype(I32)


def _route(x, g, sh, sc, w_router_t, b_router):
    t, d = x.shape
    tm = _tile(t, 256, LANES)
    vec = pl.BlockSpec((1, d), lambda i: (0, 0))
    tok = lambda dt: jax.ShapeDtypeStruct((TOP_K, t), dt)
    tok_spec = pl.BlockSpec((TOP_K, tm), lambda i: (0, i))
    return pl.pallas_call(
        _route_kernel,
        grid=(t // tm,),
        in_specs=[pl.BlockSpec((tm, d), lambda i: (i, 0)), vec, vec, vec,
                  pl.BlockSpec((N_EXPERTS, d), lambda i: (0, 0)),
                  pl.BlockSpec((N_EXPERTS, 1), lambda i: (0, 0))],
        out_specs=[pl.BlockSpec((tm, d // 2), lambda i: (i, 0)), tok_spec, tok_spec, tok_spec,
                   pl.BlockSpec((N_EXPERTS, LANES), lambda i: (0, 0))],
        out_shape=[jax.ShapeDtypeStruct((t, d // 2), U32), tok(I32), tok(F32), tok(I32),
                   jax.ShapeDtypeStruct((N_EXPERTS, LANES), I32)],
        scratch_shapes=[pltpu.VMEM((N_EXPERTS, 1), F32)],
        compiler_params=_params("arbitrary"),
        name="route",
    )(x, g, sh, sc, w_router_t, b_router)


WUNITS = 8
WSLOTS = 3
XSLOTS = 3


def _experts_kernel(blk_run, blk_goal, run_e, meta, slot_tok, hp_hbm, wg_hbm, wu_hbm, wd_hbm, o_ref,
                    xbuf, sem, sg, su, sd, wsem, wgb, wub, wdb, done_ref):
    b = pl.program_id(0)
    nb = pl.num_programs(0)
    nused = meta[0]
    nunits = meta[1] * WUNITS
    ug = sg.shape[1]
    ud = sd.shape[1]

    def unit_copies(u):
        e = run_e[u // WUNITS]
        k = u % WUNITS
        s = u % WSLOTS
        return (pltpu.make_async_copy(wg_hbm.at[e, pl.ds(k * ug, ug)], sg.at[s], wsem.at[s]),
                pltpu.make_async_copy(wu_hbm.at[e, pl.ds(k * ug, ug)], su.at[s], wsem.at[s]),
                pltpu.make_async_copy(wd_hbm.at[e, pl.ds(k * ud, ud)], sd.at[s], wsem.at[s]))

    def start_unit(u):
        for cp in unit_copies(u):
            cp.start(priority=1)

    def process_unit(u):
        for cp in unit_copies(u):
            cp.wait()

        @pl.when(u + WSLOTS - 1 < nunits)
        def _():
            start_unit(u + WSLOTS - 1)

        r = (u // WUNITS) % 2
        k = u % WUNITS
        s = u % WSLOTS
        wgb[r, pl.ds(pl.multiple_of(k * ug, ug), ug), :] = sg[s].astype(BF16)
        wub[r, pl.ds(pl.multiple_of(k * ug, ug), ug), :] = su[s].astype(BF16)
        wdb[r, pl.ds(pl.multiple_of(k * ud, ud), ud), :] = sd[s].astype(BF16)

    def start_gather(blk, slot, rows=range(MOE_BLOCK)):
        for r in rows:
            tok = slot_tok[blk * MOE_BLOCK + r]
            pltpu.make_async_copy(hp_hbm.at[pl.ds(tok, 1)], xbuf.at[slot, pl.ds(r, 1)],
                                  sem.at[slot]).start(priority=0)

    def wait_gather(slot):
        pltpu.make_async_copy(hp_hbm.at[pl.ds(0, MOE_BLOCK)], xbuf.at[slot], sem.at[slot]).wait()

    live = b < nused
    slot = lax.rem(b, XSLOTS)

    @pl.when((b == 0) & live)
    def _():
        done_ref[0] = 0
        for u in range(WSLOTS - 1):
            @pl.when(u < nunits)
            def _():
                start_unit(u)
        for blk in range(XSLOTS - 1):
            @pl.when(blk < nused)
            def _():
                start_gather(blk, blk)

    @pl.when(live)
    def _():
        done = done_ref[0]
        goal = blk_goal[b]

        def body(n, carry):
            process_unit(done + n)
            return carry
        lax.fori_loop(0, jnp.maximum(goal - done, 0), body, 0)
        done_ref[0] = jnp.maximum(goal, done)

    ahead = b + XSLOTS - 1
    ahead_live = (ahead < nb) & (ahead < nused)

    def compute(m, gather_ahead):
        r = blk_run[b] % 2
        half = xbuf.shape[2]
        group = MOE_BLOCK // 4

        def issue(q):
            if gather_ahead:
                start_gather(ahead, (m + XSLOTS - 1) % XSLOTS, range(q * group, (q + 1) * group))

        wait_gather(m)
        x = _unpack_halves(xbuf[m])
        issue(0)
        hg = jnp.dot(x, wgb[r], preferred_element_type=F32)
        issue(1)
        hu = jnp.dot(x, wub[r], preferred_element_type=F32)
        issue(2)
        hb = (_silu(hg) * hu).astype(BF16)
        for c in range(2):
            lo = slice(c * half // 2, (c + 1) * half // 2)
            hi = slice(half + c * half // 2, half + (c + 1) * half // 2)
            o = jnp.concatenate([jnp.dot(hb, wdb[r, :, lo], preferred_element_type=F32),
                                 jnp.dot(hb, wdb[r, :, hi], preferred_element_type=F32)], axis=1)
            o_ref[:, lo] = _pack_halves(o)
            if c == 0:
                issue(3)

    for m in range(XSLOTS):
        mine = live & (slot == m)
        pl.when(mine & ahead_live)(functools.partial(compute, m, True))
        pl.when(mine & jnp.logical_not(ahead_live))(functools.partial(compute, m, False))

    @pl.when(jnp.logical_not(live))
    def _():
        o_ref[...] = jnp.zeros_like(o_ref)


def _experts(blk_run, blk_goal, run_e, meta, slot_tok, hp, wg, wu, wd):
    nb = blk_run.shape[0]
    _, d, de = wg.shape
    ug, ud = d // WUNITS, de // WUNITS
    hbm = pl.BlockSpec(memory_space=pl.ANY)
    grid_spec = pltpu.PrefetchScalarGridSpec(
        num_scalar_prefetch=5,
        grid=(nb,),
        in_specs=[hbm, hbm, hbm, hbm],
        out_specs=pl.BlockSpec((MOE_BLOCK, d // 2), lambda b, *_: (b, 0)),
        scratch_shapes=[pltpu.VMEM((XSLOTS, MOE_BLOCK, d // 2), U32), pltpu.SemaphoreType.DMA((XSLOTS,)),
                        pltpu.VMEM((WSLOTS, ug, de), F32), pltpu.VMEM((WSLOTS, ug, de), F32),
                        pltpu.VMEM((WSLOTS, ud, d), F32), pltpu.SemaphoreType.DMA((WSLOTS,)),
                        pltpu.VMEM((2, d, de), BF16), pltpu.VMEM((2, d, de), BF16),
                        pltpu.VMEM((2, de, d), BF16), pltpu.SMEM((1,), I32)],
    )
    return pl.pallas_call(
        _experts_kernel,
        grid_spec=grid_spec,
        out_shape=jax.ShapeDtypeStruct((nb * MOE_BLOCK, d // 2), U32),
        compiler_params=_params("arbitrary"),
        name="experts",
    )(blk_run, blk_goal, run_e, meta, slot_tok, hp, wg, wu, wd)


def _shared_kernel(hp_ref, wg_ref, wu_ref, wd_ref, x_ref, ga_ref, o_ref):
    x = _unpack_halves(hp_ref[...])
    hg = jnp.dot(x, wg_ref[...], preferred_element_type=F32)
    hu = jnp.dot(x, wu_ref[...], preferred_element_type=F32)
    hb = (_silu(hg) * hu).astype(BF16)
    o_ref[...] = x_ref[...] + ga_ref[...] * jnp.dot(hb, wd_ref[...], preferred_element_type=F32)


def _shared(hp, wg, wu, wd, x, ga):
    t, d = x.shape
    tm = _tile(t, 256, LANES)
    ds_ = wg.shape[1]
    return pl.pallas_call(
        _shared_kernel,
        grid=(t // tm,),
        in_specs=[pl.BlockSpec((tm, d // 2), lambda i: (i, 0)),
                  pl.BlockSpec((d, ds_), lambda i: (0, 0)),
                  pl.BlockSpec((d, ds_), lambda i: (0, 0)),
                  pl.BlockSpec((ds_, d), lambda i: (0, 0)),
                  pl.BlockSpec((tm, d), lambda i: (i, 0)),
                  pl.BlockSpec((1, d), lambda i: (0, 0))],
        out_specs=pl.BlockSpec((tm, d), lambda i: (i, 0)),
        out_shape=jax.ShapeDtypeStruct((t, d), F32),
        compiler_params=_params("arbitrary"),
        name="shared",
    )(hp, wg, wu, wd, x, ga)


def _combine_kernel(dest, ys_hbm, w_ref, base_ref, ga_ref, gf_ref, o_ref, buf_a, buf_b, sem):
    i = pl.program_id(0)
    n = pl.num_programs(0)
    tm = buf_a.shape[1]
    half = buf_a.shape[2]
    d = 2 * half

    def start_gather(tile, buf, s):
        for j in range(tm * TOP_K):
            src = dest[tile * (tm * TOP_K) + j]
            pltpu.make_async_copy(ys_hbm.at[pl.ds(src, 1)], buf.at[j % TOP_K, pl.ds(j // TOP_K, 1)],
                                  sem.at[s]).start(priority=j % 2)

    def wait_gather(buf, s):
        for k in range(TOP_K):
            pltpu.make_async_copy(ys_hbm.at[pl.ds(0, tm)], buf.at[k], sem.at[s]).wait()

    def finish(buf, rows):
        ylo = jnp.zeros((tm, half), F32)
        yhi = jnp.zeros((tm, half), F32)
        for k in range(TOP_K):
            p = buf[k]
            w = w_ref[rows, k:k + 1]
            ylo = ylo + w * lax.bitcast_convert_type(p << 16, F32)
            yhi = yhi + w * lax.bitcast_convert_type(p & jnp.uint32(0xFFFF0000), F32)
        xlo = base_ref[rows, :half] + ga_ref[:, :half] * ylo
        xhi = base_ref[rows, half:] + ga_ref[:, half:] * yhi
        ms = (jnp.sum(xlo * xlo, axis=-1, keepdims=True) + jnp.sum(xhi * xhi, axis=-1, keepdims=True)) / d
        inv = lax.rsqrt(ms + EPS)
        o_ref[rows, :half] = xlo * inv * gf_ref[:, :half]
        o_ref[rows, half:] = xhi * inv * gf_ref[:, half:]

    def step(gather_next):
        wait_gather(buf_a, 0)
        start_gather(2 * i + 1, buf_b, 1)
        finish(buf_a, slice(0, tm))
        wait_gather(buf_b, 1)
        if gather_next:
            start_gather(2 * i + 2, buf_a, 0)
        finish(buf_b, slice(tm, 2 * tm))

    @pl.when(i == 0)
    def _():
        start_gather(0, buf_a, 0)

    pl.when(i + 1 < n)(functools.partial(step, True))
    pl.when(i + 1 == n)(functools.partial(step, False))


def _combine(dest, ys, wts, base, ga, g_final):
    t, d = base.shape
    tm = 64
    assert t % (2 * tm) == 0
    grid_spec = pltpu.PrefetchScalarGridSpec(
        num_scalar_prefetch=1,
        grid=(t // (2 * tm),),
        in_specs=[pl.BlockSpec(memory_space=pl.ANY),
                  pl.BlockSpec((2 * tm, TOP_K), lambda i, s: (i, 0)),
                  pl.BlockSpec((2 * tm, d), lambda i, s: (i, 0)),
                  pl.BlockSpec((1, d), lambda i, s: (0, 0)),
                  pl.BlockSpec((1, d), lambda i, s: (0, 0))],
        out_specs=pl.BlockSpec((None, 2 * tm, d), lambda i, s: (0, i, 0)),
        scratch_shapes=[pltpu.VMEM((TOP_K, tm, d // 2), U32), pltpu.VMEM((TOP_K, tm, d // 2), U32),
                        pltpu.SemaphoreType.DMA((2,))],
    )
    return pl.pallas_call(
        _combine_kernel,
        grid_spec=grid_spec,
        out_shape=jax.ShapeDtypeStruct((1, t, d), F32),
        compiler_params=_params("arbitrary"),
        name="combine",
    )(dest, ys, wts, base, ga, g_final)


def kernel(x, c, ctx, c_ctx, w_ada, b_ada, g_norm_mix, g_norm_ffn, w_in, w_gate_up, b_gate_up,
           g_gla_out, w_dw, b_dw, g_conv_ln, b_conv_ln, w_out, w_router, b_router,
           w_e_gate, w_e_up, w_e_down, w_s_gate, w_s_up, w_s_down, g_final):
    assert x.shape[0] == 1 and w_ada.shape[0] == 1, "single batch element, single layer"
    t, d = x.shape[1], x.shape[2]
    dk = w_gate_up.shape[3]
    heads = dk // GLA_HK
    dg = heads * GLA_HV
    dc = w_dw.shape[2]

    col_q, col_k, col_v = 0, dk, 2 * dk
    col_r = col_v + dg
    col_ca = col_r + dg
    col_cb = col_ca + dc
    n_gate = 2 * GATE_RANK
    w16 = w_in[0].astype(BF16)
    w_b = w16[:, col_ca + n_gate:]
    w_low = jnp.pad(w16[:, col_ca:col_ca + n_gate], ((0, 0), (0, LANES - n_gate)))

    mod = _ada(jnp.stack([c[0], c_ctx]), w_ada[0], b_ada[0])
    sh_m, sc_m, ga_m, sh_f, sc_f, ga_f = [mod[0:1, k * d:(k + 1) * d] for k in range(6)]
    csh_m, csc_m = mod[1:2, 0:d], mod[1:2, d:2 * d]

    wg_f = jnp.zeros((LANES, dk), F32).at[:GATE_RANK].set(w_gate_up[0, 0]).astype(BF16)
    wg_b = jnp.zeros((LANES, dk), F32).at[GATE_RANK:n_gate].set(w_gate_up[0, 1]).astype(BF16)
    bg_f = b_gate_up[0, 0:1]
    bg_b = b_gate_up[0, 1:2]
    cols = dict(col_q=col_q, col_k=col_k, col_v=col_v)

    gn_mix = g_norm_mix[0:1]
    proj_c, low_c = _proj(ctx, gn_mix, csh_m, csc_m, w_low, w16, col_ca, w_b)
    s0 = jnp.zeros((2, heads // 2, GLA_HV, 2 * GLA_HK), F32)
    s_ctx = _gla(proj_c, low_c, wg_f, wg_b, bg_f, bg_b, s0, with_output=False, **cols)

    proj_l, low_l = _proj(x, gn_mix, sh_m, sc_m, w_low, w16, col_ca, w_b)
    o_f, o_b = _gla(proj_l, low_l, wg_f, wg_b, bg_f, bg_b, s_ctx, with_output=True, **cols)
    mix = _mix(o_f, o_b, proj_l, g_gla_out[0:1], w_dw[0], b_dw[0:1], g_conv_ln[0:1], b_conv_ln[0:1],
               col_r, col_ca, col_cb)
    x1 = _outproj(mix, w_out[0].astype(BF16), x, ga_m)

    hp, eidx, wts, rank, cnt = _route(x1, g_norm_ffn[0:1], sh_f, sc_f, w_router[0].T,
                                      b_router[0].reshape(N_EXPERTS, 1))
    nk = t * TOP_K
    nb = (nk + MOE_BLOCK - 1) // MOE_BLOCK + N_EXPERTS
    counts = cnt[:, 0]
    pcounts = (counts + MOE_BLOCK - 1) // MOE_BLOCK * MOE_BLOCK
    pend = jnp.cumsum(pcounts)
    pstart = pend - pcounts
    experts = jnp.arange(N_EXPERTS, dtype=I32)
    pstart_tok = jnp.sum(jnp.where(eidx[:, :, None] == experts, pstart, 0), axis=-1)
    dest = (pstart_tok + rank).T.reshape(-1).astype(I32)
    tok_ids = jnp.repeat(jnp.arange(t, dtype=I32), TOP_K)
    slot_tok = jnp.zeros((nb * MOE_BLOCK,), I32).at[dest].set(tok_ids, unique_indices=True,
                                                               mode="promise_in_bounds")
    blk_start = jnp.arange(nb, dtype=I32) * MOE_BLOCK
    blk_e = jnp.minimum(jnp.sum(pend[None, :] <= blk_start[:, None], axis=1), N_EXPERTS - 1).astype(I32)
    owns = counts > 0
    run_of_e = jnp.cumsum(owns.astype(I32)) - 1
    run_e = jnp.sum(jnp.where(owns[None, :] & (run_of_e[None, :] == experts[:, None]), experts[None, :], 0),
                    axis=1).astype(I32)
    onehot = blk_e[:, None] == experts[None, :]
    blk_run = jnp.sum(jnp.where(onehot, run_of_e[None, :], 0), axis=1).astype(I32)
    nruns = jnp.sum(owns.astype(I32))
    meta = jnp.stack([pend[-1] // MOE_BLOCK, nruns]).astype(I32)
    blk_first = jnp.sum(jnp.where(onehot, pstart[None, :], 0), axis=1) // MOE_BLOCK
    run_len = jnp.maximum(jnp.sum(jnp.where(onehot, pcounts[None, :], 0), axis=1) // MOE_BLOCK, 1)
    pos = jnp.arange(nb, dtype=I32) - blk_first
    share = (WUNITS * (pos + 1) + run_len - 1) // run_len
    blk_goal = jnp.minimum(WUNITS * (blk_run + 1) + share, WUNITS * nruns).astype(I32)

    ys = _experts(blk_run, blk_goal, run_e, meta, slot_tok, hp, w_e_gate[0], w_e_up[0], w_e_down[0])
    base = _shared(hp, w_s_gate[0].astype(BF16), w_s_up[0].astype(BF16), w_s_down[0].astype(BF16), x1, ga_f)
    return _combine(dest, ys, wts.T, base, ga_f, g_final.reshape(1, d))
```

```python
import functools

import jax
import jax.numpy as jnp
import numpy as np
from jax import lax
from jax.experimental import pallas as pl
from jax.experimental.pallas import tpu as pltpu

F32 = jnp.float32
BF16 = jnp.bfloat16
I32 = jnp.int32
U32 = jnp.uint32

EPS = 1e-6
LANES = 128
GLA_HK = 64
GLA_HV = 128
GLA_CHUNK = 64
GATE_RANK = 16
GATE_TAU = 16.0
CONV_WIDTH = 31
CONV_PAD = (CONV_WIDTH - 1) // 2
CONV_LEAD = 16
GRID_W = 64
N_EXPERTS = 64
N_GROUPS = 8
TOPK_GROUPS = 4
TOP_K = 8
ROUTED_SCALE = 2.5
MOE_BLOCK = 128
VMEM_LIMIT = 56 * 1024 * 1024

NT_DIMS = (((1,), (1,)), ((), ()))
TN_DIMS = (((0,), (0,)), ((), ()))


def _params(*sem):
    return pltpu.CompilerParams(dimension_semantics=sem, vmem_limit_bytes=VMEM_LIMIT)


def _tile(n, *preferred):
    for p in preferred:
        if n % p == 0:
            return p
    raise ValueError(f"no tile for {n} among {preferred}")


def _sigmoid(x):
    return 1.0 / (1.0 + jnp.exp(-x))


def _silu(x):
    return x * _sigmoid(x)


def _ada_kernel(cb_ref, w_ref, b_ref, o_ref, s_scr):
    @pl.when(pl.program_id(0) == 0)
    def _():
        s_scr[...] = _silu(cb_ref[...])

    d, tn = w_ref.shape
    for jt in range(tn // LANES):
        sl = slice(jt * LANES, (jt + 1) * LANES)
        w = w_ref[:, sl]
        for m in range(2):
            p = (w * s_scr[m]).reshape(d // 8, 8, LANES).sum(axis=0)
            o_ref[m:m + 1, sl] = p.sum(axis=0, keepdims=True) + b_ref[:, sl]


def _ada(c2, w_ada, b_ada):
    d, n = w_ada.shape
    tn = _tile(n, 1024, 512, LANES)
    cb = jnp.broadcast_to(c2[:, :, None], (2, d, LANES))
    return pl.pallas_call(
        _ada_kernel,
        grid=(n // tn,),
        in_specs=[pl.BlockSpec((2, d, LANES), lambda j: (0, 0, 0)),
                  pl.BlockSpec((d, tn), lambda j: (0, j)),
                  pl.BlockSpec((1, tn), lambda j: (0, j))],
        out_specs=pl.BlockSpec((2, tn), lambda j: (0, j)),
        out_shape=jax.ShapeDtypeStruct((2, n), F32),
        scratch_shapes=[pltpu.VMEM((2, d, LANES), F32)],
        compiler_params=_params("arbitrary"),
        name="ada",
    )(cb, w_ada, b_ada.reshape(1, n))


def _norm_mod(x, g, sh, sc):
    ms = jnp.mean(x * x, axis=-1, keepdims=True)
    y = x * lax.rsqrt(ms + EPS) * g
    return y * (1.0 + sc) + sh


def _proj_kernel(x_ref, g_ref, sh_ref, sc_ref, wl_ref, wa_ref, wb_ref, o_ref, low_ref, h_scr, *, na):
    j = pl.program_id(1)

    @pl.when(j == 0)
    def _():
        rows = min(256, h_scr.shape[0])
        for r in range(0, h_scr.shape[0], rows):
            h_scr[r:r + rows] = _norm_mod(x_ref[r:r + rows], g_ref[...], sh_ref[...], sc_ref[...]).astype(BF16)
        low_ref[...] = jnp.dot(h_scr[...], wl_ref[...], preferred_element_type=F32)

    @pl.when(j < na)
    def _():
        o_ref[...] = jnp.dot(h_scr[...], wa_ref[...], preferred_element_type=F32)

    @pl.when(j >= na)
    def _():
        o_ref[...] = jnp.dot(h_scr[...], wb_ref[...], preferred_element_type=F32)


def _proj(x, g, sh, sc, w_low, w, cols_a, wb):
    _, t, d = x.shape
    tm = _tile(t, 1024, 512, 256, 128)
    tn = _tile(cols_a, 512, 256)
    na, nb = cols_a // tn, wb.shape[1] // tn
    assert wb.shape[1] % tn == 0
    vec = pl.BlockSpec((1, d), lambda i, j: (0, 0))
    return pl.pallas_call(
        functools.partial(_proj_kernel, na=na),
        grid=(t // tm, na + nb),
        in_specs=[pl.BlockSpec((None, tm, d), lambda i, j: (0, i, 0), pipeline_mode=pl.Buffered(1)),
                  vec, vec, vec,
                  pl.BlockSpec((d, LANES), lambda i, j: (0, 0)),
                  pl.BlockSpec((d, tn), lambda i, j: (0, jnp.minimum(j, na - 1))),
                  pl.BlockSpec((d, tn), lambda i, j: (0, jnp.maximum(j - na, 0)))],
        out_specs=[pl.BlockSpec((tm, tn), lambda i, j: (i, j)),
                   pl.BlockSpec((tm, LANES), lambda i, j: (i, 0))],
        out_shape=[jax.ShapeDtypeStruct((t, (na + nb) * tn), F32),
                   jax.ShapeDtypeStruct((t, LANES), F32)],
        scratch_shapes=[pltpu.VMEM((tm, d), BF16)],
        compiler_params=_params("arbitrary", "arbitrary"),
        name="proj",
    )(x, g, sh, sc, w_low, w, wb)


def _gla_block(q_ref, k_ref, v_ref, low_ref, wg, bg, tri, s_t, o_ref, fwd, with_output):
    c = GLA_CHUNK
    nch = q_ref.shape[0] // c
    z = jnp.dot(low_ref[...].astype(BF16), wg, preferred_element_type=F32) + bg
    g = (jnp.minimum(z, 0.0) - jnp.log(1.0 + jnp.exp(-jnp.abs(z)))) * (1.0 / GATE_TAU)
    g1 = g.astype(BF16)
    r1 = g - g1.astype(F32)
    g2 = r1.astype(BF16)
    g3 = (r1 - g2.astype(F32)).astype(BF16)
    b3 = jnp.dot(tri, jnp.concatenate([g1, g2, g3], axis=1), preferred_element_type=F32)
    b = b3[:, :LANES] + b3[:, LANES:2 * LANES] + b3[:, 2 * LANES:]

    def chunk_rows(n):
        return slice(n * c, (n + 1) * c)

    def per_chunk(row):
        return jnp.concatenate([jnp.broadcast_to(b[n * c + row:n * c + row + 1], (c, LANES))
                                for n in range(nch)], axis=0)

    tot = per_chunk(c - 1 if fwd else 0)
    lane0 = lax.broadcasted_iota(I32, (c, LANES), 1) < GLA_HK

    def stack(a, n):
        an = a[chunk_rows(n)]
        return jnp.concatenate([jnp.where(lane0, an, 0.0), jnp.where(lane0, 0.0, an)], axis=0).astype(BF16)

    k = k_ref[...]
    ks = k * jnp.exp(tot - b)
    vs = [jnp.concatenate([v_ref[chunk_rows(n), :GLA_HV], v_ref[chunk_rows(n), GLA_HV:]],
                          axis=0).astype(BF16) for n in range(nch)]
    upd = [lax.dot_general(vs[n], stack(ks, n), TN_DIMS, preferred_element_type=F32) for n in range(nch)]
    decay = jnp.exp(tot)
    order = range(nch) if fwd else range(nch - 1, -1, -1)
    s_in = [None] * nch
    for n in order:
        s_in[n] = s_t
        s_t = s_t * decay[n * c:n * c + 1] + upd[n]
    if not with_output:
        return s_t
    mid = per_chunk(c // 2)
    qs = q_ref[...] * (GLA_HK ** -0.5)
    qa = qs * jnp.exp(b - mid)
    ka = k * jnp.exp(mid - b)
    qi = qs * jnp.exp(b)
    r2 = lax.broadcasted_iota(I32, (2 * c, 2 * c), 0)
    c2 = lax.broadcasted_iota(I32, (2 * c, 2 * c), 1)
    keep = ((r2 >= c) == (c2 >= c)) & ((c2 <= r2) if fwd else (c2 >= r2))
    for n in range(nch):
        att = lax.dot_general(stack(qa, n), stack(ka, n), NT_DIMS, preferred_element_type=F32)
        att = jnp.where(keep, att, 0.0).astype(BF16)
        o = jnp.dot(att, vs[n], preferred_element_type=F32)
        o = o + lax.dot_general(stack(qi, n), s_in[n].astype(BF16), NT_DIMS, preferred_element_type=F32)
        o_ref[chunk_rows(n)] = jnp.concatenate([o[:c], o[c:]], axis=1)
    return s_t


def _gla_kernel(*refs, with_output):
    (qf, kf, vf, lf, qb, kb, vb, lb, wgf, wgb, bgf, bgb, s0_ref) = refs[:13]
    if with_output:
        of_ref, ob_ref, s_scr, tri_scr = refs[13:]
    else:
        sout_ref, s_scr, tri_scr = refs[13:]
        of_ref = ob_ref = None
    i = pl.program_id(1)
    c = GLA_CHUNK
    rows = qf.shape[0]

    @pl.when((pl.program_id(0) == 0) & (i == 0))
    def _():
        r = lax.broadcasted_iota(I32, (rows, rows), 0)
        cc = lax.broadcasted_iota(I32, (rows, rows), 1)
        shift = c.bit_length() - 1
        same = jnp.right_shift(r, shift) == jnp.right_shift(cc, shift)
        tri_scr[0] = jnp.where(same & (cc <= r), 1.0, 0.0).astype(BF16)
        tri_scr[1] = jnp.where(same & (cc >= r), 1.0, 0.0).astype(BF16)

    @pl.when(i == 0)
    def _():
        s_scr[...] = s0_ref[:, 0]

    s_scr[0] = _gla_block(qf, kf, vf, lf, wgf[...], bgf[...], tri_scr[0], s_scr[0], of_ref, True, with_output)
    s_scr[1] = _gla_block(qb, kb, vb, lb, wgb[...], bgb[...], tri_scr[1], s_scr[1], ob_ref, False, with_output)
    if not with_output:
        @pl.when(i == pl.num_programs(1) - 1)
        def _():
            sout_ref[:, 0] = s_scr[...]


def _gla(proj, low, wg_f, wg_b, bg_f, bg_b, s0, with_output, col_q, col_k, col_v):
    t = proj.shape[0]
    cb = _tile(t // GLA_CHUNK, 8, 4, 2, 1)
    rows = cb * GLA_CHUNK
    nb = t // rows
    npairs = wg_f.shape[1] // (2 * GLA_HK)
    pw = 2 * GLA_HK
    vw = 2 * GLA_HV
    fi = lambda p, i: i
    bi = lambda p, i: nb - 1 - i

    def spec(width, col0, blk):
        return pl.BlockSpec((rows, width), lambda p, i: (blk(p, i), col0 // width + p))

    def low_spec(blk):
        return pl.BlockSpec((rows, LANES), lambda p, i: (blk(p, i), 0))

    in_specs = [spec(pw, col_q, fi), spec(pw, col_k, fi), spec(vw, col_v, fi), low_spec(fi),
                spec(pw, col_q, bi), spec(pw, col_k, bi), spec(vw, col_v, bi), low_spec(bi),
                pl.BlockSpec((LANES, pw), lambda p, i: (0, p)),
                pl.BlockSpec((LANES, pw), lambda p, i: (0, p)),
                pl.BlockSpec((1, pw), lambda p, i: (0, p)),
                pl.BlockSpec((1, pw), lambda p, i: (0, p)),
                pl.BlockSpec((2, 1, GLA_HV, pw), lambda p, i: (0, p, 0, 0))]
    if with_output:
        out_specs = [pl.BlockSpec((rows, vw), lambda p, i: (i, p)),
                     pl.BlockSpec((rows, vw), lambda p, i: (nb - 1 - i, p))]
        out_shape = [jax.ShapeDtypeStruct((t, npairs * 2 * GLA_HV), F32)] * 2
    else:
        out_specs = pl.BlockSpec((2, 1, GLA_HV, pw), lambda p, i: (0, p, 0, 0))
        out_shape = jax.ShapeDtypeStruct((2, npairs, GLA_HV, pw), F32)
    return pl.pallas_call(
        functools.partial(_gla_kernel, with_output=with_output),
        grid=(npairs, nb),
        in_specs=in_specs,
        out_specs=out_specs,
        out_shape=out_shape,
        scratch_shapes=[pltpu.VMEM((2, GLA_HV, pw), F32), pltpu.VMEM((2, rows, rows), BF16)],
        compiler_params=_params("arbitrary", "arbitrary"),
        name="gla_out" if with_output else "gla_state",
    )(proj, proj, proj, low, proj, proj, proj, low, wg_f, wg_b, bg_f, bg_b, s0)


def _mix_kernel(of_ref, ob_ref, r_ref, ca_ref, cb_ref, gout_ref, wdw_ref, bdw_ref, gln_ref, bln_ref, pick_ref,
                o_ref, upad, y_scr):
    tm = of_ref.shape[0]
    dg = of_ref.shape[1]
    dc = ca_ref.shape[1]
    nrow = tm // GRID_W
    lead = CONV_LEAD
    for h in range(dg // GLA_HV):
        sl = slice(h * GLA_HV, (h + 1) * GLA_HV)
        o = of_ref[:, sl] + ob_ref[:, sl]
        o = o * lax.rsqrt(jnp.mean(o * o, axis=-1, keepdims=True) + EPS) * gout_ref[:, sl]
        o_ref[:, sl] = (o * _silu(r_ref[:, sl])).astype(o_ref.dtype)
    for r in range(nrow):
        rs = slice(r * GRID_W, (r + 1) * GRID_W)
        upad[r, 0:lead, :] = jnp.zeros((lead, dc), F32)
        upad[r, lead:lead + GRID_W, :] = ca_ref[rs, :] * _sigmoid(cb_ref[rs, :])
        upad[r, lead + GRID_W:, :] = jnp.zeros((lead, dc), F32)
    cw = 256
    span = upad.shape[1]
    kpad = pick_ref.shape[1] - CONV_WIDTH * span

    def row_body(r, carry):
        for cc in range(dc // cw):
            cs = slice(cc * cw, (cc + 1) * cw)
            u = upad[r, :, cs]
            taps = [(wdw_ref[j:j + 1, cs] * u).astype(BF16) for j in range(CONV_WIDTH)]
            taps.append(jnp.zeros((kpad, cw), BF16))
            y_scr[r, :, cs] = (jnp.dot(pick_ref[...], jnp.concatenate(taps, axis=0), preferred_element_type=F32)
                               + bdw_ref[:, cs])
        return carry

    lax.fori_loop(0, nrow, row_body, 0)
    for r in range(nrow):
        y = y_scr[r]
        mu = jnp.mean(y, axis=-1, keepdims=True)
        yc = y - mu
        var = jnp.mean(yc * yc, axis=-1, keepdims=True)
        yn = yc * lax.rsqrt(var + EPS) * gln_ref[...] + bln_ref[...]
        o_ref[r * GRID_W:(r + 1) * GRID_W, dg:] = _silu(yn).astype(o_ref.dtype)


def _mix(o_f, o_b, proj, g_out, w_dw, b_dw, g_ln, b_ln, col_r, col_ca, col_cb):
    t, dg = o_f.shape
    tm = _tile(t, 256, 128, GRID_W)
    dc = w_dw.shape[1]
    nrow = tm // GRID_W
    span = GRID_W + 2 * CONV_LEAD
    kdim = -(-CONV_WIDTH * span // LANES) * LANES
    pick = np.zeros((GRID_W, kdim), np.float32)
    for j in range(CONV_WIDTH):
        pick[np.arange(GRID_W), j * span + np.arange(GRID_W) + (CONV_LEAD - CONV_PAD + j)] = 1.0
    pick = jnp.asarray(pick, BF16)
    row = lambda width: pl.BlockSpec((1, width), lambda i: (0, 0))
    return pl.pallas_call(
        _mix_kernel,
        grid=(t // tm,),
        in_specs=[pl.BlockSpec((tm, dg), lambda i: (i, 0)),
                  pl.BlockSpec((tm, dg), lambda i: (i, 0)),
                  pl.BlockSpec((tm, dg), lambda i: (i, col_r // dg)),
                  pl.BlockSpec((tm, dc), lambda i: (i, col_ca // dc)),
                  pl.BlockSpec((tm, dc), lambda i: (i, col_cb // dc)),
                  row(dg),
                  pl.BlockSpec((CONV_WIDTH, dc), lambda i: (0, 0)),
                  row(dc), row(dc), row(dc),
                  pl.BlockSpec((GRID_W, kdim), lambda i: (0, 0))],
        out_specs=pl.BlockSpec((tm, dg + dc), lambda i: (i, 0)),
        out_shape=jax.ShapeDtypeStruct((t, dg + dc), BF16),
        scratch_shapes=[pltpu.VMEM((nrow, span, dc), F32),
                        pltpu.VMEM((nrow, GRID_W, dc), F32)],
        compiler_params=_params("arbitrary"),
        name="mix",
    )(o_f, o_b, proj, proj, proj, g_out, w_dw, b_dw, g_ln, b_ln, pick)


def _outproj_kernel(m_ref, w_ref, x_ref, ga_ref, o_ref):
    o_ref[...] = x_ref[...] + ga_ref[...] * jnp.dot(m_ref[...], w_ref[...], preferred_element_type=F32)


def _outproj(mix, w, x, ga):
    t, k = mix.shape
    n = w.shape[1]
    tm = _tile(t, 1024, 512, 256)
    tn = _tile(n, 1024, 512, 256, LANES)
    return pl.pallas_call(
        _outproj_kernel,
        grid=(t // tm, n // tn),
        in_specs=[pl.BlockSpec((tm, k), lambda i, j: (i, 0)),
                  pl.BlockSpec((k, tn), lambda i, j: (0, j)),
                  pl.BlockSpec((None, tm, tn), lambda i, j: (0, i, j)),
                  pl.BlockSpec((1, tn), lambda i, j: (0, j))],
        out_specs=pl.BlockSpec((tm, tn), lambda i, j: (i, j)),
        out_shape=jax.ShapeDtypeStruct((t, n), F32),
        compiler_params=_params("arbitrary", "arbitrary"),
        name="out_proj",
    )(mix, w, x, ga)


def _pack_halves(h):
    c = h.shape[1] // 2
    lo = lax.bitcast_convert_type(h[:, :c].astype(BF16).astype(F32), U32)
    hi = lax.bitcast_convert_type(h[:, c:].astype(BF16).astype(F32), U32)
    return (lo >> 16) | (hi & jnp.uint32(0xFFFF0000))


def _unpack_halves(p):
    lo = lax.bitcast_convert_type(p << 16, F32).astype(BF16)
    hi = lax.bitcast_convert_type(p & jnp.uint32(0xFFFF0000), F32).astype(BF16)
    return jnp.concatenate([lo, hi], axis=1)


def _route_kernel(x_ref, g_ref, sh_ref, sc_ref, wr_ref, br_ref, hp_ref, e_ref, w_ref, rk_ref, cnt_ref,
                  carry):
    tm = x_ref.shape[0]
    ne = N_EXPERTS
    gs = ne // N_GROUPS
    neg = -jnp.inf

    @pl.when(pl.program_id(0) == 0)
    def _():
        carry[...] = jnp.zeros_like(carry)

    h = _norm_mod(x_ref[...], g_ref[...], sh_ref[...], sc_ref[...])
    hp_ref[...] = _pack_halves(h)
    logits = lax.dot_general(wr_ref[...], h, NT_DIMS, preferred_element_type=F32,
                             precision=lax.Precision.HIGHEST)
    scores = _sigmoid(logits)
    sel = scores + br_ref[...]
    sel3 = sel.reshape(N_GROUPS, gs, tm)
    sub = lax.broadcasted_iota(I32, (N_GROUPS, gs, tm), 1)
    m1 = jnp.max(sel3, axis=1, keepdims=True)
    i1 = jnp.min(jnp.where(sel3 == m1, sub, gs), axis=1, keepdims=True)
    m2 = jnp.max(jnp.where(sub == i1, neg, sel3), axis=1, keepdims=True)
    gscore = (m1 + m2).reshape(N_GROUPS, tm)
    gid = lax.broadcasted_iota(I32, (N_GROUPS, tm), 0)
    gmask = jnp.zeros((N_GROUPS, tm), F32)
    for _ in range(TOPK_GROUPS):
        mx = jnp.max(gscore, axis=0, keepdims=True)
        idx = jnp.min(jnp.where(gscore == mx, gid, N_GROUPS), axis=0, keepdims=True)
        pick = gid == idx
        gmask = jnp.where(pick, 1.0, gmask)
        gscore = jnp.where(pick, neg, gscore)
    emask = jnp.broadcast_to(gmask.reshape(N_GROUPS, 1, tm), (N_GROUPS, gs, tm)).reshape(ne, tm)
    cand = jnp.where(emask > 0.0, sel, neg)
    eid = lax.broadcasted_iota(I32, (ne, tm), 0)
    chosen = jnp.zeros((ne, tm), F32)
    idxs, scs = [], []
    for _ in range(TOP_K):
        mx = jnp.max(cand, axis=0, keepdims=True)
        idx = jnp.min(jnp.where(cand == mx, eid, ne), axis=0, keepdims=True)
        pick = eid == idx
        idxs.append(idx)
        scs.append(jnp.sum(jnp.where(pick, scores, 0.0), axis=0, keepdims=True))
        chosen = jnp.where(pick, 1.0, chosen)
        cand = jnp.where(pick, neg, cand)
    ssum = scs[0]
    for s in scs[1:]:
        ssum = ssum + s
    before = (lax.broadcasted_iota(I32, (tm, tm), 0) < lax.broadcasted_iota(I32, (tm, tm), 1))
    prior = jnp.dot(chosen.astype(BF16), before.astype(BF16), preferred_element_type=F32) + carry[...]
    for k in range(TOP_K):
        e_ref[k:k + 1, :] = idxs[k]
        w_ref[k:k + 1, :] = scs[k] / ssum * ROUTED_SCALE
        rk = jnp.sum(jnp.where(eid == idxs[k], prior, 0.0), axis=0, keepdims=True)
        rk_ref[k:k + 1, :] = rk.astype(I32)
    carry[...] = carry[...] + jnp.sum(chosen, axis=1, keepdims=True)
    cnt_ref[...] = jnp.broadcast_to(carry[...], cnt_ref.shape).astype(I32)


def _route(x, g, sh, sc, w_router_t, b_router):
    t, d = x.shape
    tm = _tile(t, 256, LANES)
    vec = pl.BlockSpec((1, d), lambda i: (0, 0))
    tok = lambda dt: jax.ShapeDtypeStruct((TOP_K, t), dt)
    tok_spec = pl.BlockSpec((TOP_K, tm), lambda i: (0, i))
    return pl.pallas_call(
        _route_kernel,
        grid=(t // tm,),
        in_specs=[pl.BlockSpec((tm, d), lambda i: (i, 0)), vec, vec, vec,
                  pl.BlockSpec((N_EXPERTS, d), lambda i: (0, 0)),
                  pl.BlockSpec((N_EXPERTS, 1), lambda i: (0, 0))],
        out_specs=[pl.BlockSpec((tm, d // 2), lambda i: (i, 0)), tok_spec, tok_spec, tok_spec,
                   pl.BlockSpec((N_EXPERTS, LANES), lambda i: (0, 0))],
        out_shape=[jax.ShapeDtypeStruct((t, d // 2), U32), tok(I32), tok(F32), tok(I32),
                   jax.ShapeDtypeStruct((N_EXPERTS, LANES), I32)],
        scratch_shapes=[pltpu.VMEM((N_EXPERTS, 1), F32)],
        compiler_params=_params("arbitrary"),
        name="route",
    )(x, g, sh, sc, w_router_t, b_router)


WUNITS = 8
WSLOTS = 4
XSLOTS = 4


def _experts_kernel(blk_run, blk_goal, run_e, meta, slot_tok, hp_hbm, wg_hbm, wu_hbm, wd_hbm, o_ref,
                    xbuf, sem, sg, su, sd, wsem, wgb, wub, wdb, done_ref):
    b = pl.program_id(0)
    nb = pl.num_programs(0)
    nused = meta[0]
    nunits = meta[1] * WUNITS
    ug = sg.shape[1]
    ud = sd.shape[1]

    def unit_copies(u):
        e = run_e[u // WUNITS]
        k = u % WUNITS
        s = u % WSLOTS
        return (pltpu.make_async_copy(wg_hbm.at[e, pl.ds(k * ug, ug)], sg.at[s], wsem.at[s]),
                pltpu.make_async_copy(wu_hbm.at[e, pl.ds(k * ug, ug)], su.at[s], wsem.at[s]),
                pltpu.make_async_copy(wd_hbm.at[e, pl.ds(k * ud, ud)], sd.at[s], wsem.at[s]))

    def start_unit(u):
        for cp in unit_copies(u):
            cp.start(priority=1)

    def process_unit(u):
        for cp in unit_copies(u):
            cp.wait()

        @pl.when(u + WSLOTS - 1 < nunits)
        def _():
            start_unit(u + WSLOTS - 1)

        r = (u // WUNITS) % 2
        k = u % WUNITS
        s = u % WSLOTS
        wgb[r, pl.ds(pl.multiple_of(k * ug, ug), ug), :] = sg[s].astype(BF16)
        wub[r, pl.ds(pl.multiple_of(k * ug, ug), ug), :] = su[s].astype(BF16)
        wdb[r, pl.ds(pl.multiple_of(k * ud, ud), ud), :] = sd[s].astype(BF16)

    def start_gather(blk, slot, rows=range(MOE_BLOCK)):
        for r in rows:
            tok = slot_tok[blk * MOE_BLOCK + r]
            pltpu.make_async_copy(hp_hbm.at[pl.ds(tok, 1)], xbuf.at[slot, pl.ds(r, 1)],
                                  sem.at[slot]).start(priority=0)

    def wait_gather(slot):
        pltpu.make_async_copy(hp_hbm.at[pl.ds(0, MOE_BLOCK)], xbuf.at[slot], sem.at[slot]).wait()

    live = b < nused
    slot = lax.rem(b, XSLOTS)

    @pl.when((b == 0) & live)
    def _():
        done_ref[0] = 0
        for u in range(WSLOTS - 1):
            @pl.when(u < nunits)
            def _():
                start_unit(u)
        for blk in range(XSLOTS - 1):
            @pl.when(blk < nused)
            def _():
                start_gather(blk, blk)

    @pl.when(live)
    def _():
        done = done_ref[0]
        goal = blk_goal[b]

        def body(n, carry):
            process_unit(done + n)
            return carry
        lax.fori_loop(0, jnp.maximum(goal - done, 0), body, 0)
        done_ref[0] = jnp.maximum(goal, done)

    ahead = b + XSLOTS - 1
    ahead_live = (ahead < nb) & (ahead < nused)

    def compute(m, gather_ahead):
        r = blk_run[b] % 2
        half = xbuf.shape[2]
        group = MOE_BLOCK // 4

        def issue(q):
            if gather_ahead:
                start_gather(ahead, (m + XSLOTS - 1) % XSLOTS, range(q * group, (q + 1) * group))

        wait_gather(m)
        x = _unpack_halves(xbuf[m])
        issue(0)
        hg = jnp.dot(x, wgb[r], preferred_element_type=F32)
        issue(1)
        hu = jnp.dot(x, wub[r], preferred_element_type=F32)
        issue(2)
        hb = (_silu(hg) * hu).astype(BF16)
        for c in range(2):
            lo = slice(c * half // 2, (c + 1) * half // 2)
            hi = slice(half + c * half // 2, half + (c + 1) * half // 2)
            o = jnp.concatenate([jnp.dot(hb, wdb[r, :, lo], preferred_element_type=F32),
                                 jnp.dot(hb, wdb[r, :, hi], preferred_element_type=F32)], axis=1)
            o_ref[:, lo] = _pack_halves(o)
            if c == 0:
                issue(3)

    for m in range(XSLOTS):
        mine = live & (slot == m)
        pl.when(mine & ahead_live)(functools.partial(compute, m, True))
        pl.when(mine & jnp.logical_not(ahead_live))(functools.partial(compute, m, False))

    @pl.when(jnp.logical_not(live))
    def _():
        o_ref[...] = jnp.zeros_like(o_ref)


def _experts(blk_run, blk_goal, run_e, meta, slot_tok, hp, wg, wu, wd):
    nb = blk_run.shape[0]
    _, d, de = wg.shape
    ug, ud = d // WUNITS, de // WUNITS
    hbm = pl.BlockSpec(memory_space=pl.ANY)
    grid_spec = pltpu.PrefetchScalarGridSpec(
        num_scalar_prefetch=5,
        grid=(nb,),
        in_specs=[hbm, hbm, hbm, hbm],
        out_specs=pl.BlockSpec((MOE_BLOCK, d // 2), lambda b, *_: (b, 0)),
        scratch_shapes=[pltpu.VMEM((XSLOTS, MOE_BLOCK, d // 2), U32), pltpu.SemaphoreType.DMA((XSLOTS,)),
                        pltpu.VMEM((WSLOTS, ug, de), F32), pltpu.VMEM((WSLOTS, ug, de), F32),
                        pltpu.VMEM((WSLOTS, ud, d), F32), pltpu.SemaphoreType.DMA((WSLOTS,)),
                        pltpu.VMEM((2, d, de), BF16), pltpu.VMEM((2, d, de), BF16),
                        pltpu.VMEM((2, de, d), BF16), pltpu.SMEM((1,), I32)],
    )
    return pl.pallas_call(
        _experts_kernel,
        grid_spec=grid_spec,
        out_shape=jax.ShapeDtypeStruct((nb * MOE_BLOCK, d // 2), U32),
        compiler_params=_params("arbitrary"),
        name="experts",
    )(blk_run, blk_goal, run_e, meta, slot_tok, hp, wg, wu, wd)


def _shared_kernel(hp_ref, wg_ref, wu_ref, wd_ref, x_ref, ga_ref, o_ref):
    x = _unpack_halves(hp_ref[...])
    hg = jnp.dot(x, wg_ref[...], preferred_element_type=F32)
    hu = jnp.dot(x, wu_ref[...], preferred_element_type=F32)
    hb = (_silu(hg) * hu).astype(BF16)
    o_ref[...] = x_ref[...] + ga_ref[...] * jnp.dot(hb, wd_ref[...], preferred_element_type=F32)


def _shared(hp, wg, wu, wd, x, ga):
    t, d = x.shape
    tm = _tile(t, 256, LANES)
    ds_ = wg.shape[1]
    return pl.pallas_call(
        _shared_kernel,
        grid=(t // tm,),
        in_specs=[pl.BlockSpec((tm, d // 2), lambda i: (i, 0)),
                  pl.BlockSpec((d, ds_), lambda i: (0, 0)),
                  pl.BlockSpec((d, ds_), lambda i: (0, 0)),
                  pl.BlockSpec((ds_, d), lambda i: (0, 0)),
                  pl.BlockSpec((tm, d), lambda i: (i, 0)),
                  pl.BlockSpec((1, d), lambda i: (0, 0))],
        out_specs=pl.BlockSpec((tm, d), lambda i: (i, 0)),
        out_shape=jax.ShapeDtypeStruct((t, d), F32),
        compiler_params=_params("arbitrary"),
        name="shared",
    )(hp, wg, wu, wd, x, ga)


def _combine_kernel(dest, ys_hbm, w_ref, base_ref, ga_ref, gf_ref, o_ref, buf_a, buf_b, sem):
    i = pl.program_id(0)
    n = pl.num_programs(0)
    tm = buf_a.shape[1]
    half = buf_a.shape[2]
    d = 2 * half

    def start_gather(tile, buf, s):
        for j in range(tm * TOP_K):
            src = dest[tile * (tm * TOP_K) + j]
            pltpu.make_async_copy(ys_hbm.at[pl.ds(src, 1)], buf.at[j % TOP_K, pl.ds(j // TOP_K, 1)],
                                  sem.at[s]).start(priority=j % 2)

    def wait_gather(buf, s):
        for k in range(TOP_K):
            pltpu.make_async_copy(ys_hbm.at[pl.ds(0, tm)], buf.at[k], sem.at[s]).wait()

    def finish(buf, rows):
        ylo = jnp.zeros((tm, half), F32)
        yhi = jnp.zeros((tm, half), F32)
        for k in range(TOP_K):
            p = buf[k]
            w = w_ref[rows, k:k + 1]
            ylo = ylo + w * lax.bitcast_convert_type(p << 16, F32)
            yhi = yhi + w * lax.bitcast_convert_type(p & jnp.uint32(0xFFFF0000), F32)
        xlo = base_ref[rows, :half] + ga_ref[:, :half] * ylo
        xhi = base_ref[rows, half:] + ga_ref[:, half:] * yhi
        ms = (jnp.sum(xlo * xlo, axis=-1, keepdims=True) + jnp.sum(xhi * xhi, axis=-1, keepdims=True)) / d
        inv = lax.rsqrt(ms + EPS)
        o_ref[rows, :half] = xlo * inv * gf_ref[:, :half]
        o_ref[rows, half:] = xhi * inv * gf_ref[:, half:]

    def step(gather_next):
        wait_gather(buf_a, 0)
        start_gather(2 * i + 1, buf_b, 1)
        finish(buf_a, slice(0, tm))
        wait_gather(buf_b, 1)
        if gather_next:
            start_gather(2 * i + 2, buf_a, 0)
        finish(buf_b, slice(tm, 2 * tm))

    @pl.when(i == 0)
    def _():
        start_gather(0, buf_a, 0)

    pl.when(i + 1 < n)(functools.partial(step, True))
    pl.when(i + 1 == n)(functools.partial(step, False))


def _combine(dest, ys, wts, base, ga, g_final):
    t, d = base.shape
    tm = 64
    assert t % (2 * tm) == 0
    grid_spec = pltpu.PrefetchScalarGridSpec(
        num_scalar_prefetch=1,
        grid=(t // (2 * tm),),
        in_specs=[pl.BlockSpec(memory_space=pl.ANY),
                  pl.BlockSpec((2 * tm, TOP_K), lambda i, s: (i, 0)),
                  pl.BlockSpec((2 * tm, d), lambda i, s: (i, 0)),
                  pl.BlockSpec((1, d), lambda i, s: (0, 0)),
                  pl.BlockSpec((1, d), lambda i, s: (0, 0))],
        out_specs=pl.BlockSpec((None, 2 * tm, d), lambda i, s: (0, i, 0)),
        scratch_shapes=[pltpu.VMEM((TOP_K, tm, d // 2), U32), pltpu.VMEM((TOP_K, tm, d // 2), U32),
                        pltpu.SemaphoreType.DMA((2,))],
    )
    return pl.pallas_call(
        _combine_kernel,
        grid_spec=grid_spec,
        out_shape=jax.ShapeDtypeStruct((1, t, d), F32),
        compiler_params=_params("arbitrary"),
        name="combine",
    )(dest, ys, wts, base, ga, g_final)


def kernel(x, c, ctx, c_ctx, w_ada, b_ada, g_norm_mix, g_norm_ffn, w_in, w_gate_up, b_gate_up,
           g_gla_out, w_dw, b_dw, g_conv_ln, b_conv_ln, w_out, w_router, b_router,
           w_e_gate, w_e_up, w_e_down, w_s_gate, w_s_up, w_s_down, g_final):
    assert x.shape[0] == 1 and w_ada.shape[0] == 1, "single batch element, single layer"
    t, d = x.shape[1], x.shape[2]
    dk = w_gate_up.shape[3]
    heads = dk // GLA_HK
    dg = heads * GLA_HV
    dc = w_dw.shape[2]

    col_q, col_k, col_v = 0, dk, 2 * dk
    col_r = col_v + dg
    col_ca = col_r + dg
    col_cb = col_ca + dc
    n_gate = 2 * GATE_RANK
    w16 = w_in[0].astype(BF16)
    w_b = w16[:, col_ca + n_gate:]
    w_low = jnp.pad(w16[:, col_ca:col_ca + n_gate], ((0, 0), (0, LANES - n_gate)))

    mod = _ada(jnp.stack([c[0], c_ctx]), w_ada[0], b_ada[0])
    sh_m, sc_m, ga_m, sh_f, sc_f, ga_f = [mod[0:1, k * d:(k + 1) * d] for k in range(6)]
    csh_m, csc_m = mod[1:2, 0:d], mod[1:2, d:2 * d]

    wg_f = jnp.zeros((LANES, dk), F32).at[:GATE_RANK].set(w_gate_up[0, 0]).astype(BF16)
    wg_b = jnp.zeros((LANES, dk), F32).at[GATE_RANK:n_gate].set(w_gate_up[0, 1]).astype(BF16)
    bg_f = b_gate_up[0, 0:1]
    bg_b = b_gate_up[0, 1:2]
    cols = dict(col_q=col_q, col_k=col_k, col_v=col_v)

    gn_mix = g_norm_mix[0:1]
    proj_c, low_c = _proj(ctx, gn_mix, csh_m, csc_m, w_low, w16, col_ca, w_b)
    s0 = jnp.zeros((2, heads // 2, GLA_HV, 2 * GLA_HK), F32)
    s_ctx = _gla(proj_c, low_c, wg_f, wg_b, bg_f, bg_b, s0, with_output=False, **cols)

    proj_l, low_l = _proj(x, gn_mix, sh_m, sc_m, w_low, w16, col_ca, w_b)
    o_f, o_b = _gla(proj_l, low_l, wg_f, wg_b, bg_f, bg_b, s_ctx, with_output=True, **cols)
    mix = _mix(o_f, o_b, proj_l, g_gla_out[0:1], w_dw[0], b_dw[0:1], g_conv_ln[0:1], b_conv_ln[0:1],
               col_r, col_ca, col_cb)
    x1 = _outproj(mix, w_out[0].astype(BF16), x, ga_m)

    hp, eidx, wts, rank, cnt = _route(x1, g_norm_ffn[0:1], sh_f, sc_f, w_router[0].T,
                                      b_router[0].reshape(N_EXPERTS, 1))
    nk = t * TOP_K
    nb = (nk + MOE_BLOCK - 1) // MOE_BLOCK + N_EXPERTS
    counts = cnt[:, 0]
    pcounts = (counts + MOE_BLOCK - 1) // MOE_BLOCK * MOE_BLOCK
    pend = jnp.cumsum(pcounts)
    pstart = pend - pcounts
    experts = jnp.arange(N_EXPERTS, dtype=I32)
    pstart_tok = jnp.sum(jnp.where(eidx[:, :, None] == experts, pstart, 0), axis=-1)
    dest = (pstart_tok + rank).T.reshape(-1).astype(I32)
    tok_ids = jnp.repeat(jnp.arange(t, dtype=I32), TOP_K)
    slot_tok = jnp.zeros((nb * MOE_BLOCK,), I32).at[dest].set(tok_ids, unique_indices=True,
                                                               mode="promise_in_bounds")
    blk_start = jnp.arange(nb, dtype=I32) * MOE_BLOCK
    blk_e = jnp.minimum(jnp.sum(pend[None, :] <= blk_start[:, None], axis=1), N_EXPERTS - 1).astype(I32)
    owns = counts > 0
    run_of_e = jnp.cumsum(owns.astype(I32)) - 1
    run_e = jnp.sum(jnp.where(owns[None, :] & (run_of_e[None, :] == experts[:, None]), experts[None, :], 0),
                    axis=1).astype(I32)
    onehot = blk_e[:, None] == experts[None, :]
    blk_run = jnp.sum(jnp.where(onehot, run_of_e[None, :], 0), axis=1).astype(I32)
    nruns = jnp.sum(owns.astype(I32))
    meta = jnp.stack([pend[-1] // MOE_BLOCK, nruns]).astype(I32)
    blk_first = jnp.sum(jnp.where(onehot, pstart[None, :], 0), axis=1) // MOE_BLOCK
    run_len = jnp.maximum(jnp.sum(jnp.where(onehot, pcounts[None, :], 0), axis=1) // MOE_BLOCK, 1)
    pos = jnp.arange(nb, dtype=I32) - blk_first
    share = (WUNITS * (pos + 1) + run_len - 1) // run_len
    blk_goal = jnp.minimum(WUNITS * (blk_run + 1) + share, WUNITS * nruns).astype(I32)

    ys = _experts(blk_run, blk_goal, run_e, meta, slot_tok, hp, w_e_gate[0], w_e_up[0], w_e_down[0])
    base = _shared(hp, w_s_gate[0].astype(BF16), w_s_up[0].astype(BF16), w_s_down[0].astype(BF16), x1, ga_f)
    return _combine(dest, ys, wts.T, base, ga_f, g_final.reshape(1, d))
```

```python
import functools

import jax
import jax.numpy as jnp
import numpy as np
from jax import lax
from jax.experimental import pallas as pl
from jax.experimental.pallas import tpu as pltpu

F32 = jnp.float32
BF16 = jnp.bfloat16
I32 = jnp.int32
U32 = jnp.uint32

EPS = 1e-6
LANES = 128
GLA_HK = 64
GLA_HV = 128
GLA_CHUNK = 64
GATE_RANK = 16
GATE_TAU = 16.0
CONV_WIDTH = 31
CONV_PAD = (CONV_WIDTH - 1) // 2
CONV_LEAD = 16
GRID_W = 64
N_EXPERTS = 64
N_GROUPS = 8
TOPK_GROUPS = 4
TOP_K = 8
ROUTED_SCALE = 2.5
MOE_BLOCK = 128
VMEM_LIMIT = 56 * 1024 * 1024

NT_DIMS = (((1,), (1,)), ((), ()))
TN_DIMS = (((0,), (0,)), ((), ()))


def _params(*sem):
    return pltpu.CompilerParams(dimension_semantics=sem, vmem_limit_bytes=VMEM_LIMIT)


def _tile(n, *preferred):
    for p in preferred:
        if n % p == 0:
            return p
    raise ValueError(f"no tile for {n} among {preferred}")


def _sigmoid(x):
    return 1.0 / (1.0 + jnp.exp(-x))


def _silu(x):
    return x * _sigmoid(x)


def _ada_kernel(cb_ref, w_ref, b_ref, o_ref, s_scr):
    @pl.when(pl.program_id(0) == 0)
    def _():
        s_scr[...] = _silu(cb_ref[...])

    d, tn = w_ref.shape
    for jt in range(tn // LANES):
        sl = slice(jt * LANES, (jt + 1) * LANES)
        w = w_ref[:, sl]
        for m in range(2):
            p = (w * s_scr[m]).reshape(d // 8, 8, LANES).sum(axis=0)
            o_ref[m:m + 1, sl] = p.sum(axis=0, keepdims=True) + b_ref[:, sl]


def _ada(c2, w_ada, b_ada):
    d, n = w_ada.shape
    tn = _tile(n, 1024, 512, LANES)
    cb = jnp.broadcast_to(c2[:, :, None], (2, d, LANES))
    return pl.pallas_call(
        _ada_kernel,
        grid=(n // tn,),
        in_specs=[pl.BlockSpec((2, d, LANES), lambda j: (0, 0, 0)),
                  pl.BlockSpec((d, tn), lambda j: (0, j)),
                  pl.BlockSpec((1, tn), lambda j: (0, j))],
        out_specs=pl.BlockSpec((2, tn), lambda j: (0, j)),
        out_shape=jax.ShapeDtypeStruct((2, n), F32),
        scratch_shapes=[pltpu.VMEM((2, d, LANES), F32)],
        compiler_params=_params("arbitrary"),
        name="ada",
    )(cb, w_ada, b_ada.reshape(1, n))


def _norm_mod(x, g, sh, sc):
    ms = jnp.mean(x * x, axis=-1, keepdims=True)
    y = x * lax.rsqrt(ms + EPS) * g
    return y * (1.0 + sc) + sh


def _proj_kernel(x_ref, g_ref, sh_ref, sc_ref, wl_ref, wa_ref, wb_ref, o_ref, low_ref, h_scr, *, na):
    j = pl.program_id(1)

    @pl.when(j == 0)
    def _():
        rows = min(256, h_scr.shape[0])
        for r in range(0, h_scr.shape[0], rows):
            h_scr[r:r + rows] = _norm_mod(x_ref[r:r + rows], g_ref[...], sh_ref[...], sc_ref[...]).astype(BF16)
        low_ref[...] = jnp.dot(h_scr[...], wl_ref[...], preferred_element_type=F32)

    @pl.when(j < na)
    def _():
        o_ref[...] = jnp.dot(h_scr[...], wa_ref[...], preferred_element_type=F32)

    @pl.when(j >= na)
    def _():
        o_ref[...] = jnp.dot(h_scr[...], wb_ref[...], preferred_element_type=F32)


def _proj(x, g, sh, sc, w_low, w, cols_a, wb):
    _, t, d = x.shape
    tm = _tile(t, 1024, 512, 256, 128)
    tn = _tile(cols_a, 512, 256)
    na, nb = cols_a // tn, wb.shape[1] // tn
    assert wb.shape[1] % tn == 0
    vec = pl.BlockSpec((1, d), lambda i, j: (0, 0))
    return pl.pallas_call(
        functools.partial(_proj_kernel, na=na),
        grid=(t // tm, na + nb),
        in_specs=[pl.BlockSpec((None, tm, d), lambda i, j: (0, i, 0), pipeline_mode=pl.Buffered(1)),
                  vec, vec, vec,
                  pl.BlockSpec((d, LANES), lambda i, j: (0, 0)),
                  pl.BlockSpec((d, tn), lambda i, j: (0, jnp.minimum(j, na - 1))),
                  pl.BlockSpec((d, tn), lambda i, j: (0, jnp.maximum(j - na, 0)))],
        out_specs=[pl.BlockSpec((tm, tn), lambda i, j: (i, j)),
                   pl.BlockSpec((tm, LANES), lambda i, j: (i, 0))],
        out_shape=[jax.ShapeDtypeStruct((t, (na + nb) * tn), F32),
                   jax.ShapeDtypeStruct((t, LANES), F32)],
        scratch_shapes=[pltpu.VMEM((tm, d), BF16)],
        compiler_params=_params("arbitrary", "arbitrary"),
        name="proj",
    )(x, g, sh, sc, w_low, w, wb)


def _gla_block(q_ref, k_ref, v_ref, low_ref, wg, bg, tri, s_t, o_ref, fwd, with_output):
    c = GLA_CHUNK
    nch = q_ref.shape[0] // c
    z = jnp.dot(low_ref[...].astype(BF16), wg, preferred_element_type=F32) + bg
    g = (jnp.minimum(z, 0.0) - jnp.log(1.0 + jnp.exp(-jnp.abs(z)))) * (1.0 / GATE_TAU)
    g1 = g.astype(BF16)
    r1 = g - g1.astype(F32)
    g2 = r1.astype(BF16)
    g3 = (r1 - g2.astype(F32)).astype(BF16)
    b3 = jnp.dot(tri, jnp.concatenate([g1, g2, g3], axis=1), preferred_element_type=F32)
    b = b3[:, :LANES] + b3[:, LANES:2 * LANES] + b3[:, 2 * LANES:]

    def chunk_rows(n):
        return slice(n * c, (n + 1) * c)

    def per_chunk(row):
        return jnp.concatenate([jnp.broadcast_to(b[n * c + row:n * c + row + 1], (c, LANES))
                                for n in range(nch)], axis=0)

    tot = per_chunk(c - 1 if fwd else 0)
    lane0 = lax.broadcasted_iota(I32, (c, LANES), 1) < GLA_HK

    def stack(a, n):
        an = a[chunk_rows(n)]
        return jnp.concatenate([jnp.where(lane0, an, 0.0), jnp.where(lane0, 0.0, an)], axis=0).astype(BF16)

    k = k_ref[...]
    ks = k * jnp.exp(tot - b)
    vs = [jnp.concatenate([v_ref[chunk_rows(n), :GLA_HV], v_ref[chunk_rows(n), GLA_HV:]],
                          axis=0).astype(BF16) for n in range(nch)]
    upd = [lax.dot_general(vs[n], stack(ks, n), TN_DIMS, preferred_element_type=F32) for n in range(nch)]
    decay = jnp.exp(tot)
    order = range(nch) if fwd else range(nch - 1, -1, -1)
    s_in = [None] * nch
    for n in order:
        s_in[n] = s_t
        s_t = s_t * decay[n * c:n * c + 1] + upd[n]
    if not with_output:
        return s_t
    mid = per_chunk(c // 2)
    qs = q_ref[...] * (GLA_HK ** -0.5)
    qa = qs * jnp.exp(b - mid)
    ka = k * jnp.exp(mid - b)
    qi = qs * jnp.exp(b)
    r2 = lax.broadcasted_iota(I32, (2 * c, 2 * c), 0)
    c2 = lax.broadcasted_iota(I32, (2 * c, 2 * c), 1)
    keep = ((r2 >= c) == (c2 >= c)) & ((c2 <= r2) if fwd else (c2 >= r2))
    for n in range(nch):
        att = lax.dot_general(stack(qa, n), stack(ka, n), NT_DIMS, preferred_element_type=F32)
        att = jnp.where(keep, att, 0.0).astype(BF16)
        o = jnp.dot(att, vs[n], preferred_element_type=F32)
        o = o + lax.dot_general(stack(qi, n), s_in[n].astype(BF16), NT_DIMS, preferred_element_type=F32)
        o_ref[chunk_rows(n)] = jnp.concatenate([o[:c], o[c:]], axis=1)
    return s_t


def _gla_kernel(*refs, with_output):
    (qf, kf, vf, lf, qb, kb, vb, lb, wgf, wgb, bgf, bgb, s0_ref) = refs[:13]
    if with_output:
        of_ref, ob_ref, s_scr, tri_scr = refs[13:]
    else:
        sout_ref, s_scr, tri_scr = refs[13:]
        of_ref = ob_ref = None
    i = pl.program_id(1)
    c = GLA_CHUNK
    rows = qf.shape[0]

    @pl.when((pl.program_id(0) == 0) & (i == 0))
    def _():
        r = lax.broadcasted_iota(I32, (rows, rows), 0)
        cc = lax.broadcasted_iota(I32, (rows, rows), 1)
        shift = c.bit_length() - 1
        same = jnp.right_shift(r, shift) == jnp.right_shift(cc, shift)
        tri_scr[0] = jnp.where(same & (cc <= r), 1.0, 0.0).astype(BF16)
        tri_scr[1] = jnp.where(same & (cc >= r), 1.0, 0.0).astype(BF16)

    @pl.when(i == 0)
    def _():
        s_scr[...] = s0_ref[:, 0]

    s_scr[0] = _gla_block(qf, kf, vf, lf, wgf[...], bgf[...], tri_scr[0], s_scr[0], of_ref, True, with_output)
    s_scr[1] = _gla_block(qb, kb, vb, lb, wgb[...], bgb[...], tri_scr[1], s_scr[1], ob_ref, False, with_output)
    if not with_output:
        @pl.when(i == pl.num_programs(1) - 1)
        def _():
            sout_ref[:, 0] = s_scr[...]


def _gla(proj, low, wg_f, wg_b, bg_f, bg_b, s0, with_output, col_q, col_k, col_v):
    t = proj.shape[0]
    cb = _tile(t // GLA_CHUNK, 8, 4, 2, 1)
    rows = cb * GLA_CHUNK
    nb = t // rows
    npairs = wg_f.shape[1] // (2 * GLA_HK)
    pw = 2 * GLA_HK
    vw = 2 * GLA_HV
    fi = lambda p, i: i
    bi = lambda p, i: nb - 1 - i

    def spec(width, col0, blk):
        return pl.BlockSpec((rows, width), lambda p, i: (blk(p, i), col0 // width + p))

    def low_spec(blk):
        return pl.BlockSpec((rows, LANES), lambda p, i: (blk(p, i), 0))

    in_specs = [spec(pw, col_q, fi), spec(pw, col_k, fi), spec(vw, col_v, fi), low_spec(fi),
                spec(pw, col_q, bi), spec(pw, col_k, bi), spec(vw, col_v, bi), low_spec(bi),
                pl.BlockSpec((LANES, pw), lambda p, i: (0, p)),
                pl.BlockSpec((LANES, pw), lambda p, i: (0, p)),
                pl.BlockSpec((1, pw), lambda p, i: (0, p)),
                pl.BlockSpec((1, pw), lambda p, i: (0, p)),
                pl.BlockSpec((2, 1, GLA_HV, pw), lambda p, i: (0, p, 0, 0))]
    if with_output:
        out_specs = [pl.BlockSpec((rows, vw), lambda p, i: (i, p)),
                     pl.BlockSpec((rows, vw), lambda p, i: (nb - 1 - i, p))]
        out_shape = [jax.ShapeDtypeStruct((t, npairs * 2 * GLA_HV), F32)] * 2
    else:
        out_specs = pl.BlockSpec((2, 1, GLA_HV, pw), lambda p, i: (0, p, 0, 0))
        out_shape = jax.ShapeDtypeStruct((2, npairs, GLA_HV, pw), F32)
    return pl.pallas_call(
        functools.partial(_gla_kernel, with_output=with_output),
        grid=(npairs, nb),
        in_specs=in_specs,
        out_specs=out_specs,
        out_shape=out_shape,
        scratch_shapes=[pltpu.VMEM((2, GLA_HV, pw), F32), pltpu.VMEM((2, rows, rows), BF16)],
        compiler_params=_params("arbitrary", "arbitrary"),
        name="gla_out" if with_output else "gla_state",
    )(proj, proj, proj, low, proj, proj, proj, low, wg_f, wg_b, bg_f, bg_b, s0)


def _mix_kernel(of_ref, ob_ref, r_ref, ca_ref, cb_ref, gout_ref, wdw_ref, bdw_ref, gln_ref, bln_ref, pick_ref,
                o_ref, upad, y_scr):
    tm = of_ref.shape[0]
    dg = of_ref.shape[1]
    dc = ca_ref.shape[1]
    nrow = tm // GRID_W
    lead = CONV_LEAD
    for h in range(dg // GLA_HV):
        sl = slice(h * GLA_HV, (h + 1) * GLA_HV)
        o = of_ref[:, sl] + ob_ref[:, sl]
        o = o * lax.rsqrt(jnp.mean(o * o, axis=-1, keepdims=True) + EPS) * gout_ref[:, sl]
        o_ref[:, sl] = (o * _silu(r_ref[:, sl])).astype(o_ref.dtype)
    for r in range(nrow):
        rs = slice(r * GRID_W, (r + 1) * GRID_W)
        upad[r, 0:lead, :] = jnp.zeros((lead, dc), F32)
        upad[r, lead:lead + GRID_W, :] = ca_ref[rs, :] * _sigmoid(cb_ref[rs, :])
        upad[r, lead + GRID_W:, :] = jnp.zeros((lead, dc), F32)
    cw = 256
    span = upad.shape[1]
    kpad = pick_ref.shape[1] - CONV_WIDTH * span

    def row_body(r, carry):
        for cc in range(dc // cw):
            cs = slice(cc * cw, (cc + 1) * cw)
            u = upad[r, :, cs]
            taps = [(wdw_ref[j:j + 1, cs] * u).astype(BF16) for j in range(CONV_WIDTH)]
            taps.append(jnp.zeros((kpad, cw), BF16))
            y_scr[r, :, cs] = (jnp.dot(pick_ref[...], jnp.concatenate(taps, axis=0), preferred_element_type=F32)
                               + bdw_ref[:, cs])
        return carry

    lax.fori_loop(0, nrow, row_body, 0)
    for r in range(nrow):
        y = y_scr[r]
        mu = jnp.mean(y, axis=-1, keepdims=True)
        yc = y - mu
        var = jnp.mean(yc * yc, axis=-1, keepdims=True)
        yn = yc * lax.rsqrt(var + EPS) * gln_ref[...] + bln_ref[...]
        o_ref[r * GRID_W:(r + 1) * GRID_W, dg:] = _silu(yn).astype(o_ref.dtype)


def _mix(o_f, o_b, proj, g_out, w_dw, b_dw, g_ln, b_ln, col_r, col_ca, col_cb):
    t, dg = o_f.shape
    tm = _tile(t, 256, 128, GRID_W)
    dc = w_dw.shape[1]
    nrow = tm // GRID_W
    span = GRID_W + 2 * CONV_LEAD
    kdim = -(-CONV_WIDTH * span // LANES) * LANES
    pick = np.zeros((GRID_W, kdim), np.float32)
    for j in range(CONV_WIDTH):
        pick[np.arange(GRID_W), j * span + np.arange(GRID_W) + (CONV_LEAD - CONV_PAD + j)] = 1.0
    pick = jnp.asarray(pick, BF16)
    row = lambda width: pl.BlockSpec((1, width), lambda i: (0, 0))
    return pl.pallas_call(
        _mix_kernel,
        grid=(t // tm,),
        in_specs=[pl.BlockSpec((tm, dg), lambda i: (i, 0)),
                  pl.BlockSpec((tm, dg), lambda i: (i, 0)),
                  pl.BlockSpec((tm, dg), lambda i: (i, col_r // dg)),
                  pl.BlockSpec((tm, dc), lambda i: (i, col_ca // dc)),
                  pl.BlockSpec((tm, dc), lambda i: (i, col_cb // dc)),
                  row(dg),
                  pl.BlockSpec((CONV_WIDTH, dc), lambda i: (0, 0)),
                  row(dc), row(dc), row(dc),
                  pl.BlockSpec((GRID_W, kdim), lambda i: (0, 0))],
        out_specs=pl.BlockSpec((tm, dg + dc), lambda i: (i, 0)),
        out_shape=jax.ShapeDtypeStruct((t, dg + dc), BF16),
        scratch_shapes=[pltpu.VMEM((nrow, span, dc), F32),
                        pltpu.VMEM((nrow, GRID_W, dc), F32)],
        compiler_params=_params("arbitrary"),
        name="mix",
    )(o_f, o_b, proj, proj, proj, g_out, w_dw, b_dw, g_ln, b_ln, pick)


def _outproj_kernel(m_ref, w_ref, x_ref, ga_ref, o_ref):
    o_ref[...] = x_ref[...] + ga_ref[...] * jnp.dot(m_ref[...], w_ref[...], preferred_element_type=F32)


def _outproj(mix, w, x, ga):
    t, k = mix.shape
    n = w.shape[1]
    tm = _tile(t, 1024, 512, 256)
    tn = _tile(n, 1024, 512, 256, LANES)
    return pl.pallas_call(
        _outproj_kernel,
        grid=(t // tm, n // tn),
        in_specs=[pl.BlockSpec((tm, k), lambda i, j: (i, 0)),
                  pl.BlockSpec((k, tn), lambda i, j: (0, j)),
                  pl.BlockSpec((None, tm, tn), lambda i, j: (0, i, j)),
                  pl.BlockSpec((1, tn), lambda i, j: (0, j))],
        out_specs=pl.BlockSpec((tm, tn), lambda i, j: (i, j)),
        out_shape=jax.ShapeDtypeStruct((t, n), F32),
        compiler_params=_params("arbitrary", "arbitrary"),
        name="out_proj",
    )(mix, w, x, ga)


def _pack_halves(h):
    c = h.shape[1] // 2
    lo = lax.bitcast_convert_type(h[:, :c].astype(BF16).astype(F32), U32)
    hi = lax.bitcast_convert_type(h[:, c:].astype(BF16).astype(F32), U32)
    return (lo >> 16) | (hi & jnp.uint32(0xFFFF0000))


def _unpack_halves(p):
    lo = lax.bitcast_convert_type(p << 16, F32).astype(BF16)
    hi = lax.bitcast_convert_type(p & jnp.uint32(0xFFFF0000), F32).astype(BF16)
    return jnp.concatenate([lo, hi], axis=1)


def _route_kernel(x_ref, g_ref, sh_ref, sc_ref, wr_ref, br_ref, hp_ref, e_ref, w_ref, rk_ref, cnt_ref,
                  carry):
    tm = x_ref.shape[0]
    ne = N_EXPERTS
    gs = ne // N_GROUPS
    neg = -jnp.inf

    @pl.when(pl.program_id(0) == 0)
    def _():
        carry[...] = jnp.zeros_like(carry)

    h = _norm_mod(x_ref[...], g_ref[...], sh_ref[...], sc_ref[...])
    hp_ref[...] = _pack_halves(h)
    logits = lax.dot_general(wr_ref[...], h, NT_DIMS, preferred_element_type=F32,
                             precision=lax.Precision.HIGHEST)
    scores = _sigmoid(logits)
    sel = scores + br_ref[...]
    sel3 = sel.reshape(N_GROUPS, gs, tm)
    sub = lax.broadcasted_iota(I32, (N_GROUPS, gs, tm), 1)
    m1 = jnp.max(sel3, axis=1, keepdims=True)
    i1 = jnp.min(jnp.where(sel3 == m1, sub, gs), axis=1, keepdims=True)
    m2 = jnp.max(jnp.where(sub == i1, neg, sel3), axis=1, keepdims=True)
    gscore = (m1 + m2).reshape(N_GROUPS, tm)
    gid = lax.broadcasted_iota(I32, (N_GROUPS, tm), 0)
    gmask = jnp.zeros((N_GROUPS, tm), F32)
    for _ in range(TOPK_GROUPS):
        mx = jnp.max(gscore, axis=0, keepdims=True)
        idx = jnp.min(jnp.where(gscore == mx, gid, N_GROUPS), axis=0, keepdims=True)
        pick = gid == idx
        gmask = jnp.where(pick, 1.0, gmask)
        gscore = jnp.where(pick, neg, gscore)
    emask = jnp.broadcast_to(gmask.reshape(N_GROUPS, 1, tm), (N_GROUPS, gs, tm)).reshape(ne, tm)
    cand = jnp.where(emask > 0.0, sel, neg)
    eid = lax.broadcasted_iota(I32, (ne, tm), 0)
    chosen = jnp.zeros((ne, tm), F32)
    idxs, scs = [], []
    for _ in range(TOP_K):
        mx = jnp.max(cand, axis=0, keepdims=True)
        idx = jnp.min(jnp.where(cand == mx, eid, ne), axis=0, keepdims=True)
        pick = eid == idx
        idxs.append(idx)
        scs.append(jnp.sum(jnp.where(pick, scores, 0.0), axis=0, keepdims=True))
        chosen = jnp.where(pick, 1.0, chosen)
        cand = jnp.where(pick, neg, cand)
    ssum = scs[0]
    for s in scs[1:]:
        ssum = ssum + s
    before = (lax.broadcasted_iota(I32, (tm, tm), 0) < lax.broadcasted_iota(I32, (tm, tm), 1))
    prior = jnp.dot(chosen.astype(BF16), before.astype(BF16), preferred_element_type=F32) + carry[...]
    for k in range(TOP_K):
        e_ref[k:k + 1, :] = idxs[k]
        w_ref[k:k + 1, :] = scs[k] / ssum * ROUTED_SCALE
        rk = jnp.sum(jnp.where(eid == idxs[k], prior, 0.0), axis=0, keepdims=True)
        rk_ref[k:k + 1, :] = rk.astype(I32)
    carry[...] = carry[...] + jnp.sum(chosen, axis=1, keepdims=True)
    cnt_ref[...] = jnp.broadcast_to(carry[...], cnt_ref.shape).astype(I32)


def _route(x, g, sh, sc, w_router_t, b_router):
    t, d = x.shape
    tm = _tile(t, 256, LANES)
    vec = pl.BlockSpec((1, d), lambda i: (0, 0))
    tok = lambda dt: jax.ShapeDtypeStruct((TOP_K, t), dt)
    tok_spec = pl.BlockSpec((TOP_K, tm), lambda i: (0, i))
    return pl.pallas_call(
        _route_kernel,
        grid=(t // tm,),
        in_specs=[pl.BlockSpec((tm, d), lambda i: (i, 0)), vec, vec, vec,
                  pl.BlockSpec((N_EXPERTS, d), lambda i: (0, 0)),
                  pl.BlockSpec((N_EXPERTS, 1), lambda i: (0, 0))],
        out_specs=[pl.BlockSpec((tm, d // 2), lambda i: (i, 0)), tok_spec, tok_spec, tok_spec,
                   pl.BlockSpec((N_EXPERTS, LANES), lambda i: (0, 0))],
        out_shape=[jax.ShapeDtypeStruct((t, d // 2), U32), tok(I32), tok(F32), tok(I32),
                   jax.ShapeDtypeStruct((N_EXPERTS, LANES), I32)],
        scratch_shapes=[pltpu.VMEM((N_EXPERTS, 1), F32)],
        compiler_params=_params("arbitrary"),
        name="route",
    )(x, g, sh, sc, w_router_t, b_router)


WUNITS = 8
WSLOTS = 6
XSLOTS = 6


def _experts_kernel(blk_run, blk_goal, run_e, meta, slot_tok, hp_hbm, wg_hbm, wu_hbm, wd_hbm, o_ref,
                    xbuf, sem, sg, su, sd, wsem, wgb, wub, wdb, done_ref):
    b = pl.program_id(0)
    nb = pl.num_programs(0)
    nused = meta[0]
    nunits = meta[1] * WUNITS
    ug = sg.shape[1]
    ud = sd.shape[1]

    def unit_copies(u):
        e = run_e[u // WUNITS]
        k = u % WUNITS
        s = u % WSLOTS
        return (pltpu.make_async_copy(wg_hbm.at[e, pl.ds(k * ug, ug)], sg.at[s], wsem.at[s]),
                pltpu.make_async_copy(wu_hbm.at[e, pl.ds(k * ug, ug)], su.at[s], wsem.at[s]),
                pltpu.make_async_copy(wd_hbm.at[e, pl.ds(k * ud, ud)], sd.at[s], wsem.at[s]))

    def start_unit(u):
        for cp in unit_copies(u):
            cp.start(priority=1)

    def process_unit(u):
        for cp in unit_copies(u):
            cp.wait()

        @pl.when(u + WSLOTS - 1 < nunits)
        def _():
            start_unit(u + WSLOTS - 1)

        r = (u // WUNITS) % 2
        k = u % WUNITS
        s = u % WSLOTS
        wgb[r, pl.ds(pl.multiple_of(k * ug, ug), ug), :] = sg[s].astype(BF16)
        wub[r, pl.ds(pl.multiple_of(k * ug, ug), ug), :] = su[s].astype(BF16)
        wdb[r, pl.ds(pl.multiple_of(k * ud, ud), ud), :] = sd[s].astype(BF16)

    def start_gather(blk, slot, rows=range(MOE_BLOCK)):
        for r in rows:
            tok = slot_tok[blk * MOE_BLOCK + r]
            pltpu.make_async_copy(hp_hbm.at[pl.ds(tok, 1)], xbuf.at[slot, pl.ds(r, 1)],
                                  sem.at[slot]).start(priority=0)

    def wait_gather(slot):
        pltpu.make_async_copy(hp_hbm.at[pl.ds(0, MOE_BLOCK)], xbuf.at[slot], sem.at[slot]).wait()

    live = b < nused
    slot = lax.rem(b, XSLOTS)

    @pl.when((b == 0) & live)
    def _():
        done_ref[0] = 0
        for u in range(WSLOTS - 1):
            @pl.when(u < nunits)
            def _():
                start_unit(u)
        for blk in range(XSLOTS - 1):
            @pl.when(blk < nused)
            def _():
                start_gather(blk, blk)

    @pl.when(live)
    def _():
        done = done_ref[0]
        goal = blk_goal[b]

        def body(n, carry):
            process_unit(done + n)
            return carry
        lax.fori_loop(0, jnp.maximum(goal - done, 0), body, 0)
        done_ref[0] = jnp.maximum(goal, done)

    ahead = b + XSLOTS - 1
    ahead_live = (ahead < nb) & (ahead < nused)

    def compute(m, gather_ahead):
        r = blk_run[b] % 2
        half = xbuf.shape[2]
        group = MOE_BLOCK // 4

        def issue(q):
            if gather_ahead:
                start_gather(ahead, (m + XSLOTS - 1) % XSLOTS, range(q * group, (q + 1) * group))

        wait_gather(m)
        x = _unpack_halves(xbuf[m])
        issue(0)
        hg = jnp.dot(x, wgb[r], preferred_element_type=F32)
        issue(1)
        hu = jnp.dot(x, wub[r], preferred_element_type=F32)
        issue(2)
        hb = (_silu(hg) * hu).astype(BF16)
        for c in range(2):
            lo = slice(c * half // 2, (c + 1) * half // 2)
            hi = slice(half + c * half // 2, half + (c + 1) * half // 2)
            o = jnp.concatenate([jnp.dot(hb, wdb[r, :, lo], preferred_element_type=F32),
                                 jnp.dot(hb, wdb[r, :, hi], preferred_element_type=F32)], axis=1)
            o_ref[:, lo] = _pack_halves(o)
            if c == 0:
                issue(3)

    for m in range(XSLOTS):
        mine = live & (slot == m)
        pl.when(mine & ahead_live)(functools.partial(compute, m, True))
        pl.when(mine & jnp.logical_not(ahead_live))(functools.partial(compute, m, False))

    @pl.when(jnp.logical_not(live))
    def _():
        o_ref[...] = jnp.zeros_like(o_ref)


def _experts(blk_run, blk_goal, run_e, meta, slot_tok, hp, wg, wu, wd):
    nb = blk_run.shape[0]
    _, d, de = wg.shape
    ug, ud = d // WUNITS, de // WUNITS
    hbm = pl.BlockSpec(memory_space=pl.ANY)
    grid_spec = pltpu.PrefetchScalarGridSpec(
        num_scalar_prefetch=5,
        grid=(nb,),
        in_specs=[hbm, hbm, hbm, hbm],
        out_specs=pl.BlockSpec((MOE_BLOCK, d // 2), lambda b, *_: (b, 0)),
        scratch_shapes=[pltpu.VMEM((XSLOTS, MOE_BLOCK, d // 2), U32), pltpu.SemaphoreType.DMA((XSLOTS,)),
                        pltpu.VMEM((WSLOTS, ug, de), F32), pltpu.VMEM((WSLOTS, ug, de), F32),
                        pltpu.VMEM((WSLOTS, ud, d), F32), pltpu.SemaphoreType.DMA((WSLOTS,)),
                        pltpu.VMEM((2, d, de), BF16), pltpu.VMEM((2, d, de), BF16),
                        pltpu.VMEM((2, de, d), BF16), pltpu.SMEM((1,), I32)],
    )
    return pl.pallas_call(
        _experts_kernel,
        grid_spec=grid_spec,
        out_shape=jax.ShapeDtypeStruct((nb * MOE_BLOCK, d // 2), U32),
        compiler_params=_params("arbitrary"),
        name="experts",
    )(blk_run, blk_goal, run_e, meta, slot_tok, hp, wg, wu, wd)


def _shared_kernel(hp_ref, wg_ref, wu_ref, wd_ref, x_ref, ga_ref, o_ref):
    x = _unpack_halves(hp_ref[...])
    hg = jnp.dot(x, wg_ref[...], preferred_element_type=F32)
    hu = jnp.dot(x, wu_ref[...], preferred_element_type=F32)
    hb = (_silu(hg) * hu).astype(BF16)
    o_ref[...] = x_ref[...] + ga_ref[...] * jnp.dot(hb, wd_ref[...], preferred_element_type=F32)


def _shared(hp, wg, wu, wd, x, ga):
    t, d = x.shape
    tm = _tile(t, 256, LANES)
    ds_ = wg.shape[1]
    return pl.pallas_call(
        _shared_kernel,
        grid=(t // tm,),
        in_specs=[pl.BlockSpec((tm, d // 2), lambda i: (i, 0)),
                  pl.BlockSpec((d, ds_), lambda i: (0, 0)),
                  pl.BlockSpec((d, ds_), lambda i: (0, 0)),
                  pl.BlockSpec((ds_, d), lambda i: (0, 0)),
                  pl.BlockSpec((tm, d), lambda i: (i, 0)),
                  pl.BlockSpec((1, d), lambda i: (0, 0))],
        out_specs=pl.BlockSpec((tm, d), lambda i: (i, 0)),
        out_shape=jax.ShapeDtypeStruct((t, d), F32),
        compiler_params=_params("arbitrary"),
        name="shared",
    )(hp, wg, wu, wd, x, ga)


def _combine_kernel(dest, ys_hbm, w_ref, base_ref, ga_ref, gf_ref, o_ref, buf_a, buf_b, sem):
    i = pl.program_id(0)
    n = pl.num_programs(0)
    tm = buf_a.shape[1]
    half = buf_a.shape[2]
    d = 2 * half

    def start_gather(tile, buf, s):
        for j in range(tm * TOP_K):
            src = dest[tile * (tm * TOP_K) + j]
            pltpu.make_async_copy(ys_hbm.at[pl.ds(src, 1)], buf.at[j % TOP_K, pl.ds(j // TOP_K, 1)],
                                  sem.at[s]).start(priority=j % 2)

    def wait_gather(buf, s):
        for k in range(TOP_K):
            pltpu.make_async_copy(ys_hbm.at[pl.ds(0, tm)], buf.at[k], sem.at[s]).wait()

    def finish(buf, rows):
        ylo = jnp.zeros((tm, half), F32)
        yhi = jnp.zeros((tm, half), F32)
        for k in range(TOP_K):
            p = buf[k]
            w = w_ref[rows, k:k + 1]
            ylo = ylo + w * lax.bitcast_convert_type(p << 16, F32)
            yhi = yhi + w * lax.bitcast_convert_type(p & jnp.uint32(0xFFFF0000), F32)
        xlo = base_ref[rows, :half] + ga_ref[:, :half] * ylo
        xhi = base_ref[rows, half:] + ga_ref[:, half:] * yhi
        ms = (jnp.sum(xlo * xlo, axis=-1, keepdims=True) + jnp.sum(xhi * xhi, axis=-1, keepdims=True)) / d
        inv = lax.rsqrt(ms + EPS)
        o_ref[rows, :half] = xlo * inv * gf_ref[:, :half]
        o_ref[rows, half:] = xhi * inv * gf_ref[:, half:]

    def step(gather_next):
        wait_gather(buf_a, 0)
        start_gather(2 * i + 1, buf_b, 1)
        finish(buf_a, slice(0, tm))
        wait_gather(buf_b, 1)
        if gather_next:
            start_gather(2 * i + 2, buf_a, 0)
        finish(buf_b, slice(tm, 2 * tm))

    @pl.when(i == 0)
    def _():
        start_gather(0, buf_a, 0)

    pl.when(i + 1 < n)(functools.partial(step, True))
    pl.when(i + 1 == n)(functools.partial(step, False))


def _combine(dest, ys, wts, base, ga, g_final):
    t, d = base.shape
    tm = 64
    assert t % (2 * tm) == 0
    grid_spec = pltpu.PrefetchScalarGridSpec(
        num_scalar_prefetch=1,
        grid=(t // (2 * tm),),
        in_specs=[pl.BlockSpec(memory_space=pl.ANY),
                  pl.BlockSpec((2 * tm, TOP_K), lambda i, s: (i, 0)),
                  pl.BlockSpec((2 * tm, d), lambda i, s: (i, 0)),
                  pl.BlockSpec((1, d), lambda i, s: (0, 0)),
                  pl.BlockSpec((1, d), lambda i, s: (0, 0))],
        out_specs=pl.BlockSpec((None, 2 * tm, d), lambda i, s: (0, i, 0)),
        scratch_shapes=[pltpu.VMEM((TOP_K, tm, d // 2), U32), pltpu.VMEM((TOP_K, tm, d // 2), U32),
                        pltpu.SemaphoreType.DMA((2,))],
    )
    return pl.pallas_call(
        _combine_kernel,
        grid_spec=grid_spec,
        out_shape=jax.ShapeDtypeStruct((1, t, d), F32),
        compiler_params=_params("arbitrary"),
        name="combine",
    )(dest, ys, wts, base, ga, g_final)


def kernel(x, c, ctx, c_ctx, w_ada, b_ada, g_norm_mix, g_norm_ffn, w_in, w_gate_up, b_gate_up,
           g_gla_out, w_dw, b_dw, g_conv_ln, b_conv_ln, w_out, w_router, b_router,
           w_e_gate, w_e_up, w_e_down, w_s_gate, w_s_up, w_s_down, g_final):
    assert x.shape[0] == 1 and w_ada.shape[0] == 1, "single batch element, single layer"
    t, d = x.shape[1], x.shape[2]
    dk = w_gate_up.shape[3]
    heads = dk // GLA_HK
    dg = heads * GLA_HV
    dc = w_dw.shape[2]

    col_q, col_k, col_v = 0, dk, 2 * dk
    col_r = col_v + dg
    col_ca = col_r + dg
    col_cb = col_ca + dc
    n_gate = 2 * GATE_RANK
    w16 = w_in[0].astype(BF16)
    w_b = w16[:, col_ca + n_gate:]
    w_low = jnp.pad(w16[:, col_ca:col_ca + n_gate], ((0, 0), (0, LANES - n_gate)))

    mod = _ada(jnp.stack([c[0], c_ctx]), w_ada[0], b_ada[0])
    sh_m, sc_m, ga_m, sh_f, sc_f, ga_f = [mod[0:1, k * d:(k + 1) * d] for k in range(6)]
    csh_m, csc_m = mod[1:2, 0:d], mod[1:2, d:2 * d]

    wg_f = jnp.zeros((LANES, dk), F32).at[:GATE_RANK].set(w_gate_up[0, 0]).astype(BF16)
    wg_b = jnp.zeros((LANES, dk), F32).at[GATE_RANK:n_gate].set(w_gate_up[0, 1]).astype(BF16)
    bg_f = b_gate_up[0, 0:1]
    bg_b = b_gate_up[0, 1:2]
    cols = dict(col_q=col_q, col_k=col_k, col_v=col_v)

    gn_mix = g_norm_mix[0:1]
    proj_c, low_c = _proj(ctx, gn_mix, csh_m, csc_m, w_low, w16, col_ca, w_b)
    s0 = jnp.zeros((2, heads // 2, GLA_HV, 2 * GLA_HK), F32)
    s_ctx = _gla(proj_c, low_c, wg_f, wg_b, bg_f, bg_b, s0, with_output=False, **cols)

    proj_l, low_l = _proj(x, gn_mix, sh_m, sc_m, w_low, w16, col_ca, w_b)
    o_f, o_b = _gla(proj_l, low_l, wg_f, wg_b, bg_f, bg_b, s_ctx, with_output=True, **cols)
    mix = _mix(o_f, o_b, proj_l, g_gla_out[0:1], w_dw[0], b_dw[0:1], g_conv_ln[0:1], b_conv_ln[0:1],
               col_r, col_ca, col_cb)
    x1 = _outproj(mix, w_out[0].astype(BF16), x, ga_m)

    hp, eidx, wts, rank, cnt = _route(x1, g_norm_ffn[0:1], sh_f, sc_f, w_router[0].T,
                                      b_router[0].reshape(N_EXPERTS, 1))
    nk = t * TOP_K
    nb = (nk + MOE_BLOCK - 1) // MOE_BLOCK + N_EXPERTS
    counts = cnt[:, 0]
    pcounts = (counts + MOE_BLOCK - 1) // MOE_BLOCK * MOE_BLOCK
    pend = jnp.cumsum(pcounts)
    pstart = pend - pcounts
    experts = jnp.arange(N_EXPERTS, dtype=I32)
    pstart_tok = jnp.sum(jnp.where(eidx[:, :, None] == experts, pstart, 0), axis=-1)
    dest = (pstart_tok + rank).T.reshape(-1).astype(I32)
    tok_ids = jnp.repeat(jnp.arange(t, dtype=I32), TOP_K)
    slot_tok = jnp.zeros((nb * MOE_BLOCK,), I32).at[dest].set(tok_ids, unique_indices=True,
                                                               mode="promise_in_bounds")
    blk_start = jnp.arange(nb, dtype=I32) * MOE_BLOCK
    blk_e = jnp.minimum(jnp.sum(pend[None, :] <= blk_start[:, None], axis=1), N_EXPERTS - 1).astype(I32)
    owns = counts > 0
    run_of_e = jnp.cumsum(owns.astype(I32)) - 1
    run_e = jnp.sum(jnp.where(owns[None, :] & (run_of_e[None, :] == experts[:, None]), experts[None, :], 0),
                    axis=1).astype(I32)
    onehot = blk_e[:, None] == experts[None, :]
    blk_run = jnp.sum(jnp.where(onehot, run_of_e[None, :], 0), axis=1).astype(I32)
    nruns = jnp.sum(owns.astype(I32))
    meta = jnp.stack([pend[-1] // MOE_BLOCK, nruns]).astype(I32)
    blk_first = jnp.sum(jnp.where(onehot, pstart[None, :], 0), axis=1) // MOE_BLOCK
    run_len = jnp.maximum(jnp.sum(jnp.where(onehot, pcounts[None, :], 0), axis=1) // MOE_BLOCK, 1)
    pos = jnp.arange(nb, dtype=I32) - blk_first
    share = (WUNITS * (pos + 1) + run_len - 1) // run_len
    blk_goal = jnp.minimum(WUNITS * (blk_run + 1) + share, WUNITS * nruns).astype(I32)

    ys = _experts(blk_run, blk_goal, run_e, meta, slot_tok, hp, w_e_gate[0], w_e_up[0], w_e_down[0])
    base = _shared(hp, w_s_gate[0].astype(BF16), w_s_up[0].astype(BF16), w_s_down[0].astype(BF16), x1, ga_f)
    return _combine(dest, ys, wts.T, base, ga_f, g_final.reshape(1, d))
```
